```python
import math
import jax, jax.numpy as jnp
from jax import lax
import numpy as np

D_MODEL = 2048
BATCH = 4
SEQ = 2048
DEPTH = 2

HEAD_DIM = 128
N_HEADS_TOTAL = D_MODEL // HEAD_DIM
MIX_WIDTH = N_HEADS_TOTAL * HEAD_DIM
N_HEADS_A = (3 * N_HEADS_TOTAL) // 8
N_HEADS_B = N_HEADS_TOTAL // 4
N_HEADS_C = N_HEADS_TOTAL - N_HEADS_A - N_HEADS_B
DIFF_QK_DIM = HEAD_DIM // 2
PROJ_WIDTH = 3 * MIX_WIDTH
DILATED_PATTERNS = ((128, 1), (512, 4), (2048, 16))
DILATED_BLOCK = 128
BLOCK = 128
D_FF = 4 * D_MODEL
NUM_BUCKETS = 32
MAX_DISTANCE = 2048
EPS = 1e-6
NEG_INF = -1e30

kernel_name = "hybrid_dilated_diff_stickbreaking_block"


def rmsnorm(x, g):
    xf = x.astype(jnp.float32)
    y = xf * lax.rsqrt(jnp.mean(xf * xf, axis=-1, keepdims=True) + EPS)
    return y.astype(x.dtype) * g


def t5_bucket(dist):
    n = jnp.maximum(dist, 0)
    max_exact = NUM_BUCKETS // 2
    nf = jnp.maximum(n, max_exact).astype(jnp.float32)
    large = max_exact + (jnp.log(nf / max_exact) / math.log(MAX_DISTANCE / max_exact)
                         * (NUM_BUCKETS - max_exact)).astype(jnp.int32)
    large = jnp.minimum(large, NUM_BUCKETS - 1)
    return jnp.where(n < max_exact, n, large)


def dilated_pattern(q, k, v, table, window, dil):
    B, S, H, D = q.shape
    L = S // dil
    nk = window // dil
    blk = DILATED_BLOCK
    nb = -(-L // blk)
    Lp = nb * blk
    scale = 1.0 / math.sqrt(D)

    def to_strided(t):
        t = jnp.swapaxes(t.reshape(B, L, dil, H, D), 1, 2)
        return jnp.pad(t, ((0, 0), (0, 0), (0, Lp - L), (0, 0), (0, 0)))

    def band(t):
        ts = jnp.pad(to_strided(t), ((0, 0), (0, 0), (blk, 0), (0, 0), (0, 0)))
        ts = ts.reshape(B, dil, nb + 1, blk, H, D)
        return jnp.concatenate([ts[:, :, :-1], ts[:, :, 1:]], axis=3)

    qs = to_strided(q).reshape(B, dil, nb, blk, H, D)
    kb, vb = band(k), band(v)
    i = jnp.arange(blk)[:, None]
    c = jnp.arange(2 * blk)[None, :]
    steps = i + blk - c
    bidx = jnp.arange(nb)[:, None, None]
    valid = (steps >= 0) & (steps <= nk) & (bidx * blk + c - blk >= 0)
    bias = jnp.moveaxis(table[t5_bucket(steps * dil)], -1, 0).astype(jnp.float32)
    logits = jnp.einsum('brnihd,brnchd->brnhic', qs, kb).astype(jnp.float32) * scale + bias
    logits = jnp.where(valid[:, None], logits, NEG_INF)
    lse = jax.nn.logsumexp(logits, axis=-1)
    p = jnp.exp(logits - lse[..., None])
    out = jnp.einsum('brnhic,brnchd->brnihd', p.astype(v.dtype), vb)

    def from_strided(y):
        y = jnp.swapaxes(y[:, :, :L], 1, 2)
        return y.reshape((B, S) + y.shape[3:])

    out = from_strided(out.reshape(B, dil, Lp, H, D))
    lse = from_strided(jnp.swapaxes(lse, 3, 4).reshape(B, dil, Lp, H))
    return out, lse


def dilated_attention(q, k, v, table):
    outs, lses = [], []
    for window, dil in DILATED_PATTERNS:
        o, l = dilated_pattern(q, k, v, table, window, dil)
        outs.append(o)
        lses.append(l)
    w = jax.nn.softmax(jnp.stack(lses), axis=0)
    return jnp.einsum('pbsh,pbshd->bshd', w.astype(q.dtype), jnp.stack(outs))


def diff_attention(q, k, v, table, lam, lam_init, g):
    B, S, H, _, Dq = q.shape
    nb = S // BLOCK
    scale = 1.0 / math.sqrt(Dq)
    qblocks = jnp.moveaxis(q.reshape(B, nb, BLOCK, H, 2, Dq), 1, 0)
    s_pos = jnp.arange(S)

    def one(args):
        qi, bi = args
        t = bi * BLOCK + jnp.arange(BLOCK)
        dist = t[:, None] - s_pos[None, :]
        bias = jnp.moveaxis(table[t5_bucket(dist)], -1, 0).astype(jnp.float32)
        logits = jnp.einsum('bihpd,bshpd->bphis', qi, k).astype(jnp.float32) * scale + bias
        logits = jnp.where(dist >= 0, logits, NEG_INF)
        p = jax.nn.softmax(logits, axis=-1)
        a = p[:, 0] - lam * p[:, 1]
        return jnp.einsum('bhis,bshd->bihd', a.astype(v.dtype), v)

    out = lax.map(one, (qblocks, jnp.arange(nb)))
    out = jnp.moveaxis(out, 0, 1).reshape(B, S, H, v.shape[-1])
    return rmsnorm(out, g) * (1.0 - lam_init)


def stick_breaking_attention(q, k, v):
    B, S, H, D = q.shape
    nb = S // BLOCK
    scale = 1.0 / math.sqrt(D)
    qblocks = jnp.moveaxis(q.reshape(B, nb, BLOCK, H, D), 1, 0)
    s_pos = jnp.arange(S)

    def one(args):
        qi, bi = args
        t = bi * BLOCK + jnp.arange(BLOCK)
        mask = s_pos[None, :] < t[:, None]
        z = jnp.einsum('bihd,bshd->bhis', qi, k).astype(jnp.float32) * scale
        u = jnp.where(mask, jax.nn.log_sigmoid(-z), 0.0)
        r = lax.cumsum(u, axis=3, reverse=True) - u
        a = jnp.where(mask, jnp.exp(jax.nn.log_sigmoid(z) + r), 0.0)
        return jnp.einsum('bhis,bshd->bihd', a.astype(v.dtype), v)

    out = lax.map(one, (qblocks, jnp.arange(nb)))
    return jnp.moveaxis(out, 0, 1).reshape(B, S, H, D)


def setup_inputs(seed: int = 0) -> dict:
    key = jax.random.key(seed)
    ks = jax.random.split(key, 16)
    f32 = jnp.float32
    return {
        "x": jax.random.normal(ks[0], (BATCH, SEQ, D_MODEL), f32),
        "w_in": jax.random.normal(ks[1], (DEPTH, D_MODEL, PROJ_WIDTH), f32) * D_MODEL ** -0.5,
        "w_out": jax.random.normal(ks[2], (DEPTH, MIX_WIDTH, D_MODEL), f32) * MIX_WIDTH ** -0.5,
        "g_attn": 1.0 + 0.02 * jax.random.normal(ks[3], (DEPTH, D_MODEL), f32),
        "g_mlp": 1.0 + 0.02 * jax.random.normal(ks[4], (DEPTH, D_MODEL), f32),
        "w_mlp_in": jax.random.normal(ks[5], (DEPTH, D_MODEL, D_FF), f32) * D_MODEL ** -0.5,
        "w_mlp_out": jax.random.normal(ks[6], (DEPTH, D_FF, D_MODEL), f32) * D_FF ** -0.5,
        "rel_bias_table": 0.5 * jax.random.normal(ks[7], (NUM_BUCKETS, N_HEADS_A + N_HEADS_B), f32),
        "diff_lam_q1": 0.1 * jax.random.normal(ks[8], (DEPTH, DIFF_QK_DIM), f32),
        "diff_lam_k1": 0.1 * jax.random.normal(ks[9], (DEPTH, DIFF_QK_DIM), f32),
        "diff_lam_q2": 0.1 * jax.random.normal(ks[10], (DEPTH, DIFF_QK_DIM), f32),
        "diff_lam_k2": 0.1 * jax.random.normal(ks[11], (DEPTH, DIFF_QK_DIM), f32),
        "diff_subln_g": 1.0 + 0.02 * jax.random.normal(ks[12], (DEPTH, HEAD_DIM), f32),
        "g_final": 1.0 + 0.02 * jax.random.normal(ks[13], (D_MODEL,), f32),
    }


def reference(x, w_in, w_out, g_attn, g_mlp, w_mlp_in, w_mlp_out, rel_bias_table,
              diff_lam_q1, diff_lam_k1, diff_lam_q2, diff_lam_k2, diff_subln_g, g_final):
    B, S, _ = x.shape
    table_a = rel_bias_table[:, :N_HEADS_A]
    table_b = rel_bias_table[:, N_HEADS_A:]
    split_a = 3 * N_HEADS_A * HEAD_DIM
    split_b = split_a + 3 * N_HEADS_B * HEAD_DIM
    for l in range(DEPTH):
        h = rmsnorm(x, g_attn[l])
        proj = h @ w_in[l]
        part_a, part_b, part_c = jnp.split(proj, [split_a, split_b], axis=-1)
        qa, ka, va = [t.reshape(B, S, N_HEADS_A, HEAD_DIM) for t in jnp.split(part_a, 3, axis=-1)]
        qb, kb, vb = jnp.split(part_b, 3, axis=-1)
        qb = qb.reshape(B, S, N_HEADS_B, 2, DIFF_QK_DIM)
        kb = kb.reshape(B, S, N_HEADS_B, 2, DIFF_QK_DIM)
        vb = vb.reshape(B, S, N_HEADS_B, HEAD_DIM)
        qc, kc, vc = [t.reshape(B, S, N_HEADS_C, HEAD_DIM) for t in jnp.split(part_c, 3, axis=-1)]

        lam_init = 0.8 - 0.6 * math.exp(-0.3 * l)
        lam = (jnp.exp(jnp.sum(diff_lam_q1[l].astype(jnp.float32) * diff_lam_k1[l].astype(jnp.float32)))
               - jnp.exp(jnp.sum(diff_lam_q2[l].astype(jnp.float32) * diff_lam_k2[l].astype(jnp.float32)))
               + lam_init)

        out_a = dilated_attention(qa, ka, va, table_a)
        out_b = diff_attention(qb, kb, vb, table_b, lam, lam_init, diff_subln_g[l])
        out_c = stick_breaking_attention(qc, kc, vc)
        mixed = jnp.concatenate([out_a.reshape(B, S, -1), out_b.reshape(B, S, -1),
                                 out_c.reshape(B, S, -1)], axis=-1)
        x = x + mixed @ w_out[l]
        h = rmsnorm(x, g_mlp[l])
        x = x + jnp.square(jax.nn.relu(h @ w_mlp_in[l])) @ w_mlp_out[l]
    return rmsnorm(x, g_final)
```

```python
import functools
import math

import jax
import jax.numpy as jnp
import numpy as np
from jax import lax
from jax.experimental import pallas as pl
from jax.experimental.pallas import tpu as pltpu

HEAD_DIM = 128
N_HEADS_A = 6
N_HEADS_B = 4
N_HEADS_C = 6
DILATED_PATTERNS = ((128, 1), (512, 4), (2048, 16))
BLK = 128
NUM_BUCKETS = 32
MAX_DISTANCE = 2048
EPS = 1e-6
NEG_INF = -1e30

F32 = jnp.float32
BF16 = jnp.bfloat16

_QA, _KA, _VA = 0, N_HEADS_A, 2 * N_HEADS_A
_QB = 3 * N_HEADS_A
_KB, _VB = _QB + N_HEADS_B, _QB + 2 * N_HEADS_B
_QC = _QB + 3 * N_HEADS_B
_KC, _VC = _QC + N_HEADS_C, _QC + 2 * N_HEADS_C

_VMEM_LIMIT = 56 * 1024 * 1024


def _cparams(sem):
    return pltpu.CompilerParams(dimension_semantics=sem, vmem_limit_bytes=_VMEM_LIMIT)


def _dot_nt(a, b):
    return lax.dot_general(a, b, (((1,), (1,)), ((), ())), preferred_element_type=F32)


def _dot(a, b):
    return jnp.dot(a, b, preferred_element_type=F32)


def _t5_bucket_np(dist):
    n = np.maximum(dist, 0)
    max_exact = NUM_BUCKETS // 2
    nf = np.maximum(n, max_exact).astype(np.float32)
    large = max_exact + (np.log(nf / np.float32(max_exact)) / np.float32(math.log(MAX_DISTANCE / max_exact))
                         * np.float32(NUM_BUCKETS - max_exact)).astype(np.int32)
    large = np.minimum(large, NUM_BUCKETS - 1)
    return np.where(n < max_exact, n, large).astype(np.int32)


def _bias_kernel(tab_ref, bkt_ref, o_ref, *, n_heads):
    b = bkt_ref[0]
    for h in range(n_heads):
        acc = jnp.zeros(b.shape, F32)
        for k in range(NUM_BUCKETS):
            acc = jnp.where(b == k, tab_ref[k, h], acc)
        o_ref[h, 0] = acc


def _build_bias(table, buckets):
    n_heads = table.shape[1]
    n, r, c = buckets.shape
    return pl.pallas_call(
        functools.partial(_bias_kernel, n_heads=n_heads),
        grid=(n,),
        in_specs=[pl.BlockSpec(memory_space=pltpu.SMEM),
                  pl.BlockSpec((1, r, c), lambda i: (i, 0, 0))],
        out_specs=pl.BlockSpec((n_heads, 1, r, c), lambda i: (0, i, 0, 0)),
        out_shape=jax.ShapeDtypeStruct((n_heads, n, r, c), F32),
        compiler_params=_cparams(("arbitrary",)),
        name="bias_table",
    )(table, buckets)


def _rms_proj_kernel(x_ref, g_ref, w_ref, cs_ref, o_ref, xn_ref):
    @pl.when(pl.program_id(1) == 0)
    def _():
        x = x_ref[...]
        ms = jnp.mean(x * x, axis=-1, keepdims=True)
        xn_ref[...] = (x * lax.rsqrt(ms + EPS) * g_ref[...]).astype(BF16)

    acc = _dot(xn_ref[...], w_ref[...])
    o_ref[...] = (acc * cs_ref[...]).astype(o_ref.dtype)


def _rms_proj(x, g, w, colscale, *, tm=512, tn=1024):
    t, d = x.shape
    n = w.shape[1]
    return pl.pallas_call(
        _rms_proj_kernel,
        grid=(t // tm, n // tn),
        in_specs=[pl.BlockSpec((tm, d), lambda i, j: (i, 0)),
                  pl.BlockSpec((1, d), lambda i, j: (0, 0)),
                  pl.BlockSpec((d, tn), lambda i, j: (0, j)),
                  pl.BlockSpec((1, tn), lambda i, j: (0, j))],
        out_specs=pl.BlockSpec((tm, tn), lambda i, j: (i, j)),
        out_shape=jax.ShapeDtypeStruct((t, n), BF16),
        scratch_shapes=[pltpu.VMEM((tm, d), BF16)],
        compiler_params=_cparams(("parallel", "arbitrary")),
        name="rms_proj",
    )(x, g, w, colscale)


def _dilated_kernel(q_ref, k_ref, v_ref, bias_ref, o_ref, qf, kf, vf, acc_s, m_s, l_s, *, seq):
    qf[...] = q_ref[...].astype(F32)
    kf[...] = k_ref[...].astype(F32)
    vf[...] = v_ref[...].astype(F32)

    row = lax.broadcasted_iota(jnp.int32, (BLK, 2 * BLK), 0)
    col = lax.broadcasted_iota(jnp.int32, (BLK, 2 * BLK), 1)
    band_mask = (col >= row) & (col <= row + BLK)
    row0 = lax.broadcasted_iota(jnp.int32, (BLK, BLK), 0)
    col0 = lax.broadcasted_iota(jnp.int32, (BLK, BLK), 1)
    diag_mask = col0 <= row0

    def softmax_block(logits, vband, p_idx, start, dil):
        m = jnp.max(logits, axis=-1, keepdims=True)
        p = jnp.exp(logits - m)
        l = jnp.sum(p, axis=-1, keepdims=True)
        acc = _dot(p.astype(BF16), vband)
        rows = pl.ds(start, BLK, stride=dil) if dil > 1 else pl.ds(start, BLK)
        acc_s[p_idx, rows, :] = acc
        m_s[p_idx, rows, :] = jnp.broadcast_to(m, (BLK, HEAD_DIM))
        l_s[p_idx, rows, :] = jnp.broadcast_to(l, (BLK, HEAD_DIM))

    def ld(ref, start, size, dil):
        rows = pl.ds(start, size, stride=dil) if dil > 1 else pl.ds(start, size)
        return ref[rows, :].astype(BF16)

    for p_idx, (window, dil) in enumerate(DILATED_PATTERNS):
        assert window // dil == BLK
        nb = seq // dil // BLK
        bias_band = bias_ref[0, p_idx]
        bias_diag = bias_band[:, BLK:]

        def per_residue(r, carry, p_idx=p_idx, dil=dil, nb=nb, bias_band=bias_band, bias_diag=bias_diag):
            qb = ld(qf, r, BLK, dil)
            kb = ld(kf, r, BLK, dil)
            vb = ld(vf, r, BLK, dil)
            logits = jnp.where(diag_mask, _dot_nt(qb, kb) + bias_diag, NEG_INF)
            softmax_block(logits, vb, p_idx, r, dil)

            def per_block(n, c2):
                q_start = r + dil * BLK * n
                k_start = r + dil * BLK * (n - 1)
                qb = ld(qf, q_start, BLK, dil)
                kb = ld(kf, k_start, 2 * BLK, dil)
                vb = ld(vf, k_start, 2 * BLK, dil)
                logits = jnp.where(band_mask, _dot_nt(qb, kb) + bias_band, NEG_INF)
                softmax_block(logits, vb, p_idx, q_start, dil)
                return c2

            if nb > 1:
                lax.fori_loop(1, nb, per_block, 0)
            return carry

        if dil > 1:
            lax.fori_loop(0, dil, per_residue, 0)
        else:
            per_residue(0, 0)

    chunk = 256

    def merge(ci, carry):
        rows = pl.ds(pl.multiple_of(ci * chunk, chunk), chunk)
        m0, m1, m2 = m_s[0, rows, :], m_s[1, rows, :], m_s[2, rows, :]
        mm = jnp.maximum(jnp.maximum(m0, m1), m2)
        w0, w1, w2 = jnp.exp(m0 - mm), jnp.exp(m1 - mm), jnp.exp(m2 - mm)
        num = w0 * acc_s[0, rows, :] + w1 * acc_s[1, rows, :] + w2 * acc_s[2, rows, :]
        den = w0 * l_s[0, rows, :] + w1 * l_s[1, rows, :] + w2 * l_s[2, rows, :]
        o_ref[rows, :] = (num / den).astype(o_ref.dtype)
        return carry

    lax.fori_loop(0, seq // chunk, merge, 0)


def _dilated_attention(proj, bias_a, *, batch, seq):
    t = proj.shape[0]
    n_pat = len(DILATED_PATTERNS)
    blk = lambda off: pl.BlockSpec((seq, HEAD_DIM), lambda b, h: (b, off + h))
    return pl.pallas_call(
        functools.partial(_dilated_kernel, seq=seq),
        grid=(batch, N_HEADS_A),
        in_specs=[blk(_QA), blk(_KA), blk(_VA),
                  pl.BlockSpec((1, n_pat, BLK, 2 * BLK), lambda b, h: (h, 0, 0, 0))],
        out_specs=pl.BlockSpec((seq, HEAD_DIM), lambda b, h: (b, h)),
        out_shape=jax.ShapeDtypeStruct((t, N_HEADS_A * HEAD_DIM), BF16),
        scratch_shapes=[pltpu.VMEM((seq, HEAD_DIM), F32)] * 3
                       + [pltpu.VMEM((n_pat, seq, HEAD_DIM), F32)] * 3,
        compiler_params=_cparams(("parallel", "parallel")),
        name="dilated_attn",
    )(proj, proj, proj, bias_a)


def _diff_kernel(q_ref, k_ref, v_ref, bias_ref, lam_ref, g_ref, o_ref, *, tq, lam_init):
    i = pl.program_id(2)
    half = HEAD_DIM // 2
    q = q_ref[...]
    lane = lax.broadcasted_iota(jnp.int32, (tq, HEAD_DIM), 1)
    zero = jnp.zeros_like(q)
    qq = jnp.concatenate([jnp.where(lane < half, q, zero), jnp.where(lane >= half, q, zero)], axis=0)

    def scores(j):
        kb = k_ref[pl.ds(pl.multiple_of(j * tq, tq), tq), :]
        bias = bias_ref[0, i - j]
        return _dot_nt(qq, kb) + jnp.concatenate([bias, bias], axis=0)

    def update(s, j, carry):
        m, l, acc = carry
        vb = v_ref[pl.ds(pl.multiple_of(j * tq, tq), tq), :]
        m_new = jnp.maximum(m, jnp.max(s, axis=-1, keepdims=True))
        alpha = jnp.exp(m - m_new)
        p = jnp.exp(s - m_new)
        l = alpha * l + jnp.sum(p, axis=-1, keepdims=True)
        acc = alpha * acc + _dot(p.astype(BF16), vb)
        return m_new, l, acc

    def body(j, carry):
        return update(scores(j), j, carry)

    init = (jnp.full((2 * tq, 1), NEG_INF, F32), jnp.zeros((2 * tq, 1), F32),
            jnp.zeros((2 * tq, HEAD_DIM), F32))
    carry = lax.fori_loop(0, i, body, init)
    row = lax.broadcasted_iota(jnp.int32, (2 * tq, tq), 0)
    col = lax.broadcasted_iota(jnp.int32, (2 * tq, tq), 1)
    causal = col <= jnp.where(row >= tq, row - tq, row)
    _, l, acc = update(jnp.where(causal, scores(i), NEG_INF), i, carry)

    lp = lam_ref[...]
    lam = (jnp.exp(jnp.sum(lp[0:1] * lp[1:2], axis=-1, keepdims=True))
           - jnp.exp(jnp.sum(lp[2:3] * lp[3:4], axis=-1, keepdims=True)) + lam_init)
    o = acc / l
    out = o[:tq] - lam * o[tq:]
    ms = jnp.mean(out * out, axis=-1, keepdims=True)
    y = out * lax.rsqrt(ms + EPS) * g_ref[...] * (1.0 - lam_init)
    o_ref[...] = y.astype(o_ref.dtype)


def _diff_attention(proj, bias_b, lam_params, g, *, batch, seq, tq, lam_init):
    t = proj.shape[0]
    nq = seq // tq
    kv = lambda off: pl.BlockSpec((seq, HEAD_DIM), lambda b, h, i: (b, off + h))
    return pl.pallas_call(
        functools.partial(_diff_kernel, tq=tq, lam_init=lam_init),
        grid=(batch, N_HEADS_B, nq),
        in_specs=[pl.BlockSpec((tq, HEAD_DIM), lambda b, h, i: (b * nq + i, _QB + h)),
                  kv(_KB), kv(_VB),
                  pl.BlockSpec((1, nq, tq, tq), lambda b, h, i: (h, 0, 0, 0)),
                  pl.BlockSpec((4, HEAD_DIM // 2), lambda b, h, i: (0, 0)),
                  pl.BlockSpec((1, HEAD_DIM), lambda b, h, i: (0, 0))],
        out_specs=pl.BlockSpec((tq, HEAD_DIM), lambda b, h, i: (b * nq + i, h)),
        out_shape=jax.ShapeDtypeStruct((t, N_HEADS_B * HEAD_DIM), BF16),
        compiler_params=_cparams(("parallel", "parallel", "arbitrary")),
        name="diff_attn",
    )(proj, proj, proj, bias_b, lam_params, g)


def _stick_kernel(q_ref, k_ref, v_ref, o_ref, *, tq):
    i = pl.program_id(2)
    q = q_ref[...]
    row = lax.broadcasted_iota(jnp.int32, (tq, tq), 0)
    col = lax.broadcasted_iota(jnp.int32, (tq, tq), 1)
    strict = col < row
    suffix = (row >= col).astype(BF16)

    def block(j, carry, masked):
        c, acc = carry
        kb = k_ref[pl.ds(pl.multiple_of(j * tq, tq), tq), :]
        vb = v_ref[pl.ds(pl.multiple_of(j * tq, tq), tq), :]
        z = _dot_nt(q, kb)
        u = jnp.minimum(-z, 0.0) - jnp.log(1.0 + jnp.exp(-jnp.abs(z)))
        if masked:
            u = jnp.where(strict, u, 0.0)
        hi = u.astype(BF16)
        lo = (u - hi.astype(F32)).astype(BF16)
        incl = _dot(hi, suffix) + _dot(lo, suffix)
        a = jnp.exp(z + incl + c)
        if masked:
            a = jnp.where(strict, a, 0.0)
        acc = acc + _dot(a.astype(BF16), vb)
        c = c + jnp.sum(u, axis=-1, keepdims=True)
        return c, acc

    init = (jnp.zeros((tq, 1), F32), jnp.zeros((tq, HEAD_DIM), F32))
    carry = block(i, init, True)
    _, acc = lax.fori_loop(0, i, lambda jj, cr: block(i - 1 - jj, cr, False), carry)
    o_ref[...] = acc.astype(o_ref.dtype)


def _stick_attention(proj, *, batch, seq, tq):
    t = proj.shape[0]
    nq = seq // tq
    kv = lambda off: pl.BlockSpec((seq, HEAD_DIM), lambda b, h, i: (b, off + h))
    return pl.pallas_call(
        functools.partial(_stick_kernel, tq=tq),
        grid=(batch, N_HEADS_C, nq),
        in_specs=[pl.BlockSpec((tq, HEAD_DIM), lambda b, h, i: (b * nq + i, _QC + h)),
                  kv(_KC), kv(_VC)],
        out_specs=pl.BlockSpec((tq, HEAD_DIM), lambda b, h, i: (b * nq + i, h)),
        out_shape=jax.ShapeDtypeStruct((t, N_HEADS_C * HEAD_DIM), BF16),
        compiler_params=_cparams(("parallel", "parallel", "arbitrary")),
        name="stick_attn",
    )(proj, proj, proj)


def _out_proj_kernel(x_ref, a_ref, b_ref, c_ref, wa_ref, wb_ref, wc_ref, o_ref):
    acc = _dot(a_ref[...], wa_ref[...]) + _dot(b_ref[...], wb_ref[...]) + _dot(c_ref[...], wc_ref[...])
    o_ref[...] = x_ref[...] + acc


def _out_proj(x, ma, mb, mc, w, *, tm=1024, tn=512):
    t, d = x.shape
    ka, kb, kc = ma.shape[1], mb.shape[1], mc.shape[1]
    wa, wb, wc = w[:ka], w[ka:ka + kb], w[ka + kb:]
    act = lambda k: pl.BlockSpec((tm, k), lambda i, j: (i, 0))
    wgt = lambda k: pl.BlockSpec((k, tn), lambda i, j: (0, j))
    return pl.pallas_call(
        _out_proj_kernel,
        grid=(t // tm, d // tn),
        in_specs=[pl.BlockSpec((tm, tn), lambda i, j: (i, j)),
                  act(ka), act(kb), act(kc), wgt(ka), wgt(kb), wgt(kc)],
        out_specs=pl.BlockSpec((tm, tn), lambda i, j: (i, j)),
        out_shape=jax.ShapeDtypeStruct((t, d), F32),
        compiler_params=_cparams(("parallel", "arbitrary")),
        name="out_proj",
    )(x, ma, mb, mc, wa, wb, wc)


def _mlp_kernel(x_ref, g_ref, w1_ref, w2_ref, gf_ref, o_ref, xn_ref, acc_ref, *, final_norm):
    f = pl.program_id(1)

    @pl.when(f == 0)
    def _():
        x = x_ref[...]
        ms = jnp.mean(x * x, axis=-1, keepdims=True)
        xn_ref[...] = (x * lax.rsqrt(ms + EPS) * g_ref[...]).astype(BF16)
        acc_ref[...] = jnp.zeros_like(acc_ref)

    h = jnp.maximum(_dot(xn_ref[...], w1_ref[...]), 0.0)
    acc_ref[...] += _dot((h * h).astype(BF16), w2_ref[...])

    @pl.when(f == pl.num_programs(1) - 1)
    def _():
        y = x_ref[...] + acc_ref[...]
        if final_norm:
            ms = jnp.mean(y * y, axis=-1, keepdims=True)
            y = y * lax.rsqrt(ms + EPS) * gf_ref[...]
        o_ref[...] = y


def _mlp(x, g, w1, w2, g_final, *, final_norm, tm=512, tf=512):
    t, d = x.shape
    dff = w1.shape[1]
    return pl.pallas_call(
        functools.partial(_mlp_kernel, final_norm=final_norm),
        grid=(t // tm, dff // tf),
        in_specs=[pl.BlockSpec((tm, d), lambda i, f: (i, 0)),
                  pl.BlockSpec((1, d), lambda i, f: (0, 0)),
                  pl.BlockSpec((d, tf), lambda i, f: (0, f)),
                  pl.BlockSpec((tf, d), lambda i, f: (f, 0)),
                  pl.BlockSpec((1, d), lambda i, f: (0, 0))],
        out_specs=pl.BlockSpec((tm, d), lambda i, f: (i, 0)),
        out_shape=jax.ShapeDtypeStruct((t, d), F32),
        scratch_shapes=[pltpu.VMEM((tm, d), BF16), pltpu.VMEM((tm, d), F32)],
        compiler_params=_cparams(("parallel", "arbitrary")),
        name="mlp",
    )(x, g, w1, w2, g_final)


def kernel(x, w_in, w_out, g_attn, g_mlp, w_mlp_in, w_mlp_out, rel_bias_table,
           diff_lam_q1, diff_lam_k1, diff_lam_q2, diff_lam_k2, diff_subln_g, g_final):
    batch, seq, d_model = x.shape
    depth = w_in.shape[0]
    tq_b = 256
    tq_c = 256

    i = np.arange(BLK)[:, None]
    c = np.arange(2 * BLK)[None, :]
    steps = i + BLK - c
    buckets_a = np.stack([_t5_bucket_np(steps * dil) for _, dil in DILATED_PATTERNS])
    nq_b = seq // tq_b
    dist = (np.arange(nq_b)[:, None, None] * tq_b + np.arange(tq_b)[None, :, None]
            - np.arange(tq_b)[None, None, :])
    buckets_b = _t5_bucket_np(dist)
    bias_a = _build_bias(rel_bias_table[:, :N_HEADS_A], jnp.asarray(buckets_a))
    bias_b = _build_bias(rel_bias_table[:, N_HEADS_A:], jnp.asarray(buckets_b))

    colscale = np.ones((1, w_in.shape[2]), np.float32)
    colscale[0, _QA * HEAD_DIM:_KA * HEAD_DIM] = 1.0 / math.sqrt(HEAD_DIM)
    colscale[0, _QB * HEAD_DIM:_KB * HEAD_DIM] = 1.0 / math.sqrt(HEAD_DIM // 2)
    colscale[0, _QC * HEAD_DIM:_KC * HEAD_DIM] = 1.0 / math.sqrt(HEAD_DIM)
    colscale = jnp.asarray(colscale)

    xf = x.reshape(batch * seq, d_model)
    for l in range(depth):
        lam_init = 0.8 - 0.6 * math.exp(-0.3 * l)
        lam_params = jnp.stack([diff_lam_q1[l], diff_lam_k1[l], diff_lam_q2[l], diff_lam_k2[l]]).astype(F32)
        proj = _rms_proj(xf, g_attn[l][None, :], w_in[l].astype(BF16), colscale)
        ma = _dilated_attention(proj, bias_a, batch=batch, seq=seq)
        mb = _diff_attention(proj, bias_b, lam_params, diff_subln_g[l][None, :],
                             batch=batch, seq=seq, tq=tq_b, lam_init=lam_init)
        mc = _stick_attention(proj, batch=batch, seq=seq, tq=tq_c)
        xf = _out_proj(xf, ma, mb, mc, w_out[l].astype(BF16))
        xf = _mlp(xf, g_mlp[l][None, :], w_mlp_in[l].astype(BF16), w_mlp_out[l].astype(BF16),
                  g_final[None, :], final_norm=(l == depth - 1))
    return xf.reshape(batch, seq, d_model)
```

```python
import functools
import math

import jax
import jax.numpy as jnp
import numpy as np
from jax import lax
from jax.experimental import pallas as pl
from jax.experimental.pallas import tpu as pltpu

HEAD_DIM = 128
N_HEADS_A = 6
N_HEADS_B = 4
N_HEADS_C = 6
DILATED_PATTERNS = ((128, 1), (512, 4), (2048, 16))
BLK = 128
NUM_BUCKETS = 32
MAX_DISTANCE = 2048
EPS = 1e-6
NEG_INF = -1e30
LOG2E = 1.4426950408889634
LN2 = 0.6931471805599453

F32 = jnp.float32
BF16 = jnp.bfloat16

_QA, _KA, _VA = 0, N_HEADS_A, 2 * N_HEADS_A
_QC = 3 * N_HEADS_A
_KC, _VC = _QC + N_HEADS_C, _QC + 2 * N_HEADS_C
_QB = _QC + 3 * N_HEADS_C
_KB, _VB = _QB + N_HEADS_B, _QB + 2 * N_HEADS_B

_VMEM_LIMIT = 56 * 1024 * 1024


def _cparams(sem, flags=None):
    return pltpu.CompilerParams(dimension_semantics=sem, vmem_limit_bytes=_VMEM_LIMIT, flags=flags)


def _dot_nt(a, b):
    return lax.dot_general(a, b, (((1,), (1,)), ((), ())), preferred_element_type=F32)


def _dot(a, b):
    return jnp.dot(a, b, preferred_element_type=F32)


def _staggered(n_chains, stages):
    states = [dict() for _ in range(n_chains)]
    for t in range(n_chains + len(stages) - 1):
        for k, stage in enumerate(stages):
            c = t - k
            if stage is not None and 0 <= c < n_chains:
                stage(c, states[c])
    return states


def _t5_bucket_np(dist):
    n = np.maximum(dist, 0)
    max_exact = NUM_BUCKETS // 2
    nf = np.maximum(n, max_exact).astype(np.float32)
    large = max_exact + (np.log(nf / np.float32(max_exact)) / np.float32(math.log(MAX_DISTANCE / max_exact))
                         * np.float32(NUM_BUCKETS - max_exact)).astype(np.int32)
    large = np.minimum(large, NUM_BUCKETS - 1)
    return np.where(n < max_exact, n, large).astype(np.int32)


def _bias_kernel(tab_ref, bkt_ref, o_ref, *, n_heads):
    b = bkt_ref[0]
    for h in range(n_heads):
        acc = jnp.zeros(b.shape, F32)
        for k in range(NUM_BUCKETS):
            acc = jnp.where(b == k, tab_ref[k, h], acc)
        o_ref[h, 0] = acc


def _build_bias(table, buckets):
    n_heads = table.shape[1]
    n, r, c = buckets.shape
    return pl.pallas_call(
        functools.partial(_bias_kernel, n_heads=n_heads),
        grid=(n,),
        in_specs=[pl.BlockSpec(memory_space=pltpu.SMEM),
                  pl.BlockSpec((1, r, c), lambda i: (i, 0, 0))],
        out_specs=pl.BlockSpec((n_heads, 1, r, c), lambda i: (0, i, 0, 0)),
        out_shape=jax.ShapeDtypeStruct((n_heads, n, r, c), F32),
        compiler_params=_cparams(("arbitrary",)),
        name="bias_table",
    )(table, buckets)


def _rms_proj_kernel(x_ref, g_ref, w_ref, cs_ref, o_ref, xn_ref):
    @pl.when(pl.program_id(1) == 0)
    def _():
        x = x_ref[...]
        ms = jnp.mean(x * x, axis=-1, keepdims=True)
        xn_ref[...] = (x * lax.rsqrt(ms + EPS) * g_ref[...]).astype(BF16)

    acc = _dot(xn_ref[...], w_ref[...])
    o_ref[...] = (acc * cs_ref[...]).astype(o_ref.dtype)


def _rms_proj(x, g, w, colscale, *, tm=512, tn=1024):
    t, d = x.shape
    n = w.shape[1]
    return pl.pallas_call(
        _rms_proj_kernel,
        grid=(t // tm, n // tn),
        in_specs=[pl.BlockSpec((tm, d), lambda i, j: (i, 0)),
                  pl.BlockSpec((1, d), lambda i, j: (0, 0)),
                  pl.BlockSpec((d, tn), lambda i, j: (0, j)),
                  pl.BlockSpec((1, tn), lambda i, j: (0, j))],
        out_specs=pl.BlockSpec((tm, tn), lambda i, j: (i, j)),
        out_shape=jax.ShapeDtypeStruct((t, n), BF16),
        scratch_shapes=[pltpu.VMEM((tm, d), BF16)],
        compiler_params=_cparams(("parallel", "arbitrary")),
        name="rms_proj",
    )(x, g, w, colscale)


def _dilated_kernel(q_ref, k_ref, v_ref, bias_ref, o_ref, qf, kf, vf, acc_s, m_s, l_s, *, seq, lag):
    qf[...] = q_ref[...].astype(F32)
    kf[...] = k_ref[...].astype(F32)
    vf[...] = v_ref[...].astype(F32)

    row = lax.broadcasted_iota(jnp.int32, (BLK, 2 * BLK), 0)
    col = lax.broadcasted_iota(jnp.int32, (BLK, 2 * BLK), 1)
    band_mask = (col >= row) & (col <= row + BLK)
    row0 = lax.broadcasted_iota(jnp.int32, (BLK, BLK), 0)
    col0 = lax.broadcasted_iota(jnp.int32, (BLK, BLK), 1)
    diag_mask = col0 <= row0

    def rows_of(start, size, dil):
        return pl.ds(start, size, stride=dil) if dil > 1 else pl.ds(start, size)

    def ld(ref, start, size, dil):
        return ref[rows_of(start, size, dil), :].astype(BF16)

    chains = []
    for p_idx, (window, dil) in enumerate(DILATED_PATTERNS):
        assert window // dil == BLK
        for r in range(dil):
            for n in range(seq // dil // BLK):
                chains.append((p_idx, dil, r + dil * BLK * n, None if n == 0 else r + dil * BLK * (n - 1)))

    def scores(c, st):
        p_idx, dil, q0, k0 = chains[c]
        qb = ld(qf, q0, BLK, dil)
        if k0 is None:
            s = _dot_nt(qb, ld(kf, q0, BLK, dil)) + bias_ref[0, p_idx, :, BLK:]
            st["s"] = jnp.where(diag_mask, s, NEG_INF)
        else:
            s = _dot_nt(qb, ld(kf, k0, 2 * BLK, dil)) + bias_ref[0, p_idx]
            st["s"] = jnp.where(band_mask, s, NEG_INF)

    def softmax(c, st):
        s = st.pop("s")
        st["m"] = jnp.max(s, axis=-1, keepdims=True)
        p = jnp.exp(s - st["m"])
        st["l"] = jnp.sum(p, axis=-1, keepdims=True)
        st["p"] = p.astype(BF16)

    def values(c, st):
        p_idx, dil, q0, k0 = chains[c]
        vb = ld(vf, q0, BLK, dil) if k0 is None else ld(vf, k0, 2 * BLK, dil)
        rows = rows_of(q0, BLK, dil)
        acc_s[p_idx, rows, :] = _dot(st.pop("p"), vb)
        m_s[p_idx, rows, :] = jnp.broadcast_to(st.pop("m"), (BLK, HEAD_DIM))
        l_s[p_idx, rows, :] = jnp.broadcast_to(st.pop("l"), (BLK, HEAD_DIM))

    _staggered(len(chains), [scores] + [None] * (lag - 1) + [softmax] + [None] * (lag - 1) + [values])

    chunk = 256

    def merge(ci, carry):
        rows = pl.ds(pl.multiple_of(ci * chunk, chunk), chunk)
        m0, m1, m2 = m_s[0, rows, :], m_s[1, rows, :], m_s[2, rows, :]
        mm = jnp.maximum(jnp.maximum(m0, m1), m2)
        w0, w1, w2 = jnp.exp(m0 - mm), jnp.exp(m1 - mm), jnp.exp(m2 - mm)
        num = w0 * acc_s[0, rows, :] + w1 * acc_s[1, rows, :] + w2 * acc_s[2, rows, :]
        den = w0 * l_s[0, rows, :] + w1 * l_s[1, rows, :] + w2 * l_s[2, rows, :]
        o_ref[rows, :] = (num / den).astype(o_ref.dtype)
        return carry

    lax.fori_loop(0, seq // chunk, merge, 0)


def _dilated_attention(proj, bias_a, *, batch, seq):
    t = proj.shape[0]
    n_pat = len(DILATED_PATTERNS)
    blk = lambda off: pl.BlockSpec((seq, HEAD_DIM), lambda b, h: (b, off + h))
    return pl.pallas_call(
        functools.partial(_dilated_kernel, seq=seq, lag=3),
        grid=(batch, N_HEADS_A),
        in_specs=[blk(_QA), blk(_KA), blk(_VA),
                  pl.BlockSpec((1, n_pat, BLK, 2 * BLK), lambda b, h: (h, 0, 0, 0))],
        out_specs=pl.BlockSpec((seq, HEAD_DIM), lambda b, h: (b, h)),
        out_shape=jax.ShapeDtypeStruct((t, N_HEADS_A * HEAD_DIM), BF16),
        scratch_shapes=[pltpu.VMEM((seq, HEAD_DIM), F32)] * 3
                       + [pltpu.VMEM((n_pat, seq, HEAD_DIM), F32)] * 3,
        compiler_params=_cparams(("parallel", "parallel")),
        name="dilated_attn",
    )(proj, proj, proj, bias_a)


def _diff_kernel(q_ref, k_ref, v_ref, bias_ref, lam_ref, g_ref, o_ref, *, tq, n_heads, lam_init):
    i = pl.program_id(1)
    half = HEAD_DIM // 2
    heads = [slice(g * HEAD_DIM, (g + 1) * HEAD_DIM) for g in range(n_heads)]
    lane = lax.broadcasted_iota(jnp.int32, (tq, HEAD_DIM), 1)
    qqs = []
    for hs in heads:
        q = q_ref[:, hs]
        zero = jnp.zeros_like(q)
        qqs.append(jnp.concatenate([jnp.where(lane < half, q, zero), jnp.where(lane >= half, q, zero)], axis=0))
    row = lax.broadcasted_iota(jnp.int32, (2 * tq, tq), 0)
    col = lax.broadcasted_iota(jnp.int32, (2 * tq, tq), 1)
    causal = col <= jnp.where(row >= tq, row - tq, row)

    def step(j, carries, masked):
        rows = pl.ds(pl.multiple_of(j * tq, tq), tq)

        def scores(g, st):
            bias = bias_ref[g, i - j]
            s = _dot_nt(qqs[g], k_ref[rows, heads[g]]) + jnp.concatenate([bias, bias], axis=0)
            st["s"] = jnp.where(causal, s, NEG_INF) if masked else s

        def softmax(g, st):
            m, l, _ = carries[g]
            s = st.pop("s")
            st["m"] = jnp.maximum(m, jnp.max(s, axis=-1, keepdims=True))
            st["alpha"] = jnp.exp(m - st["m"])
            p = jnp.exp(s - st["m"])
            st["l"] = st["alpha"] * l + jnp.sum(p, axis=-1, keepdims=True)
            st["p"] = p.astype(BF16)

        def values(g, st):
            acc = st["alpha"] * carries[g][2] + _dot(st.pop("p"), v_ref[rows, heads[g]])
            st["out"] = (st["m"], st["l"], acc)

        return tuple(st["out"] for st in _staggered(n_heads, [scores, softmax, values]))

    init = tuple((jnp.full((2 * tq, 1), NEG_INF, F32), jnp.zeros((2 * tq, 1), F32),
                  jnp.zeros((2 * tq, HEAD_DIM), F32)) for _ in heads)
    carries = lax.fori_loop(0, i, lambda j, cr: step(j, cr, False), init)
    carries = step(i, carries, True)

    lp = lam_ref[...]
    lam = (jnp.exp(jnp.sum(lp[0:1] * lp[1:2], axis=-1, keepdims=True))
           - jnp.exp(jnp.sum(lp[2:3] * lp[3:4], axis=-1, keepdims=True)) + lam_init)
    for hs, (_, l, acc) in zip(heads, carries):
        o = acc / l
        out = o[:tq] - lam * o[tq:]
        ms = jnp.mean(out * out, axis=-1, keepdims=True)
        y = out * lax.rsqrt(ms + EPS) * g_ref[...] * (1.0 - lam_init)
        o_ref[:, hs] = y.astype(o_ref.dtype)


def _diff_attention(proj, bias_b, lam_params, g, *, batch, seq, tq, lam_init):
    t = proj.shape[0]
    nq = seq // tq
    nh = N_HEADS_B
    width = nh * HEAD_DIM
    assert _QB % nh == 0 and _KB % nh == 0 and _VB % nh == 0
    kv = lambda off: pl.BlockSpec((seq, width), lambda b, i: (b, off // nh))
    return pl.pallas_call(
        functools.partial(_diff_kernel, tq=tq, n_heads=nh, lam_init=lam_init),
        grid=(batch, nq),
        in_specs=[pl.BlockSpec((tq, width), lambda b, i: (b * nq + i, _QB // nh)),
                  kv(_KB), kv(_VB),
                  pl.BlockSpec((nh, nq, tq, tq), lambda b, i: (0, 0, 0, 0)),
                  pl.BlockSpec((4, HEAD_DIM // 2), lambda b, i: (0, 0)),
                  pl.BlockSpec((1, HEAD_DIM), lambda b, i: (0, 0))],
        out_specs=pl.BlockSpec((tq, width), lambda b, i: (b * nq + i, 0)),
        out_shape=jax.ShapeDtypeStruct((t, width), BF16),
        compiler_params=_cparams(("parallel", "arbitrary")),
        name="diff_attn",
    )(proj, proj, proj, bias_b, lam_params, g)


def _stick_kernel(q_ref, k_ref, v_ref, o_ref, *, tq, n_heads):
    i = pl.program_id(2)
    row = lax.broadcasted_iota(jnp.int32, (tq, tq), 0)
    col = lax.broadcasted_iota(jnp.int32, (tq, tq), 1)
    strict = col < row
    suffix = (row >= col).astype(BF16)
    heads = [slice(g * HEAD_DIM, (g + 1) * HEAD_DIM) for g in range(n_heads)]
    qs = [q_ref[:, hs] for hs in heads]

    def blocks(j, carries, masked):
        rows = pl.ds(pl.multiple_of(j * tq, tq), tq)

        def scores(g, st):
            st["z"] = _dot_nt(qs[g], k_ref[rows, heads[g]])

        def log_break(g, st):
            z = st["z"]
            neg_abs = lax.bitcast_convert_type(lax.bitcast_convert_type(z, jnp.uint32) | jnp.uint32(0x80000000), F32)
            w = jnp.maximum(z, 0.0) + jnp.log(1.0 + jnp.exp(neg_abs))
            if masked:
                w = jnp.where(strict, w, 0.0)
            st["hi"] = w.astype(BF16)
            st["lo"] = (w - st["hi"].astype(F32)).astype(BF16)
            st["c"] = carries[g][0] + jnp.sum(w, axis=-1, keepdims=True)

        def suffix_sums(g, st):
            st["incl"] = _dot(st.pop("hi"), suffix) + _dot(st.pop("lo"), suffix)

        def weights(g, st):
            a = jnp.exp((st.pop("z") - carries[g][0]) - st.pop("incl"))
            if masked:
                a = jnp.where(strict, a, 0.0)
            st["a"] = a.astype(BF16)

        def values(g, st):
            st["out"] = (st["c"], carries[g][1] + _dot(st.pop("a"), v_ref[rows, heads[g]]))

        stages = [scores, log_break, suffix_sums, weights, values]
        return tuple(st["out"] for st in _staggered(n_heads, stages))

    init = tuple((jnp.zeros((tq, 1), F32), jnp.zeros((tq, HEAD_DIM), F32)) for _ in heads)
    carries = blocks(i, init, True)
    carries = lax.fori_loop(0, i, lambda jj, cr: blocks(i - 1 - jj, cr, False), carries)
    for hs, (_, acc) in zip(heads, carries):
        o_ref[:, hs] = acc.astype(o_ref.dtype)


def _stick_attention(proj, *, batch, seq, tq, heads_per_step):
    t = proj.shape[0]
    nq = seq // tq
    g = heads_per_step
    width = g * HEAD_DIM
    kv = lambda off: pl.BlockSpec((seq, width), lambda b, h, i: (b, off // g + h))
    return pl.pallas_call(
        functools.partial(_stick_kernel, tq=tq, n_heads=g),
        grid=(batch, N_HEADS_C // g, nq),
        in_specs=[pl.BlockSpec((tq, width), lambda b, h, i: (b * nq + i, _QC // g + h)),
                  kv(_KC), kv(_VC)],
        out_specs=pl.BlockSpec((tq, width), lambda b, h, i: (b * nq + i, h)),
        out_shape=jax.ShapeDtypeStruct((t, N_HEADS_C * HEAD_DIM), BF16),
        compiler_params=_cparams(("parallel", "parallel", "arbitrary")),
        name="stick_attn",
    )(proj, proj, proj)


def _out_proj_kernel(x_ref, a_ref, b_ref, c_ref, wa_ref, wb_ref, wc_ref, o_ref):
    acc = _dot(a_ref[...], wa_ref[...]) + _dot(b_ref[...], wb_ref[...]) + _dot(c_ref[...], wc_ref[...])
    o_ref[...] = x_ref[...] + acc


def _out_proj(x, ma, mb, mc, w, *, tm=1024, tn=512):
    t, d = x.shape
    ka, kb, kc = ma.shape[1], mb.shape[1], mc.shape[1]
    wa, wb, wc = w[:ka], w[ka:ka + kb], w[ka + kb:]
    act = lambda k: pl.BlockSpec((tm, k), lambda i, j: (i, 0))
    wgt = lambda k: pl.BlockSpec((k, tn), lambda i, j: (0, j))
    return pl.pallas_call(
        _out_proj_kernel,
        grid=(t // tm, d // tn),
        in_specs=[pl.BlockSpec((tm, tn), lambda i, j: (i, j)),
                  act(ka), act(kb), act(kc), wgt(ka), wgt(kb), wgt(kc)],
        out_specs=pl.BlockSpec((tm, tn), lambda i, j: (i, j)),
        out_shape=jax.ShapeDtypeStruct((t, d), F32),
        compiler_params=_cparams(("parallel", "arbitrary")),
        name="out_proj",
    )(x, ma, mb, mc, wa, wb, wc)


def _mlp_kernel(x_ref, g_ref, w1_ref, w2_ref, gf_ref, o_ref, xn_ref, acc_ref, *, final_norm):
    f = pl.program_id(1)

    @pl.when(f == 0)
    def _():
        x = x_ref[...]
        ms = jnp.mean(x * x, axis=-1, keepdims=True)
        xn_ref[...] = (x * lax.rsqrt(ms + EPS) * g_ref[...]).astype(BF16)
        acc_ref[...] = jnp.zeros_like(acc_ref)

    h = jnp.maximum(_dot(xn_ref[...], w1_ref[...]), 0.0)
    acc_ref[...] += _dot((h * h).astype(BF16), w2_ref[...])

    @pl.when(f == pl.num_programs(1) - 1)
    def _():
        y = x_ref[...] + acc_ref[...]
        if final_norm:
            ms = jnp.mean(y * y, axis=-1, keepdims=True)
            y = y * lax.rsqrt(ms + EPS) * gf_ref[...]
        o_ref[...] = y


def _mlp(x, g, w1, w2, g_final, *, final_norm, tm=512, tf=512):
    t, d = x.shape
    dff = w1.shape[1]
    return pl.pallas_call(
        functools.partial(_mlp_kernel, final_norm=final_norm),
        grid=(t // tm, dff // tf),
        in_specs=[pl.BlockSpec((tm, d), lambda i, f: (i, 0)),
                  pl.BlockSpec((1, d), lambda i, f: (0, 0)),
                  pl.BlockSpec((d, tf), lambda i, f: (0, f)),
                  pl.BlockSpec((tf, d), lambda i, f: (f, 0)),
                  pl.BlockSpec((1, d), lambda i, f: (0, 0))],
        out_specs=pl.BlockSpec((tm, d), lambda i, f: (i, 0)),
        out_shape=jax.ShapeDtypeStruct((t, d), F32),
        scratch_shapes=[pltpu.VMEM((tm, d), BF16), pltpu.VMEM((tm, d), F32)],
        compiler_params=_cparams(("parallel", "arbitrary")),
        name="mlp",
    )(x, g, w1, w2, g_final)


def kernel(x, w_in, w_out, g_attn, g_mlp, w_mlp_in, w_mlp_out, rel_bias_table,
           diff_lam_q1, diff_lam_k1, diff_lam_q2, diff_lam_k2, diff_subln_g, g_final):
    batch, seq, d_model = x.shape
    depth = w_in.shape[0]
    tq_b = 256
    tq_c = 256

    i = np.arange(BLK)[:, None]
    c = np.arange(2 * BLK)[None, :]
    steps = i + BLK - c
    buckets_a = np.stack([_t5_bucket_np(steps * dil) for _, dil in DILATED_PATTERNS])
    nq_b = seq // tq_b
    dist = (np.arange(nq_b)[:, None, None] * tq_b + np.arange(tq_b)[None, :, None]
            - np.arange(tq_b)[None, None, :])
    buckets_b = _t5_bucket_np(dist)
    bias_a = _build_bias(rel_bias_table[:, :N_HEADS_A], jnp.asarray(buckets_a))
    bias_b = _build_bias(rel_bias_table[:, N_HEADS_A:], jnp.asarray(buckets_b))

    colscale = np.ones((1, w_in.shape[2]), np.float32)
    colscale[0, _QA * HEAD_DIM:_KA * HEAD_DIM] = 1.0 / math.sqrt(HEAD_DIM)
    colscale[0, _QB * HEAD_DIM:_KB * HEAD_DIM] = 1.0 / math.sqrt(HEAD_DIM // 2)
    colscale[0, _QC * HEAD_DIM:_KC * HEAD_DIM] = 1.0 / math.sqrt(HEAD_DIM)
    colscale = jnp.asarray(colscale)

    split_a = 3 * N_HEADS_A * HEAD_DIM
    split_b = split_a + 3 * N_HEADS_B * HEAD_DIM
    xf = x.reshape(batch * seq, d_model)
    for l in range(depth):
        lam_init = 0.8 - 0.6 * math.exp(-0.3 * l)
        lam_params = jnp.stack([diff_lam_q1[l], diff_lam_k1[l], diff_lam_q2[l], diff_lam_k2[l]]).astype(F32)
        wl = w_in[l]
        w_acb = jnp.concatenate([wl[:, :split_a], wl[:, split_b:], wl[:, split_a:split_b]], axis=1)
        proj = _rms_proj(xf, g_attn[l][None, :], w_acb.astype(BF16), colscale)
        ma = _dilated_attention(proj, bias_a, batch=batch, seq=seq)
        mb = _diff_attention(proj, bias_b, lam_params, diff_subln_g[l][None, :],
                             batch=batch, seq=seq, tq=tq_b, lam_init=lam_init)
        mc = _stick_attention(proj, batch=batch, seq=seq, tq=tq_c, heads_per_step=6)
        xf = _out_proj(xf, ma, mb, mc, w_out[l].astype(BF16))
        xf = _mlp(xf, g_mlp[l][None, :], w_mlp_in[l].astype(BF16), w_mlp_out[l].astype(BF16),
                  g_final[None, :], final_norm=(l == depth - 1))
    return xf.reshape(batch, seq, d_model)
```

```python
import functools
import math

import jax
import jax.numpy as jnp
import numpy as np
from jax import lax
from jax.experimental import pallas as pl
from jax.experimental.pallas import tpu as pltpu

HEAD_DIM = 128
N_HEADS_A = 6
N_HEADS_B = 4
N_HEADS_C = 6
DILATED_PATTERNS = ((128, 1), (512, 4), (2048, 16))
BLK = 128
NUM_BUCKETS = 32
MAX_DISTANCE = 2048
EPS = 1e-6
NEG_INF = -1e30
LOG2E = 1.4426950408889634
LN2 = 0.6931471805599453

F32 = jnp.float32
BF16 = jnp.bfloat16

_QA, _KA, _VA = 0, N_HEADS_A, 2 * N_HEADS_A
_QC = 3 * N_HEADS_A
_KC, _VC = _QC + N_HEADS_C, _QC + 2 * N_HEADS_C
_QB = _QC + 3 * N_HEADS_C
_KB, _VB = _QB + N_HEADS_B, _QB + 2 * N_HEADS_B

_VMEM_LIMIT = 56 * 1024 * 1024


def _cparams(sem, flags=None):
    return pltpu.CompilerParams(dimension_semantics=sem, vmem_limit_bytes=_VMEM_LIMIT, flags=flags)


def _dot_nt(a, b):
    return lax.dot_general(a, b, (((1,), (1,)), ((), ())), preferred_element_type=F32)


def _dot(a, b):
    return jnp.dot(a, b, preferred_element_type=F32)


def _staggered(n_chains, stages):
    states = [dict() for _ in range(n_chains)]
    for t in range(n_chains + len(stages) - 1):
        for k, stage in enumerate(stages):
            c = t - k
            if stage is not None and 0 <= c < n_chains:
                stage(c, states[c])
    return states


def _t5_bucket_np(dist):
    n = np.maximum(dist, 0)
    max_exact = NUM_BUCKETS // 2
    nf = np.maximum(n, max_exact).astype(np.float32)
    large = max_exact + (np.log(nf / np.float32(max_exact)) / np.float32(math.log(MAX_DISTANCE / max_exact))
                         * np.float32(NUM_BUCKETS - max_exact)).astype(np.int32)
    large = np.minimum(large, NUM_BUCKETS - 1)
    return np.where(n < max_exact, n, large).astype(np.int32)


def _bias_kernel(tab_ref, bkt_ref, o_ref, *, n_heads):
    b = bkt_ref[0]
    for h in range(n_heads):
        acc = jnp.zeros(b.shape, F32)
        for k in range(NUM_BUCKETS):
            acc = jnp.where(b == k, tab_ref[k, h], acc)
        o_ref[h, 0] = acc


def _build_bias(table, buckets):
    n_heads = table.shape[1]
    n, r, c = buckets.shape
    return pl.pallas_call(
        functools.partial(_bias_kernel, n_heads=n_heads),
        grid=(n,),
        in_specs=[pl.BlockSpec(memory_space=pltpu.SMEM),
                  pl.BlockSpec((1, r, c), lambda i: (i, 0, 0))],
        out_specs=pl.BlockSpec((n_heads, 1, r, c), lambda i: (0, i, 0, 0)),
        out_shape=jax.ShapeDtypeStruct((n_heads, n, r, c), F32),
        compiler_params=_cparams(("arbitrary",)),
        name="bias_table",
    )(table, buckets)


def _rms_proj_kernel(x_ref, g_ref, w_ref, cs_ref, o_ref, xn_ref):
    @pl.when(pl.program_id(1) == 0)
    def _():
        x = x_ref[...]
        ms = jnp.mean(x * x, axis=-1, keepdims=True)
        xn_ref[...] = (x * lax.rsqrt(ms + EPS) * g_ref[...]).astype(BF16)

    acc = _dot(xn_ref[...], w_ref[...])
    o_ref[...] = (acc * cs_ref[...]).astype(o_ref.dtype)


def _rms_proj(x, g, w, colscale, *, tm=1024, tn=1024):
    t, d = x.shape
    n = w.shape[1]
    return pl.pallas_call(
        _rms_proj_kernel,
        grid=(t // tm, n // tn),
        in_specs=[pl.BlockSpec((tm, d), lambda i, j: (i, 0)),
                  pl.BlockSpec((1, d), lambda i, j: (0, 0)),
                  pl.BlockSpec((d, tn), lambda i, j: (0, j)),
                  pl.BlockSpec((1, tn), lambda i, j: (0, j))],
        out_specs=pl.BlockSpec((tm, tn), lambda i, j: (i, j)),
        out_shape=jax.ShapeDtypeStruct((t, n), BF16),
        scratch_shapes=[pltpu.VMEM((tm, d), BF16)],
        compiler_params=_cparams(("parallel", "arbitrary")),
        name="rms_proj",
    )(x, g, w, colscale)


def _dilated_kernel(q_ref, k_ref, v_ref, bias_ref, o_ref, qf, kf, vf, acc_s, m_s, l_s, *, seq, lag):
    qf[...] = q_ref[...].astype(F32)
    kf[...] = k_ref[...].astype(F32)
    vf[...] = v_ref[...].astype(F32)

    row = lax.broadcasted_iota(jnp.int32, (BLK, 2 * BLK), 0)
    col = lax.broadcasted_iota(jnp.int32, (BLK, 2 * BLK), 1)
    band_mask = (col >= row) & (col <= row + BLK)
    row0 = lax.broadcasted_iota(jnp.int32, (BLK, BLK), 0)
    col0 = lax.broadcasted_iota(jnp.int32, (BLK, BLK), 1)
    diag_mask = col0 <= row0

    def rows_of(start, size, dil):
        return pl.ds(start, size, stride=dil) if dil > 1 else pl.ds(start, size)

    def ld(ref, start, size, dil):
        return ref[rows_of(start, size, dil), :].astype(BF16)

    chains = []
    for p_idx, (window, dil) in enumerate(DILATED_PATTERNS):
        assert window // dil == BLK
        for r in range(dil):
            for n in range(seq // dil // BLK):
                chains.append((p_idx, dil, r + dil * BLK * n, None if n == 0 else r + dil * BLK * (n - 1)))

    def scores(c, st):
        p_idx, dil, q0, k0 = chains[c]
        qb = ld(qf, q0, BLK, dil)
        if k0 is None:
            s = _dot_nt(qb, ld(kf, q0, BLK, dil)) + bias_ref[0, p_idx, :, BLK:]
            st["s"] = jnp.where(diag_mask, s, NEG_INF)
        else:
            s = _dot_nt(qb, ld(kf, k0, 2 * BLK, dil)) + bias_ref[0, p_idx]
            st["s"] = jnp.where(band_mask, s, NEG_INF)

    def softmax(c, st):
        s = st.pop("s")
        st["m"] = jnp.max(s, axis=-1, keepdims=True)
        p = jnp.exp(s - st["m"])
        st["l"] = jnp.sum(p, axis=-1, keepdims=True)
        st["p"] = p.astype(BF16)

    def values(c, st):
        p_idx, dil, q0, k0 = chains[c]
        vb = ld(vf, q0, BLK, dil) if k0 is None else ld(vf, k0, 2 * BLK, dil)
        rows = rows_of(q0, BLK, dil)
        acc_s[p_idx, rows, :] = _dot(st.pop("p"), vb)
        m_s[p_idx, rows, :] = jnp.broadcast_to(st.pop("m"), (BLK, HEAD_DIM))
        l_s[p_idx, rows, :] = jnp.broadcast_to(st.pop("l"), (BLK, HEAD_DIM))

    _staggered(len(chains), [scores] + [None] * (lag - 1) + [softmax] + [None] * (lag - 1) + [values])

    chunk = 256

    def merge(ci, carry):
        rows = pl.ds(pl.multiple_of(ci * chunk, chunk), chunk)
        m0, m1, m2 = m_s[0, rows, :], m_s[1, rows, :], m_s[2, rows, :]
        mm = jnp.maximum(jnp.maximum(m0, m1), m2)
        w0, w1, w2 = jnp.exp(m0 - mm), jnp.exp(m1 - mm), jnp.exp(m2 - mm)
        num = w0 * acc_s[0, rows, :] + w1 * acc_s[1, rows, :] + w2 * acc_s[2, rows, :]
        den = w0 * l_s[0, rows, :] + w1 * l_s[1, rows, :] + w2 * l_s[2, rows, :]
        o_ref[rows, :] = (num / den).astype(o_ref.dtype)
        return carry

    lax.fori_loop(0, seq // chunk, merge, 0)


def _dilated_attention(proj, bias_a, *, batch, seq):
    t = proj.shape[0]
    n_pat = len(DILATED_PATTERNS)
    blk = lambda off: pl.BlockSpec((seq, HEAD_DIM), lambda b, h: (b, off + h))
    return pl.pallas_call(
        functools.partial(_dilated_kernel, seq=seq, lag=3),
        grid=(batch, N_HEADS_A),
        in_specs=[blk(_QA), blk(_KA), blk(_VA),
                  pl.BlockSpec((1, n_pat, BLK, 2 * BLK), lambda b, h: (h, 0, 0, 0))],
        out_specs=pl.BlockSpec((seq, HEAD_DIM), lambda b, h: (b, h)),
        out_shape=jax.ShapeDtypeStruct((t, N_HEADS_A * HEAD_DIM), BF16),
        scratch_shapes=[pltpu.VMEM((seq, HEAD_DIM), F32)] * 3
                       + [pltpu.VMEM((n_pat, seq, HEAD_DIM), F32)] * 3,
        compiler_params=_cparams(("parallel", "parallel")),
        name="dilated_attn",
    )(proj, proj, proj, bias_a)


def _diff_kernel(q_ref, k_ref, v_ref, bias_ref, lam_ref, g_ref, o_ref, vt_ref, *, tq, n_heads, seq, lam_init):
    i = pl.program_id(1)
    half = HEAD_DIM // 2
    heads = [slice(g * HEAD_DIM, (g + 1) * HEAD_DIM) for g in range(n_heads)]

    @pl.when(i == 0)
    def _():
        r = lax.broadcasted_iota(jnp.int32, (HEAD_DIM, HEAD_DIM), 0)
        c = lax.broadcasted_iota(jnp.int32, (HEAD_DIM, HEAD_DIM), 1)
        eye = (r == c).astype(BF16)

        def transpose_block(jb, carry):
            rows = pl.ds(pl.multiple_of(jb * tq, tq), tq)
            for g, hs in enumerate(heads):
                vt_ref[g, jb] = _dot_nt(eye, v_ref[rows, hs]).astype(BF16)
            return carry

        lax.fori_loop(0, seq // tq, transpose_block, 0)

    lane = lax.broadcasted_iota(jnp.int32, (tq, HEAD_DIM), 1)
    qqs = []
    for hs in heads:
        q = q_ref[:, hs]
        zero = jnp.zeros_like(q)
        qqs.append(jnp.concatenate([jnp.where(lane < half, q, zero), jnp.where(lane >= half, q, zero)], axis=0))
    key = lax.broadcasted_iota(jnp.int32, (tq, 2 * tq), 0)
    qry = lax.broadcasted_iota(jnp.int32, (tq, 2 * tq), 1)
    causal = key <= jnp.where(qry >= tq, qry - tq, qry)

    def step(j, carries, masked):
        rows = pl.ds(pl.multiple_of(j * tq, tq), tq)

        def scores(g, st):
            bias = bias_ref[g, i - j]
            s = _dot_nt(k_ref[rows, heads[g]], qqs[g]) + jnp.concatenate([bias, bias], axis=1)
            st["s"] = jnp.where(causal, s, NEG_INF) if masked else s

        def softmax(g, st):
            m, l, _ = carries[g]
            s = st.pop("s")
            st["m"] = jnp.maximum(m, jnp.max(s, axis=0, keepdims=True))
            st["alpha"] = jnp.exp(m - st["m"])
            p = jnp.exp(s - st["m"])
            st["l"] = st["alpha"] * l + jnp.sum(p, axis=0, keepdims=True)
            st["p"] = p.astype(BF16)

        def values(g, st):
            acc = st["alpha"] * carries[g][2] + _dot(vt_ref[g, j], st.pop("p"))
            st["out"] = (st["m"], st["l"], acc)

        return tuple(st["out"] for st in _staggered(n_heads, [scores, softmax, values]))

    init = tuple((jnp.full((1, 2 * tq), NEG_INF, F32), jnp.zeros((1, 2 * tq), F32),
                  jnp.zeros((HEAD_DIM, 2 * tq), F32)) for _ in heads)
    carries = lax.fori_loop(0, i, lambda j, cr: step(j, cr, False), init)
    carries = step(i, carries, True)

    lp = lam_ref[...]
    lam = (jnp.exp(jnp.sum(lp[0:1] * lp[1:2], axis=-1, keepdims=True))
           - jnp.exp(jnp.sum(lp[2:3] * lp[3:4], axis=-1, keepdims=True)) + lam_init)
    for hs, (_, l, acc) in zip(heads, carries):
        o = acc / l
        out = o[:, :tq] - lam * o[:, tq:]
        ms = jnp.mean(out * out, axis=0, keepdims=True)
        y = out * lax.rsqrt(ms + EPS) * g_ref[...] * (1.0 - lam_init)
        o_ref[:, hs] = y.T.astype(o_ref.dtype)


def _diff_attention(proj, bias_b, lam_params, g, *, batch, seq, tq, lam_init):
    t = proj.shape[0]
    nq = seq // tq
    nh = N_HEADS_B
    width = nh * HEAD_DIM
    assert _QB % nh == 0 and _KB % nh == 0 and _VB % nh == 0
    kv = lambda off: pl.BlockSpec((seq, width), lambda b, i: (b, off // nh))
    return pl.pallas_call(
        functools.partial(_diff_kernel, tq=tq, n_heads=nh, seq=seq, lam_init=lam_init),
        grid=(batch, nq),
        in_specs=[pl.BlockSpec((tq, width), lambda b, i: (b * nq + i, _QB // nh)),
                  kv(_KB), kv(_VB),
                  pl.BlockSpec((nh, nq, tq, tq), lambda b, i: (0, 0, 0, 0)),
                  pl.BlockSpec((4, HEAD_DIM // 2), lambda b, i: (0, 0)),
                  pl.BlockSpec((HEAD_DIM, 1), lambda b, i: (0, 0))],
        out_specs=pl.BlockSpec((tq, width), lambda b, i: (b * nq + i, 0)),
        out_shape=jax.ShapeDtypeStruct((t, width), BF16),
        scratch_shapes=[pltpu.VMEM((nh, nq, HEAD_DIM, tq), BF16)],
        compiler_params=_cparams(("parallel", "arbitrary")),
        name="diff_attn",
    )(proj, proj, proj, bias_b, lam_params, g)


def _stick_kernel(q_ref, k_ref, v_ref, o_ref, *, tq, n_heads):
    i = pl.program_id(2)
    row = lax.broadcasted_iota(jnp.int32, (tq, tq), 0)
    col = lax.broadcasted_iota(jnp.int32, (tq, tq), 1)
    strict = col < row
    suffix = (row >= col).astype(BF16)
    heads = [slice(g * HEAD_DIM, (g + 1) * HEAD_DIM) for g in range(n_heads)]
    qs = [q_ref[:, hs] for hs in heads]

    def blocks(j, carries, masked):
        rows = pl.ds(pl.multiple_of(j * tq, tq), tq)

        def scores(g, st):
            st["z"] = _dot_nt(qs[g], k_ref[rows, heads[g]])

        def log_break(g, st):
            z = st["z"]
            neg_abs = lax.bitcast_convert_type(lax.bitcast_convert_type(z, jnp.uint32) | jnp.uint32(0x80000000), F32)
            w = jnp.maximum(z, 0.0) + jnp.log(1.0 + jnp.exp(neg_abs))
            if masked:
                w = jnp.where(strict, w, 0.0)
            st["hi"] = w.astype(BF16)
            st["lo"] = (w - st["hi"].astype(F32)).astype(BF16)
            st["c"] = carries[g][0] + jnp.sum(w, axis=-1, keepdims=True)

        def suffix_sums(g, st):
            st["incl"] = _dot(st.pop("hi"), suffix) + _dot(st.pop("lo"), suffix)

        def weights(g, st):
            a = jnp.exp((st.pop("z") - carries[g][0]) - st.pop("incl"))
            if masked:
                a = jnp.where(strict, a, 0.0)
            st["a"] = a.astype(BF16)

        def values(g, st):
            st["out"] = (st["c"], carries[g][1] + _dot(st.pop("a"), v_ref[rows, heads[g]]))

        stages = [scores, log_break, suffix_sums, weights, values]
        return tuple(st["out"] for st in _staggered(n_heads, stages))

    init = tuple((jnp.zeros((tq, 1), F32), jnp.zeros((tq, HEAD_DIM), F32)) for _ in heads)
    carries = blocks(i, init, True)
    carries = lax.fori_loop(0, i, lambda jj, cr: blocks(i - 1 - jj, cr, False), carries)
    for hs, (_, acc) in zip(heads, carries):
        o_ref[:, hs] = acc.astype(o_ref.dtype)


def _stick_attention(proj, *, batch, seq, tq, heads_per_step):
    t = proj.shape[0]
    nq = seq // tq
    g = heads_per_step
    width = g * HEAD_DIM
    kv = lambda off: pl.BlockSpec((seq, width), lambda b, h, i: (b, off // g + h))
    return pl.pallas_call(
        functools.partial(_stick_kernel, tq=tq, n_heads=g),
        grid=(batch, N_HEADS_C // g, nq),
        in_specs=[pl.BlockSpec((tq, width), lambda b, h, i: (b * nq + i, _QC // g + h)),
                  kv(_KC), kv(_VC)],
        out_specs=pl.BlockSpec((tq, width), lambda b, h, i: (b * nq + i, h)),
        out_shape=jax.ShapeDtypeStruct((t, N_HEADS_C * HEAD_DIM), BF16),
        compiler_params=_cparams(("parallel", "parallel", "arbitrary")),
        name="stick_attn",
    )(proj, proj, proj)


def _out_proj_kernel(x_ref, a_ref, b_ref, c_ref, wa_ref, wb_ref, wc_ref, o_ref):
    acc = _dot(a_ref[...], wa_ref[...]) + _dot(b_ref[...], wb_ref[...]) + _dot(c_ref[...], wc_ref[...])
    o_ref[...] = x_ref[...] + acc


def _out_proj(x, ma, mb, mc, w, *, tm=512):
    t, d = x.shape
    ka, kb, kc = ma.shape[1], mb.shape[1], mc.shape[1]
    wa, wb, wc = w[:ka], w[ka:ka + kb], w[ka + kb:]
    act = lambda k: pl.BlockSpec((tm, k), lambda i: (i, 0))
    wgt = lambda k: pl.BlockSpec((k, d), lambda i: (0, 0))
    return pl.pallas_call(
        _out_proj_kernel,
        grid=(t // tm,),
        in_specs=[act(d), act(ka), act(kb), act(kc), wgt(ka), wgt(kb), wgt(kc)],
        out_specs=act(d),
        out_shape=jax.ShapeDtypeStruct((t, d), F32),
        compiler_params=_cparams(("parallel",)),
        name="out_proj",
    )(x, ma, mb, mc, wa, wb, wc)


def _mlp_kernel(x_ref, g_ref, w1_ref, w2_ref, gf_ref, o_ref, xn_ref, *, final_norm):
    f = pl.program_id(1)

    @pl.when(f == 0)
    def _():
        x = x_ref[...]
        ms = jnp.mean(x * x, axis=-1, keepdims=True)
        xn_ref[...] = (x * lax.rsqrt(ms + EPS) * g_ref[...]).astype(BF16)
        o_ref[...] = x

    h = jnp.maximum(_dot(xn_ref[...], w1_ref[...]), 0.0)
    o_ref[...] += _dot((h * h).astype(BF16), w2_ref[...])

    if final_norm:
        @pl.when(f == pl.num_programs(1) - 1)
        def _():
            y = o_ref[...]
            ms = jnp.mean(y * y, axis=-1, keepdims=True)
            o_ref[...] = y * lax.rsqrt(ms + EPS) * gf_ref[...]


def _mlp(x, g, w1, w2, g_final, *, final_norm, tm=1024, tf=512):
    t, d = x.shape
    dff = w1.shape[1]
    return pl.pallas_call(
        functools.partial(_mlp_kernel, final_norm=final_norm),
        grid=(t // tm, dff // tf),
        in_specs=[pl.BlockSpec((tm, d), lambda i, f: (i, 0)),
                  pl.BlockSpec((1, d), lambda i, f: (0, 0)),
                  pl.BlockSpec((d, tf), lambda i, f: (0, f)),
                  pl.BlockSpec((tf, d), lambda i, f: (f, 0)),
                  pl.BlockSpec((1, d), lambda i, f: (0, 0))],
        out_specs=pl.BlockSpec((tm, d), lambda i, f: (i, 0)),
        out_shape=jax.ShapeDtypeStruct((t, d), F32),
        scratch_shapes=[pltpu.VMEM((tm, d), BF16)],
        compiler_params=_cparams(("parallel", "arbitrary")),
        name="mlp",
    )(x, g, w1, w2, g_final)


def kernel(x, w_in, w_out, g_attn, g_mlp, w_mlp_in, w_mlp_out, rel_bias_table,
           diff_lam_q1, diff_lam_k1, diff_lam_q2, diff_lam_k2, diff_subln_g, g_final):
    batch, seq, d_model = x.shape
    depth = w_in.shape[0]
    tq_b = 256
    tq_c = 256

    i = np.arange(BLK)[:, None]
    c = np.arange(2 * BLK)[None, :]
    steps = i + BLK - c
    buckets_a = np.stack([_t5_bucket_np(steps * dil) for _, dil in DILATED_PATTERNS])
    nq_b = seq // tq_b
    dist = (np.arange(nq_b)[:, None, None] * tq_b + np.arange(tq_b)[None, None, :]
            - np.arange(tq_b)[None, :, None])
    buckets_b = _t5_bucket_np(dist)
    bias_a = _build_bias(rel_bias_table[:, :N_HEADS_A], jnp.asarray(buckets_a))
    bias_b = _build_bias(rel_bias_table[:, N_HEADS_A:], jnp.asarray(buckets_b))

    colscale = np.ones((1, w_in.shape[2]), np.float32)
    colscale[0, _QA * HEAD_DIM:_KA * HEAD_DIM] = 1.0 / math.sqrt(HEAD_DIM)
    colscale[0, _QB * HEAD_DIM:_KB * HEAD_DIM] = 1.0 / math.sqrt(HEAD_DIM // 2)
    colscale[0, _QC * HEAD_DIM:_KC * HEAD_DIM] = 1.0 / math.sqrt(HEAD_DIM)
    colscale = jnp.asarray(colscale)

    split_a = 3 * N_HEADS_A * HEAD_DIM
    split_b = split_a + 3 * N_HEADS_B * HEAD_DIM
    xf = x.reshape(batch * seq, d_model)
    for l in range(depth):
        lam_init = 0.8 - 0.6 * math.exp(-0.3 * l)
        lam_params = jnp.stack([diff_lam_q1[l], diff_lam_k1[l], diff_lam_q2[l], diff_lam_k2[l]]).astype(F32)
        wl = w_in[l]
        w_acb = jnp.concatenate([wl[:, :split_a], wl[:, split_b:], wl[:, split_a:split_b]], axis=1)
        proj = _rms_proj(xf, g_attn[l][None, :], w_acb.astype(BF16), colscale)
        ma = _dilated_attention(proj, bias_a, batch=batch, seq=seq)
        mb = _diff_attention(proj, bias_b, lam_params, diff_subln_g[l][:, None],
                             batch=batch, seq=seq, tq=tq_b, lam_init=lam_init)
        mc = _stick_attention(proj, batch=batch, seq=seq, tq=tq_c, heads_per_step=6)
        xf = _out_proj(xf, ma, mb, mc, w_out[l].astype(BF16))
        xf = _mlp(xf, g_mlp[l][None, :], w_mlp_in[l].astype(BF16), w_mlp_out[l].astype(BF16),
                  g_final[None, :], final_norm=(l == depth - 1))
    return xf.reshape(batch, seq, d_model)
```

```python
import functools
import math

import jax
import jax.numpy as jnp
import numpy as np
from jax import lax
from jax.experimental import pallas as pl
from jax.experimental.pallas import tpu as pltpu

HEAD_DIM = 128
N_HEADS_A = 6
N_HEADS_B = 4
N_HEADS_C = 6
DILATED_PATTERNS = ((128, 1), (512, 4), (2048, 16))
BLK = 128
NUM_BUCKETS = 32
MAX_DISTANCE = 2048
EPS = 1e-6
NEG_INF = -1e30
LOG2E = 1.4426950408889634
LN2 = 0.6931471805599453

F32 = jnp.float32
BF16 = jnp.bfloat16

_QA, _KA, _VA = 0, N_HEADS_A, 2 * N_HEADS_A
_QC = 3 * N_HEADS_A
_KC, _VC = _QC + N_HEADS_C, _QC + 2 * N_HEADS_C
_QB = _QC + 3 * N_HEADS_C
_KB, _VB = _QB + N_HEADS_B, _QB + 2 * N_HEADS_B

_VMEM_LIMIT = 56 * 1024 * 1024


def _cparams(sem, flags=None):
    return pltpu.CompilerParams(dimension_semantics=sem, vmem_limit_bytes=_VMEM_LIMIT, flags=flags)


def _dot_nt(a, b):
    return lax.dot_general(a, b, (((1,), (1,)), ((), ())), preferred_element_type=F32)


def _dot(a, b):
    return jnp.dot(a, b, preferred_element_type=F32)


def _staggered(n_chains, stages, states=None):
    states = [dict() for _ in range(n_chains)] if states is None else states
    for t in range(n_chains + len(stages) - 1):
        for k, stage in enumerate(stages):
            c = t - k
            if stage is not None and 0 <= c < n_chains:
                stage(c, states[c])
    return states


def _t5_bucket_np(dist):
    n = np.maximum(dist, 0)
    max_exact = NUM_BUCKETS // 2
    nf = np.maximum(n, max_exact).astype(np.float32)
    large = max_exact + (np.log(nf / np.float32(max_exact)) / np.float32(math.log(MAX_DISTANCE / max_exact))
                         * np.float32(NUM_BUCKETS - max_exact)).astype(np.int32)
    large = np.minimum(large, NUM_BUCKETS - 1)
    return np.where(n < max_exact, n, large).astype(np.int32)


def _bias_kernel(tab_ref, bkt_ref, o_ref, *, n_heads):
    b = bkt_ref[0]
    for h in range(n_heads):
        acc = jnp.zeros(b.shape, F32)
        for k in range(NUM_BUCKETS):
            acc = jnp.where(b == k, tab_ref[k, h], acc)
        o_ref[h, 0] = acc


def _build_bias(table, buckets):
    n_heads = table.shape[1]
    n, r, c = buckets.shape
    return pl.pallas_call(
        functools.partial(_bias_kernel, n_heads=n_heads),
        grid=(n,),
        in_specs=[pl.BlockSpec(memory_space=pltpu.SMEM),
                  pl.BlockSpec((1, r, c), lambda i: (i, 0, 0))],
        out_specs=pl.BlockSpec((n_heads, 1, r, c), lambda i: (0, i, 0, 0)),
        out_shape=jax.ShapeDtypeStruct((n_heads, n, r, c), F32),
        compiler_params=_cparams(("arbitrary",)),
        name="bias_table",
    )(table, buckets)


def _cast_kernel(w_ref, o_ref, *, col_segments):
    dst = 0
    for src, width in col_segments:
        o_ref[:, dst:dst + width] = w_ref[:, src:src + width].astype(o_ref.dtype)
        dst += width


def _cast_weights(w, *, block_bytes=8 * 1024 * 1024, col_segments=None):
    depth, r, c = w.shape
    col_segments = col_segments or [(0, c)]
    assert sum(width for _, width in col_segments) == c
    tr = r
    while tr * c * w.dtype.itemsize > block_bytes:
        assert tr % 2 == 0
        tr //= 2
    spec = pl.BlockSpec((None, tr, c), lambda l, i: (l, i, 0))
    return pl.pallas_call(
        functools.partial(_cast_kernel, col_segments=col_segments),
        grid=(depth, r // tr),
        in_specs=[spec],
        out_specs=spec,
        out_shape=jax.ShapeDtypeStruct(w.shape, BF16),
        compiler_params=_cparams(("parallel", "parallel")),
        name="cast_weights",
    )(w)


def _rms_proj_kernel(x_ref, g_ref, w_ref, cs_ref, o_ref, xn_ref):
    @pl.when(pl.program_id(1) == 0)
    def _():
        x = x_ref[...]
        ms = jnp.mean(x * x, axis=-1, keepdims=True)
        xn_ref[...] = (x * lax.rsqrt(ms + EPS) * g_ref[...]).astype(BF16)

    acc = _dot(xn_ref[...], w_ref[...])
    o_ref[...] = (acc * cs_ref[...]).astype(o_ref.dtype)


def _rms_proj(x, g, w, layer, colscale, *, tm=1024, tn=1024):
    t, d = x.shape
    n = w.shape[2]
    return pl.pallas_call(
        _rms_proj_kernel,
        grid=(t // tm, n // tn),
        in_specs=[pl.BlockSpec((tm, d), lambda i, j: (i, 0)),
                  pl.BlockSpec((1, d), lambda i, j: (0, 0)),
                  pl.BlockSpec((None, d, tn), lambda i, j: (layer, 0, j)),
                  pl.BlockSpec((1, tn), lambda i, j: (0, j))],
        out_specs=pl.BlockSpec((tm, tn), lambda i, j: (i, j)),
        out_shape=jax.ShapeDtypeStruct((t, n), BF16),
        scratch_shapes=[pltpu.VMEM((tm, d), BF16)],
        compiler_params=_cparams(("parallel", "arbitrary")),
        name="rms_proj",
    )(x, g, w, colscale)


def _dilated_kernel(q_ref, k_ref, v_ref, bias_ref, o_ref, qf, kf, vf, acc_s, m_s, l_s, *, seq, lag):
    qf[...] = q_ref[...].astype(F32)
    kf[...] = k_ref[...].astype(F32)
    vf[...] = v_ref[...].astype(F32)

    row = lax.broadcasted_iota(jnp.int32, (BLK, 2 * BLK), 0)
    col = lax.broadcasted_iota(jnp.int32, (BLK, 2 * BLK), 1)
    band_mask = (col >= row) & (col <= row + BLK)
    row0 = lax.broadcasted_iota(jnp.int32, (BLK, BLK), 0)
    col0 = lax.broadcasted_iota(jnp.int32, (BLK, BLK), 1)
    diag_mask = col0 <= row0

    def rows_of(start, size, dil):
        return pl.ds(start, size, stride=dil) if dil > 1 else pl.ds(start, size)

    def ld(ref, start, size, dil):
        return ref[rows_of(start, size, dil), :].astype(BF16)

    chains = []
    for p_idx, (window, dil) in enumerate(DILATED_PATTERNS):
        assert window // dil == BLK
        for r in range(dil):
            for n in range(seq // dil // BLK):
                chains.append((p_idx, dil, r + dil * BLK * n, None if n == 0 else r + dil * BLK * (n - 1)))

    def scores(c, st):
        p_idx, dil, q0, k0 = chains[c]
        qb = ld(qf, q0, BLK, dil)
        if k0 is None:
            s = _dot_nt(qb, ld(kf, q0, BLK, dil)) + bias_ref[0, p_idx, :, BLK:]
            st["s"] = jnp.where(diag_mask, s, NEG_INF)
        else:
            s = _dot_nt(qb, ld(kf, k0, 2 * BLK, dil)) + bias_ref[0, p_idx]
            st["s"] = jnp.where(band_mask, s, NEG_INF)

    def softmax(c, st):
        s = st.pop("s")
        st["m"] = jnp.max(s, axis=-1, keepdims=True)
        p = jnp.exp(s - st["m"])
        st["l"] = jnp.sum(p, axis=-1, keepdims=True)
        st["p"] = p.astype(BF16)

    def values(c, st):
        p_idx, dil, q0, k0 = chains[c]
        vb = ld(vf, q0, BLK, dil) if k0 is None else ld(vf, k0, 2 * BLK, dil)
        rows = rows_of(q0, BLK, dil)
        acc_s[p_idx, rows, :] = _dot(st.pop("p"), vb)
        m_s[p_idx, rows, :] = jnp.broadcast_to(st.pop("m"), (BLK, HEAD_DIM))
        l_s[p_idx, rows, :] = jnp.broadcast_to(st.pop("l"), (BLK, HEAD_DIM))

    _staggered(len(chains), [scores] + [None] * (lag - 1) + [softmax] + [None] * (lag - 1) + [values])

    chunk = 256

    def merge(ci, carry):
        rows = pl.ds(pl.multiple_of(ci * chunk, chunk), chunk)
        m0, m1, m2 = m_s[0, rows, :], m_s[1, rows, :], m_s[2, rows, :]
        mm = jnp.maximum(jnp.maximum(m0, m1), m2)
        w0, w1, w2 = jnp.exp(m0 - mm), jnp.exp(m1 - mm), jnp.exp(m2 - mm)
        num = w0 * acc_s[0, rows, :] + w1 * acc_s[1, rows, :] + w2 * acc_s[2, rows, :]
        den = w0 * l_s[0, rows, :] + w1 * l_s[1, rows, :] + w2 * l_s[2, rows, :]
        o_ref[rows, :] = (num / den).astype(o_ref.dtype)
        return carry

    lax.fori_loop(0, seq // chunk, merge, 0)


def _dilated_attention(proj, bias_a, *, batch, seq):
    t = proj.shape[0]
    n_pat = len(DILATED_PATTERNS)
    blk = lambda off: pl.BlockSpec((seq, HEAD_DIM), lambda b, h: (b, off + h))
    return pl.pallas_call(
        functools.partial(_dilated_kernel, seq=seq, lag=3),
        grid=(batch, N_HEADS_A),
        in_specs=[blk(_QA), blk(_KA), blk(_VA),
                  pl.BlockSpec((1, n_pat, BLK, 2 * BLK), lambda b, h: (h, 0, 0, 0))],
        out_specs=pl.BlockSpec((seq, HEAD_DIM), lambda b, h: (b, h)),
        out_shape=jax.ShapeDtypeStruct((t, N_HEADS_A * HEAD_DIM), BF16),
        scratch_shapes=[pltpu.VMEM((seq, HEAD_DIM), F32)] * 3
                       + [pltpu.VMEM((n_pat, seq, HEAD_DIM), F32)] * 3,
        compiler_params=_cparams(("parallel", "parallel")),
        name="dilated_attn",
    )(proj, proj, proj, bias_a)


def _diff_kernel(q_ref, k_ref, v_ref, bias_ref, lam_ref, g_ref, o_ref, vt_ref, *, tq, n_heads, seq, lam_init):
    i = pl.program_id(1)
    half = HEAD_DIM // 2
    heads = [slice(g * HEAD_DIM, (g + 1) * HEAD_DIM) for g in range(n_heads)]

    @pl.when(i == 0)
    def _():
        r = lax.broadcasted_iota(jnp.int32, (HEAD_DIM, HEAD_DIM), 0)
        c = lax.broadcasted_iota(jnp.int32, (HEAD_DIM, HEAD_DIM), 1)
        eye = (r == c).astype(BF16)

        def transpose_block(jb, carry):
            rows = pl.ds(pl.multiple_of(jb * tq, tq), tq)
            for g, hs in enumerate(heads):
                vt_ref[g, jb] = _dot_nt(eye, v_ref[rows, hs]).astype(BF16)
            return carry

        lax.fori_loop(0, seq // tq, transpose_block, 0)

    lane = lax.broadcasted_iota(jnp.int32, (tq, HEAD_DIM), 1)
    qqs = []
    for hs in heads:
        q = q_ref[:, hs]
        zero = jnp.zeros_like(q)
        qqs.append(jnp.concatenate([jnp.where(lane < half, q, zero), jnp.where(lane >= half, q, zero)], axis=0))
    key = lax.broadcasted_iota(jnp.int32, (tq, 2 * tq), 0)
    qry = lax.broadcasted_iota(jnp.int32, (tq, 2 * tq), 1)
    causal = key <= jnp.where(qry >= tq, qry - tq, qry)

    def step(js, carries, masked):
        n_chains = len(js) * n_heads
        states = [dict() for _ in range(n_chains)]

        def carry(c, name, idx):
            return carries[c][idx] if c < n_heads else states[c - n_heads][name]

        def scores(c, st):
            j, g = js[c // n_heads], c % n_heads
            rows = pl.ds(pl.multiple_of(j * tq, tq), tq)
            bias = bias_ref[g, i - j]
            s = _dot_nt(k_ref[rows, heads[g]], qqs[g]) + jnp.concatenate([bias, bias], axis=1)
            st["s"] = jnp.where(causal, s, NEG_INF) if masked else s

        def softmax(c, st):
            m, l = carry(c, "m", 0), carry(c, "l", 1)
            s = st.pop("s")
            st["m"] = jnp.maximum(m, jnp.max(s, axis=0, keepdims=True))
            st["alpha"] = jnp.exp(m - st["m"])
            p = jnp.exp(s - st["m"])
            st["l"] = st["alpha"] * l + jnp.sum(p, axis=0, keepdims=True)
            st["p"] = p.astype(BF16)

        def values(c, st):
            vt = vt_ref[c % n_heads, js[c // n_heads]]
            st["acc"] = st.pop("alpha") * carry(c, "acc", 2) + _dot(vt, st.pop("p"))

        _staggered(n_chains, [scores, softmax, values], states)
        return tuple((st["m"], st["l"], st["acc"]) for st in states[-n_heads:])

    init = tuple((jnp.full((1, 2 * tq), NEG_INF, F32), jnp.zeros((1, 2 * tq), F32),
                  jnp.zeros((HEAD_DIM, 2 * tq), F32)) for _ in heads)
    carries = lax.fori_loop(0, i // 2, lambda t, cr: step([2 * t, 2 * t + 1], cr, False), init)
    carries = lax.cond(i % 2 == 1, lambda cr: step([i - 1], cr, False), lambda cr: cr, carries)
    carries = step([i], carries, True)

    lp = lam_ref[...]
    lam = (jnp.exp(jnp.sum(lp[0:1] * lp[1:2], axis=-1, keepdims=True))
           - jnp.exp(jnp.sum(lp[2:3] * lp[3:4], axis=-1, keepdims=True)) + lam_init)
    for hs, (_, l, acc) in zip(heads, carries):
        o = acc / l
        out = o[:, :tq] - lam * o[:, tq:]
        ms = jnp.mean(out * out, axis=0, keepdims=True)
        y = out * lax.rsqrt(ms + EPS) * g_ref[...] * (1.0 - lam_init)
        o_ref[:, hs] = y.T.astype(o_ref.dtype)


def _diff_attention(proj, bias_b, lam_params, g, *, batch, seq, tq, lam_init):
    t = proj.shape[0]
    nq = seq // tq
    nh = N_HEADS_B
    width = nh * HEAD_DIM
    assert _QB % nh == 0 and _KB % nh == 0 and _VB % nh == 0
    kv = lambda off: pl.BlockSpec((seq, width), lambda b, i: (b, off // nh))
    return pl.pallas_call(
        functools.partial(_diff_kernel, tq=tq, n_heads=nh, seq=seq, lam_init=lam_init),
        grid=(batch, nq),
        in_specs=[pl.BlockSpec((tq, width), lambda b, i: (b * nq + i, _QB // nh)),
                  kv(_KB), kv(_VB),
                  pl.BlockSpec((nh, nq, tq, tq), lambda b, i: (0, 0, 0, 0)),
                  pl.BlockSpec((4, HEAD_DIM // 2), lambda b, i: (0, 0)),
                  pl.BlockSpec((HEAD_DIM, 1), lambda b, i: (0, 0))],
        out_specs=pl.BlockSpec((tq, width), lambda b, i: (b * nq + i, 0)),
        out_shape=jax.ShapeDtypeStruct((t, width), BF16),
        scratch_shapes=[pltpu.VMEM((nh, nq, HEAD_DIM, tq), BF16)],
        compiler_params=_cparams(("parallel", "arbitrary")),
        name="diff_attn",
    )(proj, proj, proj, bias_b, lam_params, g)


def _stick_kernel(q_ref, k_ref, v_ref, o_ref, *, tq, n_heads):
    i = pl.program_id(2)
    row = lax.broadcasted_iota(jnp.int32, (tq, tq), 0)
    col = lax.broadcasted_iota(jnp.int32, (tq, tq), 1)
    strict = col < row
    suffix = (row >= col).astype(BF16)
    heads = [slice(g * HEAD_DIM, (g + 1) * HEAD_DIM) for g in range(n_heads)]
    qs = [q_ref[:, hs] for hs in heads]

    def blocks(js, carries, masked):
        n_chains = len(js) * n_heads
        states = [dict() for _ in range(n_chains)]
        rows_of = [pl.ds(pl.multiple_of(j * tq, tq), tq) for j in js]

        def carry_c(c):
            return carries[c][0] if c < n_heads else states[c - n_heads]["c"]

        def carry_acc(c):
            return carries[c][1] if c < n_heads else states[c - n_heads]["acc"]

        def scores(c, st):
            st["z"] = _dot_nt(qs[c % n_heads], k_ref[rows_of[c // n_heads], heads[c % n_heads]])

        def log_break(c, st):
            z = st["z"]
            neg_abs = lax.bitcast_convert_type(lax.bitcast_convert_type(z, jnp.uint32) | jnp.uint32(0x80000000), F32)
            w = jnp.maximum(z, 0.0) + jnp.log(1.0 + jnp.exp(neg_abs))
            if masked:
                w = jnp.where(strict, w, 0.0)
            st["hi"] = w.astype(BF16)
            st["lo"] = (w - st["hi"].astype(F32)).astype(BF16)
            st["c"] = carry_c(c) + jnp.sum(w, axis=-1, keepdims=True)

        def suffix_sums(c, st):
            st["incl"] = _dot(st.pop("hi"), suffix) + _dot(st.pop("lo"), suffix)

        def weights(c, st):
            a = jnp.exp((st.pop("z") - carry_c(c)) - st.pop("incl"))
            if masked:
                a = jnp.where(strict, a, 0.0)
            st["a"] = a.astype(BF16)

        def values(c, st):
            st["acc"] = carry_acc(c) + _dot(st.pop("a"), v_ref[rows_of[c // n_heads], heads[c % n_heads]])

        _staggered(n_chains, [scores, log_break, suffix_sums, weights, values], states)
        return tuple((st["c"], st["acc"]) for st in states[-n_heads:])

    init = tuple((jnp.zeros((tq, 1), F32), jnp.zeros((tq, HEAD_DIM), F32)) for _ in heads)
    carries = blocks([i], init, True)
    carries = lax.fori_loop(0, i // 2, lambda t, cr: blocks([i - 1 - 2 * t, i - 2 - 2 * t], cr, False), carries)
    carries = lax.cond(i % 2 == 1, lambda cr: blocks([0], cr, False), lambda cr: cr, carries)
    for hs, (_, acc) in zip(heads, carries):
        o_ref[:, hs] = acc.astype(o_ref.dtype)


def _stick_attention(proj, *, batch, seq, tq, heads_per_step):
    t = proj.shape[0]
    nq = seq // tq
    g = heads_per_step
    width = g * HEAD_DIM
    kv = lambda off: pl.BlockSpec((seq, width), lambda b, h, i: (b, off // g + h))
    return pl.pallas_call(
        functools.partial(_stick_kernel, tq=tq, n_heads=g),
        grid=(batch, N_HEADS_C // g, nq),
        in_specs=[pl.BlockSpec((tq, width), lambda b, h, i: (b * nq + i, _QC // g + h)),
                  kv(_KC), kv(_VC)],
        out_specs=pl.BlockSpec((tq, width), lambda b, h, i: (b * nq + i, h)),
        out_shape=jax.ShapeDtypeStruct((t, N_HEADS_C * HEAD_DIM), BF16),
        compiler_params=_cparams(("parallel", "parallel", "arbitrary")),
        name="stick_attn",
    )(proj, proj, proj)


def _out_proj_kernel(x_ref, a_ref, b_ref, c_ref, w_ref, o_ref):
    ka, kb = a_ref.shape[1], b_ref.shape[1]
    acc = (_dot(a_ref[...], w_ref[:ka]) + _dot(b_ref[...], w_ref[ka:ka + kb])
           + _dot(c_ref[...], w_ref[ka + kb:]))
    o_ref[...] = x_ref[...] + acc


def _out_proj(x, ma, mb, mc, w, layer, *, tm=512):
    t, d = x.shape
    act = lambda k: pl.BlockSpec((tm, k), lambda i: (i, 0))
    return pl.pallas_call(
        _out_proj_kernel,
        grid=(t // tm,),
        in_specs=[act(d), act(ma.shape[1]), act(mb.shape[1]), act(mc.shape[1]),
                  pl.BlockSpec((None,) + w.shape[1:], lambda i: (layer, 0, 0))],
        out_specs=act(d),
        out_shape=jax.ShapeDtypeStruct((t, d), F32),
        compiler_params=_cparams(("parallel",)),
        name="out_proj",
    )(x, ma, mb, mc, w)


def _mlp_kernel(x_ref, g_ref, w1_ref, w2_ref, gf_ref, o_ref, xn_ref, *, final_norm):
    f = pl.program_id(1)

    @pl.when(f == 0)
    def _():
        x = x_ref[...]
        ms = jnp.mean(x * x, axis=-1, keepdims=True)
        xn_ref[...] = (x * lax.rsqrt(ms + EPS) * g_ref[...]).astype(BF16)
        o_ref[...] = x

    h = jnp.maximum(_dot(xn_ref[...], w1_ref[...]), 0.0)
    o_ref[...] += _dot((h * h).astype(BF16), w2_ref[...])

    if final_norm:
        @pl.when(f == pl.num_programs(1) - 1)
        def _():
            y = o_ref[...]
            ms = jnp.mean(y * y, axis=-1, keepdims=True)
            o_ref[...] = y * lax.rsqrt(ms + EPS) * gf_ref[...]


def _mlp(x, g, w1, w2, layer, g_final, *, final_norm, tm=1024, tf=512):
    t, d = x.shape
    dff = w1.shape[2]
    return pl.pallas_call(
        functools.partial(_mlp_kernel, final_norm=final_norm),
        grid=(t // tm, dff // tf),
        in_specs=[pl.BlockSpec((tm, d), lambda i, f: (i, 0)),
                  pl.BlockSpec((1, d), lambda i, f: (0, 0)),
                  pl.BlockSpec((None, d, tf), lambda i, f: (layer, 0, f)),
                  pl.BlockSpec((None, tf, d), lambda i, f: (layer, f, 0)),
                  pl.BlockSpec((1, d), lambda i, f: (0, 0))],
        out_specs=pl.BlockSpec((tm, d), lambda i, f: (i, 0)),
        out_shape=jax.ShapeDtypeStruct((t, d), F32),
        scratch_shapes=[pltpu.VMEM((tm, d), BF16)],
        compiler_params=_cparams(("parallel", "arbitrary")),
        name="mlp",
    )(x, g, w1, w2, g_final)


def kernel(x, w_in, w_out, g_attn, g_mlp, w_mlp_in, w_mlp_out, rel_bias_table,
           diff_lam_q1, diff_lam_k1, diff_lam_q2, diff_lam_k2, diff_subln_g, g_final):
    batch, seq, d_model = x.shape
    depth = w_in.shape[0]
    tq_b = 256
    tq_c = 256

    i = np.arange(BLK)[:, None]
    c = np.arange(2 * BLK)[None, :]
    steps = i + BLK - c
    buckets_a = np.stack([_t5_bucket_np(steps * dil) for _, dil in DILATED_PATTERNS])
    nq_b = seq // tq_b
    dist = (np.arange(nq_b)[:, None, None] * tq_b + np.arange(tq_b)[None, None, :]
            - np.arange(tq_b)[None, :, None])
    buckets_b = _t5_bucket_np(dist)
    bias_a = _build_bias(rel_bias_table[:, :N_HEADS_A], jnp.asarray(buckets_a))
    bias_b = _build_bias(rel_bias_table[:, N_HEADS_A:], jnp.asarray(buckets_b))

    colscale = np.ones((1, w_in.shape[2]), np.float32)
    colscale[0, _QA * HEAD_DIM:_KA * HEAD_DIM] = 1.0 / math.sqrt(HEAD_DIM)
    colscale[0, _QB * HEAD_DIM:_KB * HEAD_DIM] = 1.0 / math.sqrt(HEAD_DIM // 2)
    colscale[0, _QC * HEAD_DIM:_KC * HEAD_DIM] = 1.0 / math.sqrt(HEAD_DIM)
    colscale = jnp.asarray(colscale)

    width_a, width_b, width_c = (3 * n * HEAD_DIM for n in (N_HEADS_A, N_HEADS_B, N_HEADS_C))
    w_in_b = _cast_weights(w_in, col_segments=[(0, width_a), (width_a + width_b, width_c), (width_a, width_b)])
    w_out_b = _cast_weights(w_out)
    w1_b = _cast_weights(w_mlp_in)
    w2_b = _cast_weights(w_mlp_out)

    xf = x.reshape(batch * seq, d_model)
    for l in range(depth):
        lam_init = 0.8 - 0.6 * math.exp(-0.3 * l)
        lam_params = jnp.stack([diff_lam_q1[l], diff_lam_k1[l], diff_lam_q2[l], diff_lam_k2[l]]).astype(F32)
        proj = _rms_proj(xf, g_attn[l][None, :], w_in_b, l, colscale)
        ma = _dilated_attention(proj, bias_a, batch=batch, seq=seq)
        mb = _diff_attention(proj, bias_b, lam_params, diff_subln_g[l][:, None],
                             batch=batch, seq=seq, tq=tq_b, lam_init=lam_init)
        mc = _stick_attention(proj, batch=batch, seq=seq, tq=tq_c, heads_per_step=6)
        xf = _out_proj(xf, ma, mb, mc, w_out_b, l)
        xf = _mlp(xf, g_mlp[l][None, :], w1_b, w2_b, l, g_final[None, :], final_norm=(l == depth - 1))
    return xf.reshape(batch, seq, d_model)
```

```python
import functools
import math

import jax
import jax.numpy as jnp
import numpy as np
from jax import lax
from jax.experimental import pallas as pl
from jax.experimental.pallas import tpu as pltpu

HEAD_DIM = 128
N_HEADS_A = 6
N_HEADS_B = 4
N_HEADS_C = 6
DILATED_PATTERNS = ((128, 1), (512, 4), (2048, 16))
BLK = 128
NUM_BUCKETS = 32
MAX_DISTANCE = 2048
EPS = 1e-6
NEG_INF = -1e30
LOG2E = 1.4426950408889634
LN2 = 0.6931471805599453

F32 = jnp.float32
BF16 = jnp.bfloat16

_QA, _KA, _VA = 0, N_HEADS_A, 2 * N_HEADS_A
_QC = 3 * N_HEADS_A
_KC, _VC = _QC + N_HEADS_C, _QC + 2 * N_HEADS_C
_QB = _QC + 3 * N_HEADS_C
_KB, _VB = _QB + N_HEADS_B, _QB + 2 * N_HEADS_B

_VMEM_LIMIT = 56 * 1024 * 1024


def _cparams(sem, flags=None):
    return pltpu.CompilerParams(dimension_semantics=sem, vmem_limit_bytes=_VMEM_LIMIT, flags=flags)


def _dot_nt(a, b):
    return lax.dot_general(a, b, (((1,), (1,)), ((), ())), preferred_element_type=F32)


def _dot(a, b):
    return jnp.dot(a, b, preferred_element_type=F32)


def _staggered(n_chains, stages, states=None):
    states = [dict() for _ in range(n_chains)] if states is None else states
    for t in range(n_chains + len(stages) - 1):
        for k, stage in enumerate(stages):
            c = t - k
            if stage is not None and 0 <= c < n_chains:
                stage(c, states[c])
    return states


def _t5_bucket_np(dist):
    n = np.maximum(dist, 0)
    max_exact = NUM_BUCKETS // 2
    nf = np.maximum(n, max_exact).astype(np.float32)
    large = max_exact + (np.log(nf / np.float32(max_exact)) / np.float32(math.log(MAX_DISTANCE / max_exact))
                         * np.float32(NUM_BUCKETS - max_exact)).astype(np.int32)
    large = np.minimum(large, NUM_BUCKETS - 1)
    return np.where(n < max_exact, n, large).astype(np.int32)


def _bias_kernel(tab_ref, bkt_ref, o_ref, *, n_heads):
    b = bkt_ref[0]
    for h in range(n_heads):
        acc = jnp.zeros(b.shape, F32)
        for k in range(NUM_BUCKETS):
            acc = jnp.where(b == k, tab_ref[k, h], acc)
        o_ref[h, 0] = acc


def _build_bias(table, buckets):
    n_heads = table.shape[1]
    n, r, c = buckets.shape
    return pl.pallas_call(
        functools.partial(_bias_kernel, n_heads=n_heads),
        grid=(n,),
        in_specs=[pl.BlockSpec(memory_space=pltpu.SMEM),
                  pl.BlockSpec((1, r, c), lambda i: (i, 0, 0))],
        out_specs=pl.BlockSpec((n_heads, 1, r, c), lambda i: (0, i, 0, 0)),
        out_shape=jax.ShapeDtypeStruct((n_heads, n, r, c), F32),
        compiler_params=_cparams(("arbitrary",)),
        name="bias_table",
    )(table, buckets)


def _rms_proj_kernel(x_ref, g_ref, w_ref, cs_ref, o_ref, xn_ref):
    @pl.when(pl.program_id(1) == 0)
    def _():
        x = x_ref[...]
        ms = jnp.mean(x * x, axis=-1, keepdims=True)
        xn_ref[...] = (x * lax.rsqrt(ms + EPS) * g_ref[...]).astype(BF16)

    acc = _dot(xn_ref[...], w_ref[...].astype(BF16))
    o_ref[...] = (acc * cs_ref[...]).astype(o_ref.dtype)


def _rms_proj(x, g, w, layer, colscale, src_tiles, *, tm=1024):
    t, d = x.shape
    n = w.shape[2]
    tn = n // len(src_tiles)

    def src_tile(j):
        idx = src_tiles[-1]
        for k in reversed(range(len(src_tiles) - 1)):
            idx = jnp.where(j == k, src_tiles[k], idx)
        return idx

    return pl.pallas_call(
        _rms_proj_kernel,
        grid=(t // tm, n // tn),
        in_specs=[pl.BlockSpec((tm, d), lambda i, j: (i, 0)),
                  pl.BlockSpec((1, d), lambda i, j: (0, 0)),
                  pl.BlockSpec((None, d, tn), lambda i, j: (layer, 0, src_tile(j))),
                  pl.BlockSpec((1, tn), lambda i, j: (0, j))],
        out_specs=pl.BlockSpec((tm, tn), lambda i, j: (i, j)),
        out_shape=jax.ShapeDtypeStruct((t, n), BF16),
        scratch_shapes=[pltpu.VMEM((tm, d), BF16)],
        compiler_params=_cparams(("parallel", "arbitrary")),
        name="rms_proj",
    )(x, g, w, colscale)


def _dilated_kernel(q_ref, k_ref, v_ref, bias_ref, o_ref, qf, kf, vf, acc_s, m_s, l_s, *, seq, lag):
    qf[...] = q_ref[...].astype(F32)
    kf[...] = k_ref[...].astype(F32)
    vf[...] = v_ref[...].astype(F32)

    row = lax.broadcasted_iota(jnp.int32, (BLK, 2 * BLK), 0)
    col = lax.broadcasted_iota(jnp.int32, (BLK, 2 * BLK), 1)
    band_mask = (col >= row) & (col <= row + BLK)
    row0 = lax.broadcasted_iota(jnp.int32, (BLK, BLK), 0)
    col0 = lax.broadcasted_iota(jnp.int32, (BLK, BLK), 1)
    diag_mask = col0 <= row0

    def rows_of(start, size, dil):
        return pl.ds(start, size, stride=dil) if dil > 1 else pl.ds(start, size)

    def ld(ref, start, size, dil):
        return ref[rows_of(start, size, dil), :].astype(BF16)

    chains = []
    for p_idx, (window, dil) in enumerate(DILATED_PATTERNS):
        assert window // dil == BLK
        for r in range(dil):
            for n in range(seq // dil // BLK):
                chains.append((p_idx, dil, r + dil * BLK * n, None if n == 0 else r + dil * BLK * (n - 1)))

    def scores(c, st):
        p_idx, dil, q0, k0 = chains[c]
        qb = ld(qf, q0, BLK, dil)
        if k0 is None:
            s = _dot_nt(qb, ld(kf, q0, BLK, dil)) + bias_ref[0, p_idx, :, BLK:]
            st["s"] = jnp.where(diag_mask, s, NEG_INF)
        else:
            s = _dot_nt(qb, ld(kf, k0, 2 * BLK, dil)) + bias_ref[0, p_idx]
            st["s"] = jnp.where(band_mask, s, NEG_INF)

    def softmax(c, st):
        s = st.pop("s")
        st["m"] = jnp.max(s, axis=-1, keepdims=True)
        p = jnp.exp(s - st["m"])
        st["l"] = jnp.sum(p, axis=-1, keepdims=True)
        st["p"] = p.astype(BF16)

    def values(c, st):
        p_idx, dil, q0, k0 = chains[c]
        vb = ld(vf, q0, BLK, dil) if k0 is None else ld(vf, k0, 2 * BLK, dil)
        rows = rows_of(q0, BLK, dil)
        acc_s[p_idx, rows, :] = _dot(st.pop("p"), vb)
        m_s[p_idx, rows, :] = jnp.broadcast_to(st.pop("m"), (BLK, HEAD_DIM))
        l_s[p_idx, rows, :] = jnp.broadcast_to(st.pop("l"), (BLK, HEAD_DIM))

    _staggered(len(chains), [scores] + [None] * (lag - 1) + [softmax] + [None] * (lag - 1) + [values])

    chunk = 256

    def merge(ci, carry):
        rows = pl.ds(pl.multiple_of(ci * chunk, chunk), chunk)
        m0, m1, m2 = m_s[0, rows, :], m_s[1, rows, :], m_s[2, rows, :]
        mm = jnp.maximum(jnp.maximum(m0, m1), m2)
        w0, w1, w2 = jnp.exp(m0 - mm), jnp.exp(m1 - mm), jnp.exp(m2 - mm)
        num = w0 * acc_s[0, rows, :] + w1 * acc_s[1, rows, :] + w2 * acc_s[2, rows, :]
        den = w0 * l_s[0, rows, :] + w1 * l_s[1, rows, :] + w2 * l_s[2, rows, :]
        o_ref[rows, :] = (num / den).astype(o_ref.dtype)
        return carry

    lax.fori_loop(0, seq // chunk, merge, 0)


def _dilated_attention(proj, bias_a, *, batch, seq):
    t = proj.shape[0]
    n_pat = len(DILATED_PATTERNS)
    blk = lambda off: pl.BlockSpec((seq, HEAD_DIM), lambda b, h: (b, off + h))
    return pl.pallas_call(
        functools.partial(_dilated_kernel, seq=seq, lag=3),
        grid=(batch, N_HEADS_A),
        in_specs=[blk(_QA), blk(_KA), blk(_VA),
                  pl.BlockSpec((1, n_pat, BLK, 2 * BLK), lambda b, h: (h, 0, 0, 0))],
        out_specs=pl.BlockSpec((seq, HEAD_DIM), lambda b, h: (b, h)),
        out_shape=jax.ShapeDtypeStruct((t, N_HEADS_A * HEAD_DIM), BF16),
        scratch_shapes=[pltpu.VMEM((seq, HEAD_DIM), F32)] * 3
                       + [pltpu.VMEM((n_pat, seq, HEAD_DIM), F32)] * 3,
        compiler_params=_cparams(("parallel", "parallel")),
        name="dilated_attn",
    )(proj, proj, proj, bias_a)


def _diff_kernel(q_ref, k_ref, v_ref, bias_ref, lam_ref, g_ref, o_ref, vt_ref, *, tq, n_heads, seq, lam_init):
    i = pl.program_id(1)
    half = HEAD_DIM // 2
    heads = [slice(g * HEAD_DIM, (g + 1) * HEAD_DIM) for g in range(n_heads)]

    @pl.when(i == 0)
    def _():
        r = lax.broadcasted_iota(jnp.int32, (HEAD_DIM, HEAD_DIM), 0)
        c = lax.broadcasted_iota(jnp.int32, (HEAD_DIM, HEAD_DIM), 1)
        eye = (r == c).astype(BF16)

        def transpose_block(jb, carry):
            rows = pl.ds(pl.multiple_of(jb * tq, tq), tq)
            for g, hs in enumerate(heads):
                vt_ref[g, jb] = _dot_nt(eye, v_ref[rows, hs]).astype(BF16)
            return carry

        lax.fori_loop(0, seq // tq, transpose_block, 0)

    lane = lax.broadcasted_iota(jnp.int32, (tq, HEAD_DIM), 1)
    qqs = []
    for hs in heads:
        q = q_ref[:, hs]
        zero = jnp.zeros_like(q)
        qqs.append(jnp.concatenate([jnp.where(lane < half, q, zero), jnp.where(lane >= half, q, zero)], axis=0))
    key = lax.broadcasted_iota(jnp.int32, (tq, 2 * tq), 0)
    qry = lax.broadcasted_iota(jnp.int32, (tq, 2 * tq), 1)
    causal = key <= jnp.where(qry >= tq, qry - tq, qry)

    def step(js, carries, masked):
        n_chains = len(js) * n_heads
        states = [dict() for _ in range(n_chains)]

        def carry(c, name, idx):
            return carries[c][idx] if c < n_heads else states[c - n_heads][name]

        def scores(c, st):
            j, g = js[c // n_heads], c % n_heads
            rows = pl.ds(pl.multiple_of(j * tq, tq), tq)
            bias = bias_ref[g, i - j]
            s = _dot_nt(k_ref[rows, heads[g]], qqs[g]) + jnp.concatenate([bias, bias], axis=1)
            st["s"] = jnp.where(causal, s, NEG_INF) if masked else s

        def softmax(c, st):
            m, l = carry(c, "m", 0), carry(c, "l", 1)
            s = st.pop("s")
            st["m"] = jnp.maximum(m, jnp.max(s, axis=0, keepdims=True))
            st["alpha"] = jnp.exp(m - st["m"])
            p = jnp.exp(s - st["m"])
            st["l"] = st["alpha"] * l + jnp.sum(p, axis=0, keepdims=True)
            st["p"] = p.astype(BF16)

        def values(c, st):
            vt = vt_ref[c % n_heads, js[c // n_heads]]
            st["acc"] = st.pop("alpha") * carry(c, "acc", 2) + _dot(vt, st.pop("p"))

        _staggered(n_chains, [scores, softmax, values], states)
        return tuple((st["m"], st["l"], st["acc"]) for st in states[-n_heads:])

    init = tuple((jnp.full((1, 2 * tq), NEG_INF, F32), jnp.zeros((1, 2 * tq), F32),
                  jnp.zeros((HEAD_DIM, 2 * tq), F32)) for _ in heads)
    carries = lax.fori_loop(0, i // 2, lambda t, cr: step([2 * t, 2 * t + 1], cr, False), init)
    carries = lax.cond(i % 2 == 1, lambda cr: step([i - 1], cr, False), lambda cr: cr, carries)
    carries = step([i], carries, True)

    lp = lam_ref[...]
    lam = (jnp.exp(jnp.sum(lp[0:1] * lp[1:2], axis=-1, keepdims=True))
           - jnp.exp(jnp.sum(lp[2:3] * lp[3:4], axis=-1, keepdims=True)) + lam_init)
    for hs, (_, l, acc) in zip(heads, carries):
        o = acc / l
        out = o[:, :tq] - lam * o[:, tq:]
        ms = jnp.mean(out * out, axis=0, keepdims=True)
        y = out * lax.rsqrt(ms + EPS) * g_ref[...] * (1.0 - lam_init)
        o_ref[:, hs] = y.T.astype(o_ref.dtype)


def _diff_attention(proj, bias_b, lam_params, g, *, batch, seq, tq, lam_init):
    t = proj.shape[0]
    nq = seq // tq
    nh = N_HEADS_B
    width = nh * HEAD_DIM
    assert _QB % nh == 0 and _KB % nh == 0 and _VB % nh == 0
    kv = lambda off: pl.BlockSpec((seq, width), lambda b, i: (b, off // nh))
    return pl.pallas_call(
        functools.partial(_diff_kernel, tq=tq, n_heads=nh, seq=seq, lam_init=lam_init),
        grid=(batch, nq),
        in_specs=[pl.BlockSpec((tq, width), lambda b, i: (b * nq + i, _QB // nh)),
                  kv(_KB), kv(_VB),
                  pl.BlockSpec((nh, nq, tq, tq), lambda b, i: (0, 0, 0, 0)),
                  pl.BlockSpec((4, HEAD_DIM // 2), lambda b, i: (0, 0)),
                  pl.BlockSpec((HEAD_DIM, 1), lambda b, i: (0, 0))],
        out_specs=pl.BlockSpec((tq, width), lambda b, i: (b * nq + i, 0)),
        out_shape=jax.ShapeDtypeStruct((t, width), BF16),
        scratch_shapes=[pltpu.VMEM((nh, nq, HEAD_DIM, tq), BF16)],
        compiler_params=_cparams(("parallel", "arbitrary")),
        name="diff_attn",
    )(proj, proj, proj, bias_b, lam_params, g)


def _stick_kernel(q_ref, k_ref, v_ref, o_ref, *, tq, n_heads):
    i = pl.program_id(2)
    row = lax.broadcasted_iota(jnp.int32, (tq, tq), 0)
    col = lax.broadcasted_iota(jnp.int32, (tq, tq), 1)
    strict = col < row
    suffix = (row >= col).astype(BF16)
    heads = [slice(g * HEAD_DIM, (g + 1) * HEAD_DIM) for g in range(n_heads)]
    qs = [q_ref[:, hs] for hs in heads]

    def blocks(js, carries, masked):
        n_chains = len(js) * n_heads
        states = [dict() for _ in range(n_chains)]
        rows_of = [pl.ds(pl.multiple_of(j * tq, tq), tq) for j in js]

        def carry_c(c):
            return carries[c][0] if c < n_heads else states[c - n_heads]["c"]

        def carry_acc(c):
            return carries[c][1] if c < n_heads else states[c - n_heads]["acc"]

        def scores(c, st):
            st["z"] = _dot_nt(qs[c % n_heads], k_ref[rows_of[c // n_heads], heads[c % n_heads]])

        def log_break(c, st):
            z = st["z"]
            neg_abs = lax.bitcast_convert_type(lax.bitcast_convert_type(z, jnp.uint32) | jnp.uint32(0x80000000), F32)
            w = jnp.maximum(z, 0.0) + jnp.log(1.0 + jnp.exp(neg_abs))
            if masked:
                w = jnp.where(strict, w, 0.0)
            st["hi"] = w.astype(BF16)
            st["lo"] = (w - st["hi"].astype(F32)).astype(BF16)
            st["c"] = carry_c(c) + jnp.sum(w, axis=-1, keepdims=True)

        def suffix_sums(c, st):
            st["incl"] = _dot(st.pop("hi"), suffix) + _dot(st.pop("lo"), suffix)

        def weights(c, st):
            a = jnp.exp((st.pop("z") - carry_c(c)) - st.pop("incl"))
            if masked:
                a = jnp.where(strict, a, 0.0)
            st["a"] = a.astype(BF16)

        def values(c, st):
            st["acc"] = carry_acc(c) + _dot(st.pop("a"), v_ref[rows_of[c // n_heads], heads[c % n_heads]])

        _staggered(n_chains, [scores, log_break, suffix_sums, weights, values], states)
        return tuple((st["c"], st["acc"]) for st in states[-n_heads:])

    init = tuple((jnp.zeros((tq, 1), F32), jnp.zeros((tq, HEAD_DIM), F32)) for _ in heads)
    carries = blocks([i], init, True)
    carries = lax.fori_loop(0, i // 2, lambda t, cr: blocks([i - 1 - 2 * t, i - 2 - 2 * t], cr, False), carries)
    carries = lax.cond(i % 2 == 1, lambda cr: blocks([0], cr, False), lambda cr: cr, carries)
    for hs, (_, acc) in zip(heads, carries):
        o_ref[:, hs] = acc.astype(o_ref.dtype)


def _stick_attention(proj, *, batch, seq, tq, heads_per_step):
    t = proj.shape[0]
    nq = seq // tq
    g = heads_per_step
    width = g * HEAD_DIM
    kv = lambda off: pl.BlockSpec((seq, width), lambda b, h, i: (b, off // g + h))
    return pl.pallas_call(
        functools.partial(_stick_kernel, tq=tq, n_heads=g),
        grid=(batch, N_HEADS_C // g, nq),
        in_specs=[pl.BlockSpec((tq, width), lambda b, h, i: (b * nq + i, _QC // g + h)),
                  kv(_KC), kv(_VC)],
        out_specs=pl.BlockSpec((tq, width), lambda b, h, i: (b * nq + i, h)),
        out_shape=jax.ShapeDtypeStruct((t, N_HEADS_C * HEAD_DIM), BF16),
        compiler_params=_cparams(("parallel", "parallel", "arbitrary")),
        name="stick_attn",
    )(proj, proj, proj)


def _out_proj_kernel(x_ref, a_ref, b_ref, c_ref, w_ref, o_ref):
    ka, kb = a_ref.shape[1], b_ref.shape[1]
    acc = (_dot(a_ref[...], w_ref[:ka].astype(BF16)) + _dot(b_ref[...], w_ref[ka:ka + kb].astype(BF16))
           + _dot(c_ref[...], w_ref[ka + kb:].astype(BF16)))
    o_ref[...] = x_ref[...] + acc


def _out_proj(x, ma, mb, mc, w, layer, *, tm=512):
    t, d = x.shape
    act = lambda k: pl.BlockSpec((tm, k), lambda i: (i, 0))
    return pl.pallas_call(
        _out_proj_kernel,
        grid=(t // tm,),
        in_specs=[act(d), act(ma.shape[1]), act(mb.shape[1]), act(mc.shape[1]),
                  pl.BlockSpec((None,) + w.shape[1:], lambda i: (layer, 0, 0))],
        out_specs=act(d),
        out_shape=jax.ShapeDtypeStruct((t, d), F32),
        compiler_params=_cparams(("parallel",)),
        name="out_proj",
    )(x, ma, mb, mc, w)


def _mlp_kernel(x_ref, g_ref, w1_ref, w2_ref, gf_ref, o_ref, xn_ref, *, final_norm):
    f = pl.program_id(1)

    @pl.when(f == 0)
    def _():
        x = x_ref[...]
        ms = jnp.mean(x * x, axis=-1, keepdims=True)
        xn_ref[...] = (x * lax.rsqrt(ms + EPS) * g_ref[...]).astype(BF16)
        o_ref[...] = x

    h = jnp.maximum(_dot(xn_ref[...], w1_ref[...].astype(BF16)), 0.0)
    o_ref[...] += _dot((h * h).astype(BF16), w2_ref[...].astype(BF16))

    if final_norm:
        @pl.when(f == pl.num_programs(1) - 1)
        def _():
            y = o_ref[...]
            ms = jnp.mean(y * y, axis=-1, keepdims=True)
            o_ref[...] = y * lax.rsqrt(ms + EPS) * gf_ref[...]


def _mlp(x, g, w1, w2, layer, g_final, *, final_norm, tm=1024, tf=512):
    t, d = x.shape
    dff = w1.shape[2]
    return pl.pallas_call(
        functools.partial(_mlp_kernel, final_norm=final_norm),
        grid=(t // tm, dff // tf),
        in_specs=[pl.BlockSpec((tm, d), lambda i, f: (i, 0)),
                  pl.BlockSpec((1, d), lambda i, f: (0, 0)),
                  pl.BlockSpec((None, d, tf), lambda i, f: (layer, 0, f)),
                  pl.BlockSpec((None, tf, d), lambda i, f: (layer, f, 0)),
                  pl.BlockSpec((1, d), lambda i, f: (0, 0))],
        out_specs=pl.BlockSpec((tm, d), lambda i, f: (i, 0)),
        out_shape=jax.ShapeDtypeStruct((t, d), F32),
        scratch_shapes=[pltpu.VMEM((tm, d), BF16)],
        compiler_params=_cparams(("parallel", "arbitrary")),
        name="mlp",
    )(x, g, w1, w2, g_final)


def kernel(x, w_in, w_out, g_attn, g_mlp, w_mlp_in, w_mlp_out, rel_bias_table,
           diff_lam_q1, diff_lam_k1, diff_lam_q2, diff_lam_k2, diff_subln_g, g_final):
    batch, seq, d_model = x.shape
    depth = w_in.shape[0]
    tq_b = 256
    tq_c = 256

    i = np.arange(BLK)[:, None]
    c = np.arange(2 * BLK)[None, :]
    steps = i + BLK - c
    buckets_a = np.stack([_t5_bucket_np(steps * dil) for _, dil in DILATED_PATTERNS])
    nq_b = seq // tq_b
    dist = (np.arange(nq_b)[:, None, None] * tq_b + np.arange(tq_b)[None, None, :]
            - np.arange(tq_b)[None, :, None])
    buckets_b = _t5_bucket_np(dist)
    bias_a = _build_bias(rel_bias_table[:, :N_HEADS_A], jnp.asarray(buckets_a))
    bias_b = _build_bias(rel_bias_table[:, N_HEADS_A:], jnp.asarray(buckets_b))

    colscale = np.ones((1, w_in.shape[2]), np.float32)
    colscale[0, _QA * HEAD_DIM:_KA * HEAD_DIM] = 1.0 / math.sqrt(HEAD_DIM)
    colscale[0, _QB * HEAD_DIM:_KB * HEAD_DIM] = 1.0 / math.sqrt(HEAD_DIM // 2)
    colscale[0, _QC * HEAD_DIM:_KC * HEAD_DIM] = 1.0 / math.sqrt(HEAD_DIM)
    colscale = jnp.asarray(colscale)

    tile = N_HEADS_A * HEAD_DIM
    assert N_HEADS_C == N_HEADS_A and (3 * N_HEADS_B * HEAD_DIM) % tile == 0
    tiles_a, tiles_b = 3, 3 * N_HEADS_B * HEAD_DIM // tile
    src_tiles = (tuple(range(tiles_a)) + tuple(range(tiles_a + tiles_b, 2 * tiles_a + tiles_b))
                 + tuple(range(tiles_a, tiles_a + tiles_b)))

    xf = x.reshape(batch * seq, d_model)
    for l in range(depth):
        lam_init = 0.8 - 0.6 * math.exp(-0.3 * l)
        lam_params = jnp.stack([diff_lam_q1[l], diff_lam_k1[l], diff_lam_q2[l], diff_lam_k2[l]]).astype(F32)
        proj = _rms_proj(xf, g_attn[l][None, :], w_in, l, colscale, src_tiles)
        ma = _dilated_attention(proj, bias_a, batch=batch, seq=seq)
        mb = _diff_attention(proj, bias_b, lam_params, diff_subln_g[l][:, None],
                             batch=batch, seq=seq, tq=tq_b, lam_init=lam_init)
        mc = _stick_attention(proj, batch=batch, seq=seq, tq=tq_c, heads_per_step=6)
        xf = _out_proj(xf, ma, mb, mc, w_out, l)
        xf = _mlp(xf, g_mlp[l][None, :], w_mlp_in, w_mlp_out, l, g_final[None, :], final_norm=(l == depth - 1))
    return xf.reshape(batch, seq, d_model)
```

```python
import functools
import math

import jax
import jax.numpy as jnp
import numpy as np
from jax import lax
from jax.experimental import pallas as pl
from jax.experimental.pallas import tpu as pltpu

HEAD_DIM = 128
N_HEADS_A = 6
N_HEADS_B = 4
N_HEADS_C = 6
DILATED_PATTERNS = ((128, 1), (512, 4), (2048, 16))
BLK = 128
NUM_BUCKETS = 32
MAX_DISTANCE = 2048
EPS = 1e-6
NEG_INF = -1e30
ONES_ROWS = 16

F32 = jnp.float32
BF16 = jnp.bfloat16

_QA, _KA, _VA = 0, N_HEADS_A, 2 * N_HEADS_A
_QC = 3 * N_HEADS_A
_KC, _VC = _QC + N_HEADS_C, _QC + 2 * N_HEADS_C
_QB = _QC + 3 * N_HEADS_C
_KB, _VB = _QB + N_HEADS_B, _QB + 2 * N_HEADS_B

_VMEM_LIMIT = 56 * 1024 * 1024


def _cparams(sem, flags=None):
    return pltpu.CompilerParams(dimension_semantics=sem, vmem_limit_bytes=_VMEM_LIMIT, flags=flags)


def _dot_nt(a, b):
    return lax.dot_general(a, b, (((1,), (1,)), ((), ())), preferred_element_type=F32)


def _dot(a, b):
    return jnp.dot(a, b, preferred_element_type=F32)


def _staggered(n_chains, stages, states=None):
    states = [dict() for _ in range(n_chains)] if states is None else states
    for t in range(n_chains + len(stages) - 1):
        for k, stage in enumerate(stages):
            c = t - k
            if stage is not None and 0 <= c < n_chains:
                stage(c, states[c])
    return states


def _t5_bucket_np(dist):
    n = np.maximum(dist, 0)
    max_exact = NUM_BUCKETS // 2
    nf = np.maximum(n, max_exact).astype(np.float32)
    large = max_exact + (np.log(nf / np.float32(max_exact)) / np.float32(math.log(MAX_DISTANCE / max_exact))
                         * np.float32(NUM_BUCKETS - max_exact)).astype(np.int32)
    large = np.minimum(large, NUM_BUCKETS - 1)
    return np.where(n < max_exact, n, large).astype(np.int32)


def _bias_kernel(tab_ref, bkt_ref, o_ref, *, n_heads):
    b = bkt_ref[0]
    for h in range(n_heads):
        acc = jnp.zeros(b.shape, F32)
        for k in range(NUM_BUCKETS):
            acc = jnp.where(b == k, tab_ref[k, h], acc)
        o_ref[h, 0] = acc


def _build_bias(table, buckets):
    n_heads = table.shape[1]
    n, r, c = buckets.shape
    return pl.pallas_call(
        functools.partial(_bias_kernel, n_heads=n_heads),
        grid=(n,),
        in_specs=[pl.BlockSpec(memory_space=pltpu.SMEM),
                  pl.BlockSpec((1, r, c), lambda i: (i, 0, 0))],
        out_specs=pl.BlockSpec((n_heads, 1, r, c), lambda i: (0, i, 0, 0)),
        out_shape=jax.ShapeDtypeStruct((n_heads, n, r, c), F32),
        compiler_params=_cparams(("arbitrary",)),
        name="bias_table",
    )(table, buckets)


def _rms_proj_kernel(x_ref, g_ref, w_ref, cs_ref, o_ref, xn_ref):
    @pl.when(pl.program_id(1) == 0)
    def _():
        x = x_ref[...]
        ms = jnp.mean(x * x, axis=-1, keepdims=True)
        xn_ref[...] = (x * lax.rsqrt(ms + EPS) * g_ref[...]).astype(BF16)

    acc = _dot(xn_ref[...], w_ref[...].astype(BF16))
    o_ref[...] = (acc * cs_ref[...]).astype(o_ref.dtype)


def _rms_proj(x, g, w, layer, colscale, src_tiles, *, tm=1024):
    t, d = x.shape
    n = w.shape[2]
    tn = n // len(src_tiles)

    def src_tile(j):
        idx = src_tiles[-1]
        for k in reversed(range(len(src_tiles) - 1)):
            idx = jnp.where(j == k, src_tiles[k], idx)
        return idx

    return pl.pallas_call(
        _rms_proj_kernel,
        grid=(t // tm, n // tn),
        in_specs=[pl.BlockSpec((tm, d), lambda i, j: (i, 0)),
                  pl.BlockSpec((1, d), lambda i, j: (0, 0)),
                  pl.BlockSpec((None, d, tn), lambda i, j: (layer, 0, src_tile(j))),
                  pl.BlockSpec((1, tn), lambda i, j: (0, j))],
        out_specs=pl.BlockSpec((tm, tn), lambda i, j: (i, j)),
        out_shape=jax.ShapeDtypeStruct((t, n), BF16),
        scratch_shapes=[pltpu.VMEM((tm, d), BF16)],
        compiler_params=_cparams(("parallel", "arbitrary")),
        name="rms_proj",
    )(x, g, w, colscale)


def _dilated_kernel(q_ref, k_ref, v_ref, bias_ref, o_ref, qf, kf, vf, acc_s, m_s, l_s, *, seq, lag):
    qf[...] = q_ref[...].astype(F32)
    kf[...] = k_ref[...].astype(F32)
    vf[...] = v_ref[...].astype(F32)

    row = lax.broadcasted_iota(jnp.int32, (BLK, 2 * BLK), 0)
    col = lax.broadcasted_iota(jnp.int32, (BLK, 2 * BLK), 1)
    band_mask = (col >= row) & (col <= row + BLK)
    row0 = lax.broadcasted_iota(jnp.int32, (BLK, BLK), 0)
    col0 = lax.broadcasted_iota(jnp.int32, (BLK, BLK), 1)
    diag_mask = col0 <= row0

    def rows_of(start, size, dil):
        return pl.ds(start, size, stride=dil) if dil > 1 else pl.ds(start, size)

    def ld(ref, start, size, dil):
        return ref[rows_of(start, size, dil), :].astype(BF16)

    chains = []
    for p_idx, (window, dil) in enumerate(DILATED_PATTERNS):
        assert window // dil == BLK
        for r in range(dil):
            for n in range(seq // dil // BLK):
                chains.append((p_idx, dil, r + dil * BLK * n, None if n == 0 else r + dil * BLK * (n - 1)))

    def scores(c, st):
        p_idx, dil, q0, k0 = chains[c]
        qb = ld(qf, q0, BLK, dil)
        if k0 is None:
            s = _dot_nt(qb, ld(kf, q0, BLK, dil)) + bias_ref[0, p_idx, :, BLK:]
            st["s"] = jnp.where(diag_mask, s, NEG_INF)
        else:
            s = _dot_nt(qb, ld(kf, k0, 2 * BLK, dil)) + bias_ref[0, p_idx]
            st["s"] = jnp.where(band_mask, s, NEG_INF)

    def softmax(c, st):
        s = st.pop("s")
        st["m"] = jnp.max(s, axis=-1, keepdims=True)
        p = jnp.exp(s - st["m"])
        st["l"] = jnp.sum(p, axis=-1, keepdims=True)
        st["p"] = p.astype(BF16)

    def values(c, st):
        p_idx, dil, q0, k0 = chains[c]
        vb = ld(vf, q0, BLK, dil) if k0 is None else ld(vf, k0, 2 * BLK, dil)
        rows = rows_of(q0, BLK, dil)
        acc_s[p_idx, rows, :] = _dot(st.pop("p"), vb)
        m_s[p_idx, rows, :] = jnp.broadcast_to(st.pop("m"), (BLK, HEAD_DIM))
        l_s[p_idx, rows, :] = jnp.broadcast_to(st.pop("l"), (BLK, HEAD_DIM))

    _staggered(len(chains), [scores] + [None] * (lag - 1) + [softmax] + [None] * (lag - 1) + [values])

    chunk = 256

    def merge(ci, carry):
        rows = pl.ds(pl.multiple_of(ci * chunk, chunk), chunk)
        m0, m1, m2 = m_s[0, rows, :], m_s[1, rows, :], m_s[2, rows, :]
        mm = jnp.maximum(jnp.maximum(m0, m1), m2)
        w0, w1, w2 = jnp.exp(m0 - mm), jnp.exp(m1 - mm), jnp.exp(m2 - mm)
        num = w0 * acc_s[0, rows, :] + w1 * acc_s[1, rows, :] + w2 * acc_s[2, rows, :]
        den = w0 * l_s[0, rows, :] + w1 * l_s[1, rows, :] + w2 * l_s[2, rows, :]
        o_ref[rows, :] = (num / den).astype(o_ref.dtype)
        return carry

    lax.fori_loop(0, seq // chunk, merge, 0)


def _dilated_attention(proj, bias_a, *, batch, seq):
    t = proj.shape[0]
    n_pat = len(DILATED_PATTERNS)
    blk = lambda off: pl.BlockSpec((seq, HEAD_DIM), lambda b, h: (b, off + h))
    return pl.pallas_call(
        functools.partial(_dilated_kernel, seq=seq, lag=3),
        grid=(batch, N_HEADS_A),
        in_specs=[blk(_QA), blk(_KA), blk(_VA),
                  pl.BlockSpec((1, n_pat, BLK, 2 * BLK), lambda b, h: (h, 0, 0, 0))],
        out_specs=pl.BlockSpec((seq, HEAD_DIM), lambda b, h: (b, h)),
        out_shape=jax.ShapeDtypeStruct((t, N_HEADS_A * HEAD_DIM), BF16),
        scratch_shapes=[pltpu.VMEM((seq, HEAD_DIM), F32)] * 3
                       + [pltpu.VMEM((n_pat, seq, HEAD_DIM), F32)] * 3,
        compiler_params=_cparams(("parallel", "parallel")),
        name="dilated_attn",
    )(proj, proj, proj, bias_a)


def _diff_kernel(q_ref, k_ref, v_ref, bias_ref, lam_ref, g_ref, o_ref, vt_ref, *, tq, n_heads, seq, lam_init):
    i = pl.program_id(1)
    half = HEAD_DIM // 2
    heads = [slice(g * HEAD_DIM, (g + 1) * HEAD_DIM) for g in range(n_heads)]

    @pl.when(i == 0)
    def _():
        r = lax.broadcasted_iota(jnp.int32, (HEAD_DIM, HEAD_DIM), 0)
        c = lax.broadcasted_iota(jnp.int32, (HEAD_DIM, HEAD_DIM), 1)
        eye = (r == c).astype(BF16)

        def transpose_block(jb, carry):
            rows = pl.ds(pl.multiple_of(jb * tq, tq), tq)
            for g, hs in enumerate(heads):
                vt_ref[g, jb, :HEAD_DIM] = _dot_nt(eye, v_ref[rows, hs]).astype(BF16)
                vt_ref[g, jb, HEAD_DIM:] = jnp.ones((ONES_ROWS, tq), BF16)
            return carry

        lax.fori_loop(0, seq // tq, transpose_block, 0)

    lane = lax.broadcasted_iota(jnp.int32, (tq, HEAD_DIM), 1)
    qqs = []
    for hs in heads:
        q = q_ref[:, hs]
        zero = jnp.zeros_like(q)
        qqs.append(jnp.concatenate([jnp.where(lane < half, q, zero), jnp.where(lane >= half, q, zero)], axis=0))
    key = lax.broadcasted_iota(jnp.int32, (tq, 2 * tq), 0)
    qry = lax.broadcasted_iota(jnp.int32, (tq, 2 * tq), 1)
    causal = key <= jnp.where(qry >= tq, qry - tq, qry)

    def step(js, carries, masked):
        n_chains = len(js) * n_heads
        states = [dict() for _ in range(n_chains)]

        def carry(c, name, idx):
            return carries[c][idx] if c < n_heads else states[c - n_heads][name]

        def scores(c, st):
            j, g = js[c // n_heads], c % n_heads
            rows = pl.ds(pl.multiple_of(j * tq, tq), tq)
            bias = bias_ref[g, i - j]
            s = _dot_nt(k_ref[rows, heads[g]], qqs[g]) + jnp.concatenate([bias, bias], axis=1)
            st["s"] = jnp.where(causal, s, NEG_INF) if masked else s

        def softmax(c, st):
            m = carry(c, "m", 0)
            s = st.pop("s")
            st["m"] = jnp.maximum(m, jnp.max(s, axis=0, keepdims=True))
            st["alpha"] = jnp.exp(m - st["m"])
            st["p"] = jnp.exp(s - st["m"]).astype(BF16)

        def values(c, st):
            vt = vt_ref[c % n_heads, js[c // n_heads]]
            st["acc"] = st.pop("alpha") * carry(c, "acc", 1) + _dot(vt, st.pop("p"))

        _staggered(n_chains, [scores, softmax, values], states)
        return tuple((st["m"], st["acc"]) for st in states[-n_heads:])

    init = tuple((jnp.full((1, 2 * tq), NEG_INF, F32), jnp.zeros((HEAD_DIM + ONES_ROWS, 2 * tq), F32))
                 for _ in heads)
    carries = lax.fori_loop(0, i // 2, lambda t, cr: step([2 * t, 2 * t + 1], cr, False), init)
    carries = lax.cond(i % 2 == 1, lambda cr: step([i - 1], cr, False), lambda cr: cr, carries)
    carries = step([i], carries, True)

    lp = lam_ref[...]
    lam = (jnp.exp(jnp.sum(lp[0:1] * lp[1:2], axis=-1, keepdims=True))
           - jnp.exp(jnp.sum(lp[2:3] * lp[3:4], axis=-1, keepdims=True)) + lam_init)
    for hs, (_, acc) in zip(heads, carries):
        o = acc[:HEAD_DIM] / acc[HEAD_DIM:HEAD_DIM + 1]
        out = o[:, :tq] - lam * o[:, tq:]
        ms = jnp.mean(out * out, axis=0, keepdims=True)
        y = out * lax.rsqrt(ms + EPS) * g_ref[...] * (1.0 - lam_init)
        o_ref[:, hs] = y.T.astype(o_ref.dtype)


def _diff_attention(proj, bias_b, lam_params, g, *, batch, seq, tq, lam_init):
    t = proj.shape[0]
    nq = seq // tq
    nh = N_HEADS_B
    width = nh * HEAD_DIM
    assert _QB % nh == 0 and _KB % nh == 0 and _VB % nh == 0
    kv = lambda off: pl.BlockSpec((seq, width), lambda b, i: (b, off // nh))
    return pl.pallas_call(
        functools.partial(_diff_kernel, tq=tq, n_heads=nh, seq=seq, lam_init=lam_init),
        grid=(batch, nq),
        in_specs=[pl.BlockSpec((tq, width), lambda b, i: (b * nq + i, _QB // nh)),
                  kv(_KB), kv(_VB),
                  pl.BlockSpec((nh, nq, tq, tq), lambda b, i: (0, 0, 0, 0)),
                  pl.BlockSpec((4, HEAD_DIM // 2), lambda b, i: (0, 0)),
                  pl.BlockSpec((HEAD_DIM, 1), lambda b, i: (0, 0))],
        out_specs=pl.BlockSpec((tq, width), lambda b, i: (b * nq + i, 0)),
        out_shape=jax.ShapeDtypeStruct((t, width), BF16),
        scratch_shapes=[pltpu.VMEM((nh, nq, HEAD_DIM + ONES_ROWS, tq), BF16)],
        compiler_params=_cparams(("parallel", "arbitrary")),
        name="diff_attn",
    )(proj, proj, proj, bias_b, lam_params, g)


def _stick_kernel(q_ref, k_ref, v_ref, o_ref, *, tq, n_heads):
    i = pl.program_id(2)
    row = lax.broadcasted_iota(jnp.int32, (tq, tq), 0)
    col = lax.broadcasted_iota(jnp.int32, (tq, tq), 1)
    strict = col < row
    suffix = (row >= col).astype(BF16)
    suffix2 = jnp.concatenate([suffix, suffix], axis=0)
    heads = [slice(g * HEAD_DIM, (g + 1) * HEAD_DIM) for g in range(n_heads)]
    qs = [q_ref[:, hs] for hs in heads]

    def blocks(js, carries, masked):
        n_chains = len(js) * n_heads
        states = [dict() for _ in range(n_chains)]
        rows_of = [pl.ds(pl.multiple_of(j * tq, tq), tq) for j in js]

        def carry_c(c):
            return carries[c][0] if c < n_heads else states[c - n_heads]["c"]

        def carry_acc(c):
            return carries[c][1] if c < n_heads else states[c - n_heads]["acc"]

        def scores(c, st):
            st["z"] = _dot_nt(qs[c % n_heads], k_ref[rows_of[c // n_heads], heads[c % n_heads]])

        def log_break(c, st):
            z = st["z"]
            neg_abs = lax.bitcast_convert_type(lax.bitcast_convert_type(z, jnp.uint32) | jnp.uint32(0x80000000), F32)
            w = jnp.maximum(z, 0.0) + jnp.log(1.0 + jnp.exp(neg_abs))
            if masked:
                w = jnp.where(strict, w, 0.0)
            hi = w.astype(BF16)
            st["hilo"] = jnp.concatenate([hi, (w - hi.astype(F32)).astype(BF16)], axis=1)

        def suffix_sums(c, st):
            st["incl"] = _dot(st.pop("hilo"), suffix2)
            st["c"] = carry_c(c) + st["incl"][:, 0:1]

        def weights(c, st):
            a = jnp.exp((st.pop("z") - carry_c(c)) - st.pop("incl"))
            if masked:
                a = jnp.where(strict, a, 0.0)
            st["a"] = a.astype(BF16)

        def values(c, st):
            st["acc"] = carry_acc(c) + _dot(st.pop("a"), v_ref[rows_of[c // n_heads], heads[c % n_heads]])

        _staggered(n_chains, [scores, log_break, suffix_sums, weights, values], states)
        return tuple((st["c"], st["acc"]) for st in states[-n_heads:])

    init = tuple((jnp.zeros((tq, 1), F32), jnp.zeros((tq, HEAD_DIM), F32)) for _ in heads)
    carries = blocks([i], init, True)
    carries = lax.fori_loop(0, i // 2, lambda t, cr: blocks([i - 1 - 2 * t, i - 2 - 2 * t], cr, False), carries)
    carries = lax.cond(i % 2 == 1, lambda cr: blocks([0], cr, False), lambda cr: cr, carries)
    for hs, (_, acc) in zip(heads, carries):
        o_ref[:, hs] = acc.astype(o_ref.dtype)


def _stick_attention(proj, *, batch, seq, tq, heads_per_step):
    t = proj.shape[0]
    nq = seq // tq
    g = heads_per_step
    width = g * HEAD_DIM
    kv = lambda off: pl.BlockSpec((seq, width), lambda b, h, i: (b, off // g + h))
    return pl.pallas_call(
        functools.partial(_stick_kernel, tq=tq, n_heads=g),
        grid=(batch, N_HEADS_C // g, nq),
        in_specs=[pl.BlockSpec((tq, width), lambda b, h, i: (b * nq + i, _QC // g + h)),
                  kv(_KC), kv(_VC)],
        out_specs=pl.BlockSpec((tq, width), lambda b, h, i: (b * nq + i, h)),
        out_shape=jax.ShapeDtypeStruct((t, N_HEADS_C * HEAD_DIM), BF16),
        compiler_params=_cparams(("parallel", "parallel", "arbitrary")),
        name="stick_attn",
    )(proj, proj, proj)


def _out_proj_kernel(x_ref, a_ref, b_ref, c_ref, w_ref, o_ref):
    ka, kb = a_ref.shape[1], b_ref.shape[1]
    acc = (_dot(a_ref[...], w_ref[:ka].astype(BF16)) + _dot(b_ref[...], w_ref[ka:ka + kb].astype(BF16))
           + _dot(c_ref[...], w_ref[ka + kb:].astype(BF16)))
    o_ref[...] = x_ref[...] + acc


def _out_proj(x, ma, mb, mc, w, layer, *, tm=512):
    t, d = x.shape
    act = lambda k: pl.BlockSpec((tm, k), lambda i: (i, 0))
    return pl.pallas_call(
        _out_proj_kernel,
        grid=(t // tm,),
        in_specs=[act(d), act(ma.shape[1]), act(mb.shape[1]), act(mc.shape[1]),
                  pl.BlockSpec((None,) + w.shape[1:], lambda i: (layer, 0, 0))],
        out_specs=act(d),
        out_shape=jax.ShapeDtypeStruct((t, d), F32),
        compiler_params=_cparams(("parallel",)),
        name="out_proj",
    )(x, ma, mb, mc, w)


def _mlp_kernel(x_ref, g_ref, w1_ref, w2_ref, gf_ref, o_ref, xn_ref, *, final_norm):
    f = pl.program_id(1)

    @pl.when(f == 0)
    def _():
        x = x_ref[...]
        ms = jnp.mean(x * x, axis=-1, keepdims=True)
        xn_ref[...] = (x * lax.rsqrt(ms + EPS) * g_ref[...]).astype(BF16)
        o_ref[...] = x

    h = jnp.maximum(_dot(xn_ref[...], w1_ref[...].astype(BF16)), 0.0)
    o_ref[...] += _dot((h * h).astype(BF16), w2_ref[...].astype(BF16))

    if final_norm:
        @pl.when(f == pl.num_programs(1) - 1)
        def _():
            y = o_ref[...]
            ms = jnp.mean(y * y, axis=-1, keepdims=True)
            o_ref[...] = y * lax.rsqrt(ms + EPS) * gf_ref[...]


def _mlp(x, g, w1, w2, layer, g_final, *, final_norm, tm=1024, tf=512):
    t, d = x.shape
    dff = w1.shape[2]
    return pl.pallas_call(
        functools.partial(_mlp_kernel, final_norm=final_norm),
        grid=(t // tm, dff // tf),
        in_specs=[pl.BlockSpec((tm, d), lambda i, f: (i, 0)),
                  pl.BlockSpec((1, d), lambda i, f: (0, 0)),
                  pl.BlockSpec((None, d, tf), lambda i, f: (layer, 0, f)),
                  pl.BlockSpec((None, tf, d), lambda i, f: (layer, f, 0)),
                  pl.BlockSpec((1, d), lambda i, f: (0, 0))],
        out_specs=pl.BlockSpec((tm, d), lambda i, f: (i, 0)),
        out_shape=jax.ShapeDtypeStruct((t, d), F32),
        scratch_shapes=[pltpu.VMEM((tm, d), BF16)],
        compiler_params=_cparams(("parallel", "arbitrary")),
        name="mlp",
    )(x, g, w1, w2, g_final)


def kernel(x, w_in, w_out, g_attn, g_mlp, w_mlp_in, w_mlp_out, rel_bias_table,
           diff_lam_q1, diff_lam_k1, diff_lam_q2, diff_lam_k2, diff_subln_g, g_final):
    batch, seq, d_model = x.shape
    depth = w_in.shape[0]
    tq_b = 256
    tq_c = 256

    i = np.arange(BLK)[:, None]
    c = np.arange(2 * BLK)[None, :]
    steps = i + BLK - c
    buckets_a = np.stack([_t5_bucket_np(steps * dil) for _, dil in DILATED_PATTERNS])
    nq_b = seq // tq_b
    dist = (np.arange(nq_b)[:, None, None] * tq_b + np.arange(tq_b)[None, None, :]
            - np.arange(tq_b)[None, :, None])
    buckets_b = _t5_bucket_np(dist)
    bias_a = _build_bias(rel_bias_table[:, :N_HEADS_A], jnp.asarray(buckets_a))
    bias_b = _build_bias(rel_bias_table[:, N_HEADS_A:], jnp.asarray(buckets_b))

    colscale = np.ones((1, w_in.shape[2]), np.float32)
    colscale[0, _QA * HEAD_DIM:_KA * HEAD_DIM] = 1.0 / math.sqrt(HEAD_DIM)
    colscale[0, _QB * HEAD_DIM:_KB * HEAD_DIM] = 1.0 / math.sqrt(HEAD_DIM // 2)
    colscale[0, _QC * HEAD_DIM:_KC * HEAD_DIM] = 1.0 / math.sqrt(HEAD_DIM)
    colscale = jnp.asarray(colscale)

    tile = N_HEADS_A * HEAD_DIM
    assert N_HEADS_C == N_HEADS_A and (3 * N_HEADS_B * HEAD_DIM) % tile == 0
    tiles_a, tiles_b = 3, 3 * N_HEADS_B * HEAD_DIM // tile
    src_tiles = (tuple(range(tiles_a)) + tuple(range(tiles_a + tiles_b, 2 * tiles_a + tiles_b))
                 + tuple(range(tiles_a, tiles_a + tiles_b)))

    xf = x.reshape(batch * seq, d_model)
    for l in range(depth):
        lam_init = 0.8 - 0.6 * math.exp(-0.3 * l)
        lam_params = jnp.stack([diff_lam_q1[l], diff_lam_k1[l], diff_lam_q2[l], diff_lam_k2[l]]).astype(F32)
        proj = _rms_proj(xf, g_attn[l][None, :], w_in, l, colscale, src_tiles)
        ma = _dilated_attention(proj, bias_a, batch=batch, seq=seq)
        mb = _diff_attention(proj, bias_b, lam_params, diff_subln_g[l][:, None],
                             batch=batch, seq=seq, tq=tq_b, lam_init=lam_init)
        mc = _stick_attention(proj, batch=batch, seq=seq, tq=tq_c, heads_per_step=6)
        xf = _out_proj(xf, ma, mb, mc, w_out, l)
        xf = _mlp(xf, g_mlp[l][None, :], w_mlp_in, w_mlp_out, l, g_final[None, :], final_norm=(l == depth - 1))
    return xf.reshape(batch, seq, d_model)
```

```python
import functools
import math

import jax
import jax.numpy as jnp
import numpy as np
from jax import lax
from jax.experimental import pallas as pl
from jax.experimental.pallas import tpu as pltpu

HEAD_DIM = 128
N_HEADS_A = 6
N_HEADS_B = 4
N_HEADS_C = 6
DILATED_PATTERNS = ((128, 1), (512, 4), (2048, 16))
BLK = 128
NUM_BUCKETS = 32
MAX_DISTANCE = 2048
EPS = 1e-6
NEG_INF = -1e30
ONES_ROWS = 16

F32 = jnp.float32
BF16 = jnp.bfloat16

_QA, _KA, _VA = 0, N_HEADS_A, 2 * N_HEADS_A
_QC = 3 * N_HEADS_A
_KC, _VC = _QC + N_HEADS_C, _QC + 2 * N_HEADS_C
_QB = _QC + 3 * N_HEADS_C
_KB, _VB = _QB + N_HEADS_B, _QB + 2 * N_HEADS_B

_VMEM_LIMIT = 56 * 1024 * 1024


def _cparams(sem):
    return pltpu.CompilerParams(dimension_semantics=sem, vmem_limit_bytes=_VMEM_LIMIT)


def _dot_nt(a, b):
    return lax.dot_general(a, b, (((1,), (1,)), ((), ())), preferred_element_type=F32)


def _dot(a, b):
    return jnp.dot(a, b, preferred_element_type=F32)


def _staggered(n_chains, stages, states=None):
    states = [dict() for _ in range(n_chains)] if states is None else states
    for t in range(n_chains + len(stages) - 1):
        for k, stage in enumerate(stages):
            c = t - k
            if stage is not None and 0 <= c < n_chains:
                stage(c, states[c])
    return states


def _t5_bucket_np(dist):
    n = np.maximum(dist, 0)
    max_exact = NUM_BUCKETS // 2
    nf = np.maximum(n, max_exact).astype(np.float32)
    large = max_exact + (np.log(nf / np.float32(max_exact)) / np.float32(math.log(MAX_DISTANCE / max_exact))
                         * np.float32(NUM_BUCKETS - max_exact)).astype(np.int32)
    large = np.minimum(large, NUM_BUCKETS - 1)
    return np.where(n < max_exact, n, large).astype(np.int32)


def _bias_kernel(tab_ref, bkt_ref, o_ref, *, n_heads, tile_buckets):
    for t, present in enumerate(tile_buckets):
        b = bkt_ref[t]
        for h in range(n_heads):
            acc = jnp.full(b.shape, tab_ref[present[0], h], F32)
            for k in present[1:]:
                acc = jnp.where(b == k, tab_ref[k, h], acc)
            o_ref[h, t] = acc


def _build_bias(table, buckets):
    n_heads = table.shape[1]
    n, r, c = buckets.shape
    tile_buckets = tuple(tuple(int(k) for k in np.unique(buckets[t])) for t in range(n))
    return pl.pallas_call(
        functools.partial(_bias_kernel, n_heads=n_heads, tile_buckets=tile_buckets),
        in_specs=[pl.BlockSpec(memory_space=pltpu.SMEM),
                  pl.BlockSpec(memory_space=pltpu.VMEM)],
        out_specs=pl.BlockSpec(memory_space=pltpu.VMEM),
        out_shape=jax.ShapeDtypeStruct((n_heads, n, r, c), F32),
        compiler_params=pltpu.CompilerParams(vmem_limit_bytes=_VMEM_LIMIT),
        name="bias_table",
    )(table, jnp.asarray(buckets))


def _rms_proj_kernel(x_ref, g_ref, w_ref, cs_ref, o_ref, xn_ref):
    @pl.when(pl.program_id(1) == 0)
    def _():
        x = x_ref[...]
        ms = jnp.mean(x * x, axis=-1, keepdims=True)
        xn_ref[...] = (x * lax.rsqrt(ms + EPS) * g_ref[...]).astype(BF16)

    acc = _dot(xn_ref[...], w_ref[...].astype(BF16))
    o_ref[...] = (acc * cs_ref[...]).astype(o_ref.dtype)


def _rms_proj(x, g, w, layer, colscale, src_tiles, *, tm=1024):
    t, d = x.shape
    n = w.shape[2]
    tn = n // len(src_tiles)

    def src_tile(j):
        idx = src_tiles[-1]
        for k in reversed(range(len(src_tiles) - 1)):
            idx = jnp.where(j == k, src_tiles[k], idx)
        return idx

    return pl.pallas_call(
        _rms_proj_kernel,
        grid=(t // tm, n // tn),
        in_specs=[pl.BlockSpec((tm, d), lambda i, j: (i, 0)),
                  pl.BlockSpec((1, d), lambda i, j: (0, 0)),
                  pl.BlockSpec((None, d, tn), lambda i, j: (layer, 0, src_tile(j))),
                  pl.BlockSpec((1, tn), lambda i, j: (0, j))],
        out_specs=pl.BlockSpec((tm, tn), lambda i, j: (i, j)),
        out_shape=jax.ShapeDtypeStruct((t, n), BF16),
        scratch_shapes=[pltpu.VMEM((tm, d), BF16)],
        compiler_params=_cparams(("parallel", "arbitrary")),
        name="rms_proj",
    )(x, g, w, colscale)


def _dilated_kernel(q_ref, k_ref, v_ref, bias_ref, o_ref, qf, kf, vf, acc_s, m_s, l_s, *, seq, lag):
    qf[...] = q_ref[...].astype(F32)
    kf[...] = k_ref[...].astype(F32)
    vf[...] = v_ref[...].astype(F32)

    row = lax.broadcasted_iota(jnp.int32, (BLK, 2 * BLK), 0)
    col = lax.broadcasted_iota(jnp.int32, (BLK, 2 * BLK), 1)
    band_mask = (col >= row) & (col <= row + BLK)
    row0 = lax.broadcasted_iota(jnp.int32, (BLK, BLK), 0)
    col0 = lax.broadcasted_iota(jnp.int32, (BLK, BLK), 1)
    diag_mask = col0 <= row0

    def rows_of(start, size, dil):
        return pl.ds(start, size, stride=dil) if dil > 1 else pl.ds(start, size)

    def ld(ref, start, size, dil):
        return ref[rows_of(start, size, dil), :].astype(BF16)

    chains = []
    for p_idx, (window, dil) in enumerate(DILATED_PATTERNS):
        assert window // dil == BLK
        for r in range(dil):
            for n in range(seq // dil // BLK):
                chains.append((p_idx, dil, r + dil * BLK * n, None if n == 0 else r + dil * BLK * (n - 1)))

    def scores(c, st):
        p_idx, dil, q0, k0 = chains[c]
        qb = ld(qf, q0, BLK, dil)
        if k0 is None:
            s = _dot_nt(qb, ld(kf, q0, BLK, dil)) + bias_ref[0, p_idx, :, BLK:]
            st["s"] = jnp.where(diag_mask, s, NEG_INF)
        else:
            s = _dot_nt(qb, ld(kf, k0, 2 * BLK, dil)) + bias_ref[0, p_idx]
            st["s"] = jnp.where(band_mask, s, NEG_INF)

    def softmax(c, st):
        s = st.pop("s")
        st["m"] = jnp.max(s, axis=-1, keepdims=True)
        p = jnp.exp(s - st["m"])
        st["l"] = jnp.sum(p, axis=-1, keepdims=True)
        st["p"] = p.astype(BF16)

    def values(c, st):
        p_idx, dil, q0, k0 = chains[c]
        vb = ld(vf, q0, BLK, dil) if k0 is None else ld(vf, k0, 2 * BLK, dil)
        rows = rows_of(q0, BLK, dil)
        acc_s[p_idx, rows, :] = _dot(st.pop("p"), vb)
        m_s[p_idx, rows, :] = jnp.broadcast_to(st.pop("m"), (BLK, HEAD_DIM))
        l_s[p_idx, rows, :] = jnp.broadcast_to(st.pop("l"), (BLK, HEAD_DIM))

    _staggered(len(chains), [scores] + [None] * (lag - 1) + [softmax] + [None] * (lag - 1) + [values])

    chunk = 256

    def merge(ci, carry):
        rows = pl.ds(pl.multiple_of(ci * chunk, chunk), chunk)
        m0, m1, m2 = m_s[0, rows, :], m_s[1, rows, :], m_s[2, rows, :]
        mm = jnp.maximum(jnp.maximum(m0, m1), m2)
        w0, w1, w2 = jnp.exp(m0 - mm), jnp.exp(m1 - mm), jnp.exp(m2 - mm)
        num = w0 * acc_s[0, rows, :] + w1 * acc_s[1, rows, :] + w2 * acc_s[2, rows, :]
        den = w0 * l_s[0, rows, :] + w1 * l_s[1, rows, :] + w2 * l_s[2, rows, :]
        o_ref[rows, :] = (num / den).astype(o_ref.dtype)
        return carry

    lax.fori_loop(0, seq // chunk, merge, 0)


def _dilated_attention(proj, bias_a, *, batch, seq):
    t = proj.shape[0]
    n_pat = len(DILATED_PATTERNS)
    blk = lambda off: pl.BlockSpec((seq, HEAD_DIM), lambda b, h: (b, off + h))
    return pl.pallas_call(
        functools.partial(_dilated_kernel, seq=seq, lag=3),
        grid=(batch, N_HEADS_A),
        in_specs=[blk(_QA), blk(_KA), blk(_VA),
                  pl.BlockSpec((1, n_pat, BLK, 2 * BLK), lambda b, h: (h, 0, 0, 0))],
        out_specs=pl.BlockSpec((seq, HEAD_DIM), lambda b, h: (b, h)),
        out_shape=jax.ShapeDtypeStruct((t, N_HEADS_A * HEAD_DIM), BF16),
        scratch_shapes=[pltpu.VMEM((seq, HEAD_DIM), F32)] * 3
                       + [pltpu.VMEM((n_pat, seq, HEAD_DIM), F32)] * 3,
        compiler_params=_cparams(("parallel", "parallel")),
        name="dilated_attn",
    )(proj, proj, proj, bias_a)


def _diff_kernel(q_ref, k_ref, v_ref, bias_ref, lam_ref, g_ref, o_ref, vt_ref, *, tq, n_heads, seq, lam_init):
    i = pl.program_id(1)
    half = HEAD_DIM // 2
    heads = [slice(g * HEAD_DIM, (g + 1) * HEAD_DIM) for g in range(n_heads)]

    @pl.when(i == 0)
    def _():
        r = lax.broadcasted_iota(jnp.int32, (HEAD_DIM, HEAD_DIM), 0)
        c = lax.broadcasted_iota(jnp.int32, (HEAD_DIM, HEAD_DIM), 1)
        eye = (r == c).astype(BF16)

        def transpose_block(jb, carry):
            rows = pl.ds(pl.multiple_of(jb * tq, tq), tq)
            for g, hs in enumerate(heads):
                vt_ref[g, jb, :HEAD_DIM] = _dot_nt(eye, v_ref[rows, hs]).astype(BF16)
                vt_ref[g, jb, HEAD_DIM:] = jnp.ones((ONES_ROWS, tq), BF16)
            return carry

        lax.fori_loop(0, seq // tq, transpose_block, 0)

    lane = lax.broadcasted_iota(jnp.int32, (tq, HEAD_DIM), 1)
    qqs = []
    for hs in heads:
        q = q_ref[:, hs]
        zero = jnp.zeros_like(q)
        qqs.append(jnp.concatenate([jnp.where(lane < half, q, zero), jnp.where(lane >= half, q, zero)], axis=0))
    key = lax.broadcasted_iota(jnp.int32, (tq, 2 * tq), 0)
    qry = lax.broadcasted_iota(jnp.int32, (tq, 2 * tq), 1)
    causal = key <= jnp.where(qry >= tq, qry - tq, qry)

    def step(js, carries, diagonal_last=False):
        n_chains = len(js) * n_heads
        states = [dict() for _ in range(n_chains)]

        def carry(c, name, idx):
            return carries[c][idx] if c < n_heads else states[c - n_heads][name]

        def scores(c, st):
            j, g = js[c // n_heads], c % n_heads
            rows = pl.ds(pl.multiple_of(j * tq, tq), tq)
            bias = bias_ref[g, i - j]
            s = _dot_nt(k_ref[rows, heads[g]], qqs[g]) + jnp.concatenate([bias, bias], axis=1)
            masked = diagonal_last and c // n_heads == len(js) - 1
            st["s"] = jnp.where(causal, s, NEG_INF) if masked else s

        def softmax(c, st):
            m = carry(c, "m", 0)
            s = st.pop("s")
            st["m"] = jnp.maximum(m, jnp.max(s, axis=0, keepdims=True))
            st["alpha"] = jnp.exp(m - st["m"])
            st["p"] = jnp.exp(s - st["m"]).astype(BF16)

        def values(c, st):
            vt = vt_ref[c % n_heads, js[c // n_heads]]
            st["acc"] = st.pop("alpha") * carry(c, "acc", 1) + _dot(vt, st.pop("p"))

        _staggered(n_chains, [scores, softmax, values], states)
        return tuple((st["m"], st["acc"]) for st in states[-n_heads:])

    init = tuple((jnp.full((1, 2 * tq), NEG_INF, F32), jnp.zeros((HEAD_DIM + ONES_ROWS, 2 * tq), F32))
                 for _ in heads)

    def several_blocks(cr):
        cr = lax.fori_loop(0, (i - 1) // 2, lambda t, c: step([2 * t, 2 * t + 1], c), cr)
        cr = lax.cond((i - 1) % 2 == 1, lambda c: step([i - 2], c), lambda c: c, cr)
        return step([i - 1, i], cr, diagonal_last=True)

    carries = lax.cond(i == 0, lambda cr: step([i], cr, diagonal_last=True), several_blocks, init)

    lp = lam_ref[...]
    lam = (jnp.exp(jnp.sum(lp[0:1] * lp[1:2], axis=-1, keepdims=True))
           - jnp.exp(jnp.sum(lp[2:3] * lp[3:4], axis=-1, keepdims=True)) + lam_init)
    for hs, (_, acc) in zip(heads, carries):
        o = acc[:HEAD_DIM] / acc[HEAD_DIM:HEAD_DIM + 1]
        out = o[:, :tq] - lam * o[:, tq:]
        ms = jnp.mean(out * out, axis=0, keepdims=True)
        y = out * lax.rsqrt(ms + EPS) * g_ref[...] * (1.0 - lam_init)
        o_ref[:, hs] = y.T.astype(o_ref.dtype)


def _diff_attention(proj, bias_b, lam_params, g, *, batch, seq, tq, lam_init):
    t = proj.shape[0]
    nq = seq // tq
    nh = N_HEADS_B
    width = nh * HEAD_DIM
    assert _QB % nh == 0 and _KB % nh == 0 and _VB % nh == 0
    kv = lambda off: pl.BlockSpec((seq, width), lambda b, i: (b, off // nh))
    return pl.pallas_call(
        functools.partial(_diff_kernel, tq=tq, n_heads=nh, seq=seq, lam_init=lam_init),
        grid=(batch, nq),
        in_specs=[pl.BlockSpec((tq, width), lambda b, i: (b * nq + i, _QB // nh)),
                  kv(_KB), kv(_VB),
                  pl.BlockSpec((nh, nq, tq, tq), lambda b, i: (0, 0, 0, 0)),
                  pl.BlockSpec((4, HEAD_DIM // 2), lambda b, i: (0, 0)),
                  pl.BlockSpec((HEAD_DIM, 1), lambda b, i: (0, 0))],
        out_specs=pl.BlockSpec((tq, width), lambda b, i: (b * nq + i, 0)),
        out_shape=jax.ShapeDtypeStruct((t, width), BF16),
        scratch_shapes=[pltpu.VMEM((nh, nq, HEAD_DIM + ONES_ROWS, tq), BF16)],
        compiler_params=_cparams(("parallel", "arbitrary")),
        name="diff_attn",
    )(proj, proj, proj, bias_b, lam_params, g)


def _stick_kernel(q_ref, k_ref, v_ref, o_ref, *, tq, n_heads):
    i = pl.program_id(2)
    row = lax.broadcasted_iota(jnp.int32, (tq, tq), 0)
    col = lax.broadcasted_iota(jnp.int32, (tq, tq), 1)
    strict = col < row
    suffix = (row >= col).astype(BF16)
    suffix2 = jnp.concatenate([suffix, suffix], axis=0)
    heads = [slice(g * HEAD_DIM, (g + 1) * HEAD_DIM) for g in range(n_heads)]
    qs = [q_ref[:, hs] for hs in heads]

    def blocks(js, carries, diagonal_first=False):
        n_chains = len(js) * n_heads
        states = [dict() for _ in range(n_chains)]
        rows_of = [pl.ds(pl.multiple_of(j * tq, tq), tq) for j in js]

        def masked(c):
            return diagonal_first and c < n_heads

        def carry_c(c):
            return carries[c][0] if c < n_heads else states[c - n_heads]["c"]

        def carry_acc(c):
            return carries[c][1] if c < n_heads else states[c - n_heads]["acc"]

        def scores(c, st):
            st["z"] = _dot_nt(qs[c % n_heads], k_ref[rows_of[c // n_heads], heads[c % n_heads]])

        def log_break(c, st):
            z = st["z"]
            neg_abs = lax.bitcast_convert_type(lax.bitcast_convert_type(z, jnp.uint32) | jnp.uint32(0x80000000), F32)
            w = jnp.maximum(z, 0.0) + jnp.log(1.0 + jnp.exp(neg_abs))
            if masked(c):
                w = jnp.where(strict, w, 0.0)
            hi = w.astype(BF16)
            st["hilo"] = jnp.concatenate([hi, (w - hi.astype(F32)).astype(BF16)], axis=1)

        def suffix_sums(c, st):
            st["incl"] = _dot(st.pop("hilo"), suffix2)
            st["c"] = carry_c(c) + st["incl"][:, 0:1]

        def weights(c, st):
            a = jnp.exp((st.pop("z") - carry_c(c)) - st.pop("incl"))
            if masked(c):
                a = jnp.where(strict, a, 0.0)
            st["a"] = a.astype(BF16)

        def values(c, st):
            st["acc"] = carry_acc(c) + _dot(st.pop("a"), v_ref[rows_of[c // n_heads], heads[c % n_heads]])

        _staggered(n_chains, [scores, log_break, suffix_sums, weights, values], states)
        return tuple((st["c"], st["acc"]) for st in states[-n_heads:])

    init = tuple((jnp.zeros((tq, 1), F32), jnp.zeros((tq, HEAD_DIM), F32)) for _ in heads)

    def several_blocks(cr):
        cr = blocks([i, i - 1], cr, diagonal_first=True)
        cr = lax.fori_loop(0, (i - 1) // 2, lambda t, c: blocks([i - 2 - 2 * t, i - 3 - 2 * t], c), cr)
        return lax.cond((i - 1) % 2 == 1, lambda c: blocks([0], c), lambda c: c, cr)

    carries = lax.cond(i == 0, lambda cr: blocks([i], cr, diagonal_first=True), several_blocks, init)
    for hs, (_, acc) in zip(heads, carries):
        o_ref[:, hs] = acc.astype(o_ref.dtype)


def _stick_attention(proj, *, batch, seq, tq, heads_per_step):
    t = proj.shape[0]
    nq = seq // tq
    g = heads_per_step
    width = g * HEAD_DIM
    kv = lambda off: pl.BlockSpec((seq, width), lambda b, h, i: (b, off // g + h))
    return pl.pallas_call(
        functools.partial(_stick_kernel, tq=tq, n_heads=g),
        grid=(batch, N_HEADS_C // g, nq),
        in_specs=[pl.BlockSpec((tq, width), lambda b, h, i: (b * nq + i, _QC // g + h)),
                  kv(_KC), kv(_VC)],
        out_specs=pl.BlockSpec((tq, width), lambda b, h, i: (b * nq + i, h)),
        out_shape=jax.ShapeDtypeStruct((t, N_HEADS_C * HEAD_DIM), BF16),
        compiler_params=_cparams(("parallel", "parallel", "arbitrary")),
        name="stick_attn",
    )(proj, proj, proj)


def _out_proj_kernel(x_ref, a_ref, b_ref, c_ref, w_ref, o_ref):
    ka, kb = a_ref.shape[1], b_ref.shape[1]
    acc = (_dot(a_ref[...], w_ref[:ka].astype(BF16)) + _dot(b_ref[...], w_ref[ka:ka + kb].astype(BF16))
           + _dot(c_ref[...], w_ref[ka + kb:].astype(BF16)))
    o_ref[...] = x_ref[...] + acc


def _out_proj(x, ma, mb, mc, w, layer, *, tm=512):
    t, d = x.shape
    act = lambda k: pl.BlockSpec((tm, k), lambda i: (i, 0))
    return pl.pallas_call(
        _out_proj_kernel,
        grid=(t // tm,),
        in_specs=[act(d), act(ma.shape[1]), act(mb.shape[1]), act(mc.shape[1]),
                  pl.BlockSpec((None,) + w.shape[1:], lambda i: (layer, 0, 0))],
        out_specs=act(d),
        out_shape=jax.ShapeDtypeStruct((t, d), F32),
        compiler_params=_cparams(("parallel",)),
        name="out_proj",
    )(x, ma, mb, mc, w)


def _mlp_kernel(x_ref, g_ref, w1_ref, w2_ref, gf_ref, o_ref, xn_ref, *, final_norm):
    f = pl.program_id(1)

    @pl.when(f == 0)
    def _():
        x = x_ref[...]
        ms = jnp.mean(x * x, axis=-1, keepdims=True)
        xn_ref[...] = (x * lax.rsqrt(ms + EPS) * g_ref[...]).astype(BF16)
        o_ref[...] = x

    h = jnp.maximum(_dot(xn_ref[...], w1_ref[...].astype(BF16)), 0.0)
    o_ref[...] += _dot((h * h).astype(BF16), w2_ref[...].astype(BF16))

    if final_norm:
        @pl.when(f == pl.num_programs(1) - 1)
        def _():
            y = o_ref[...]
            ms = jnp.mean(y * y, axis=-1, keepdims=True)
            o_ref[...] = y * lax.rsqrt(ms + EPS) * gf_ref[...]


def _mlp(x, g, w1, w2, layer, g_final, *, final_norm, tm=1024, tf=512):
    t, d = x.shape
    dff = w1.shape[2]
    return pl.pallas_call(
        functools.partial(_mlp_kernel, final_norm=final_norm),
        grid=(t // tm, dff // tf),
        in_specs=[pl.BlockSpec((tm, d), lambda i, f: (i, 0)),
                  pl.BlockSpec((1, d), lambda i, f: (0, 0)),
                  pl.BlockSpec((None, d, tf), lambda i, f: (layer, 0, f)),
                  pl.BlockSpec((None, tf, d), lambda i, f: (layer, f, 0)),
                  pl.BlockSpec((1, d), lambda i, f: (0, 0))],
        out_specs=pl.BlockSpec((tm, d), lambda i, f: (i, 0)),
        out_shape=jax.ShapeDtypeStruct((t, d), F32),
        scratch_shapes=[pltpu.VMEM((tm, d), BF16)],
        compiler_params=_cparams(("parallel", "arbitrary")),
        name="mlp",
    )(x, g, w1, w2, g_final)


def kernel(x, w_in, w_out, g_attn, g_mlp, w_mlp_in, w_mlp_out, rel_bias_table,
           diff_lam_q1, diff_lam_k1, diff_lam_q2, diff_lam_k2, diff_subln_g, g_final):
    batch, seq, d_model = x.shape
    depth = w_in.shape[0]
    tq_b = 256
    tq_c = 256

    i = np.arange(BLK)[:, None]
    c = np.arange(2 * BLK)[None, :]
    steps = i + BLK - c
    buckets_a = np.stack([_t5_bucket_np(steps * dil) for _, dil in DILATED_PATTERNS])
    nq_b = seq // tq_b
    dist = (np.arange(nq_b)[:, None, None] * tq_b + np.arange(tq_b)[None, None, :]
            - np.arange(tq_b)[None, :, None])
    buckets_b = _t5_bucket_np(dist)
    bias_a = _build_bias(rel_bias_table[:, :N_HEADS_A], buckets_a)
    bias_b = _build_bias(rel_bias_table[:, N_HEADS_A:], buckets_b)

    colscale = np.ones((1, w_in.shape[2]), np.float32)
    colscale[0, _QA * HEAD_DIM:_KA * HEAD_DIM] = 1.0 / math.sqrt(HEAD_DIM)
    colscale[0, _QB * HEAD_DIM:_KB * HEAD_DIM] = 1.0 / math.sqrt(HEAD_DIM // 2)
    colscale[0, _QC * HEAD_DIM:_KC * HEAD_DIM] = 1.0 / math.sqrt(HEAD_DIM)
    colscale = jnp.asarray(colscale)

    tile = N_HEADS_A * HEAD_DIM
    assert N_HEADS_C == N_HEADS_A and (3 * N_HEADS_B * HEAD_DIM) % tile == 0
    tiles_a, tiles_b = 3, 3 * N_HEADS_B * HEAD_DIM // tile
    src_tiles = (tuple(range(tiles_a)) + tuple(range(tiles_a + tiles_b, 2 * tiles_a + tiles_b))
                 + tuple(range(tiles_a, tiles_a + tiles_b)))

    xf = x.reshape(batch * seq, d_model)
    for l in range(depth):
        lam_init = 0.8 - 0.6 * math.exp(-0.3 * l)
        lam_params = jnp.stack([diff_lam_q1[l], diff_lam_k1[l], diff_lam_q2[l], diff_lam_k2[l]]).astype(F32)
        proj = _rms_proj(xf, g_attn[l][None, :], w_in, l, colscale, src_tiles)
        ma = _dilated_attention(proj, bias_a, batch=batch, seq=seq)
        mb = _diff_attention(proj, bias_b, lam_params, diff_subln_g[l][:, None],
                             batch=batch, seq=seq, tq=tq_b, lam_init=lam_init)
        mc = _stick_attention(proj, batch=batch, seq=seq, tq=tq_c, heads_per_step=6)
        xf = _out_proj(xf, ma, mb, mc, w_out, l)
        xf = _mlp(xf, g_mlp[l][None, :], w_mlp_in, w_mlp_out, l, g_final[None, :], final_norm=(l == depth - 1))
    return xf.reshape(batch, seq, d_model)
```

```python
import functools
import math

import jax
import jax.numpy as jnp
import numpy as np
from jax import lax
from jax.experimental import pallas as pl
from jax.experimental.pallas import tpu as pltpu

HEAD_DIM = 128
N_HEADS_A = 6
N_HEADS_B = 4
N_HEADS_C = 6
DILATED_PATTERNS = ((128, 1), (512, 4), (2048, 16))
BLK = 128
NUM_BUCKETS = 32
MAX_DISTANCE = 2048
EPS = 1e-6
NEG_INF = -1e30

F32 = jnp.float32
BF16 = jnp.bfloat16

V7X_VMEM_BYTES = 64 * 1024 * 1024
BF16_SUBLANE_TILE = 16
F32_SIGN_BIT = 0x80000000

ONES_ROWS = BF16_SUBLANE_TILE

_VMEM_LIMIT = V7X_VMEM_BYTES * 7 // 8
PROJ_TM = 1024
OUT_PROJ_TM = 512
MLP_TM, MLP_TF = 1024, 512
TQ_DIFF = 256
TQ_STICK = 256
STICK_HEADS_PER_STEP = 6
DILATED_LAG = 3
MERGE_ROWS = 256

_QA, _KA, _VA = 0, N_HEADS_A, 2 * N_HEADS_A
_QC = 3 * N_HEADS_A
_KC, _VC = _QC + N_HEADS_C, _QC + 2 * N_HEADS_C
_QB = _QC + 3 * N_HEADS_C
_KB, _VB = _QB + N_HEADS_B, _QB + 2 * N_HEADS_B


def _cparams(sem):
    return pltpu.CompilerParams(dimension_semantics=sem, vmem_limit_bytes=_VMEM_LIMIT)


def _dot_nt(a, b):
    return lax.dot_general(a, b, (((1,), (1,)), ((), ())), preferred_element_type=F32)


def _dot(a, b):
    return jnp.dot(a, b, preferred_element_type=F32)


def _staggered(n_chains, stages, states=None):
    states = [dict() for _ in range(n_chains)] if states is None else states
    for t in range(n_chains + len(stages) - 1):
        for k, stage in enumerate(stages):
            c = t - k
            if stage is not None and 0 <= c < n_chains:
                stage(c, states[c])
    return states


def _t5_bucket_np(dist):
    n = np.maximum(dist, 0)
    max_exact = NUM_BUCKETS // 2
    nf = np.maximum(n, max_exact).astype(np.float32)
    large = max_exact + (np.log(nf / np.float32(max_exact)) / np.float32(math.log(MAX_DISTANCE / max_exact))
                         * np.float32(NUM_BUCKETS - max_exact)).astype(np.int32)
    large = np.minimum(large, NUM_BUCKETS - 1)
    return np.where(n < max_exact, n, large).astype(np.int32)


def _bias_kernel(tab_ref, bkt_ref, o_ref, *, n_heads, tile_buckets):
    for t, present in enumerate(tile_buckets):
        b = bkt_ref[t]
        for h in range(n_heads):
            acc = jnp.full(b.shape, tab_ref[present[0], h], F32)
            for k in present[1:]:
                acc = jnp.where(b == k, tab_ref[k, h], acc)
            o_ref[h, t] = acc


def _build_bias(table, buckets):
    n_heads = table.shape[1]
    n, r, c = buckets.shape
    tile_buckets = tuple(tuple(int(k) for k in np.unique(buckets[t])) for t in range(n))
    return pl.pallas_call(
        functools.partial(_bias_kernel, n_heads=n_heads, tile_buckets=tile_buckets),
        in_specs=[pl.BlockSpec(memory_space=pltpu.SMEM),
                  pl.BlockSpec(memory_space=pltpu.VMEM)],
        out_specs=pl.BlockSpec(memory_space=pltpu.VMEM),
        out_shape=jax.ShapeDtypeStruct((n_heads, n, r, c), F32),
        compiler_params=pltpu.CompilerParams(vmem_limit_bytes=_VMEM_LIMIT),
        name="bias_table",
    )(table, jnp.asarray(buckets))


def _rms_proj_kernel(x_ref, g_ref, w_ref, cs_ref, o_ref, xn_ref):
    def project(xn):
        acc = _dot(xn, w_ref[...].astype(BF16))
        o_ref[...] = (acc * cs_ref[...]).astype(o_ref.dtype)

    @pl.when(pl.program_id(1) == 0)
    def _():
        x = x_ref[...]
        ms = jnp.mean(x * x, axis=-1, keepdims=True)
        xn = (x * lax.rsqrt(ms + EPS) * g_ref[...]).astype(BF16)
        xn_ref[...] = xn
        project(xn)

    @pl.when(pl.program_id(1) > 0)
    def _():
        project(xn_ref[...])


def _rms_proj(x, g, w, layer, colscale, src_tiles, *, tm=PROJ_TM):
    t, d = x.shape
    n = w.shape[2]
    tn = n // len(src_tiles)

    def src_tile(j):
        idx = src_tiles[-1]
        for k in reversed(range(len(src_tiles) - 1)):
            idx = jnp.where(j == k, src_tiles[k], idx)
        return idx

    return pl.pallas_call(
        _rms_proj_kernel,
        grid=(t // tm, n // tn),
        in_specs=[pl.BlockSpec((tm, d), lambda i, j: (i, 0)),
                  pl.BlockSpec((1, d), lambda i, j: (0, 0)),
                  pl.BlockSpec((None, d, tn), lambda i, j: (layer, 0, src_tile(j))),
                  pl.BlockSpec((1, tn), lambda i, j: (0, j))],
        out_specs=pl.BlockSpec((tm, tn), lambda i, j: (i, j)),
        out_shape=jax.ShapeDtypeStruct((t, n), BF16),
        scratch_shapes=[pltpu.VMEM((tm, d), BF16)],
        compiler_params=_cparams(("parallel", "arbitrary")),
        name="rms_proj",
    )(x, g, w, colscale)


def _dilated_kernel(q_ref, k_ref, v_ref, bias_ref, o_ref, qf, kf, vf, acc_s, m_s, l_s, *, seq, lag):
    qf[...] = q_ref[...].astype(F32)
    kf[...] = k_ref[...].astype(F32)
    vf[...] = v_ref[...].astype(F32)

    row = lax.broadcasted_iota(jnp.int32, (BLK, 2 * BLK), 0)
    col = lax.broadcasted_iota(jnp.int32, (BLK, 2 * BLK), 1)
    band_mask = (col >= row) & (col <= row + BLK)
    row0 = lax.broadcasted_iota(jnp.int32, (BLK, BLK), 0)
    col0 = lax.broadcasted_iota(jnp.int32, (BLK, BLK), 1)
    diag_mask = col0 <= row0

    def rows_of(start, size, dil):
        return pl.ds(start, size, stride=dil) if dil > 1 else pl.ds(start, size)

    def ld(ref, start, size, dil):
        return ref[rows_of(start, size, dil), :].astype(BF16)

    chains = []
    for p_idx, (window, dil) in enumerate(DILATED_PATTERNS):
        assert window // dil == BLK
        for r in range(dil):
            for n in range(seq // dil // BLK):
                chains.append((p_idx, dil, r + dil * BLK * n, None if n == 0 else r + dil * BLK * (n - 1)))

    def scores(c, st):
        p_idx, dil, q0, k0 = chains[c]
        qb = ld(qf, q0, BLK, dil)
        if k0 is None:
            s = _dot_nt(qb, ld(kf, q0, BLK, dil)) + bias_ref[0, p_idx, :, BLK:]
            st["s"] = jnp.where(diag_mask, s, NEG_INF)
        else:
            s = _dot_nt(qb, ld(kf, k0, 2 * BLK, dil)) + bias_ref[0, p_idx]
            st["s"] = jnp.where(band_mask, s, NEG_INF)

    def softmax(c, st):
        s = st.pop("s")
        st["m"] = jnp.max(s, axis=-1, keepdims=True)
        p = jnp.exp(s - st["m"])
        st["l"] = jnp.sum(p, axis=-1, keepdims=True)
        st["p"] = p.astype(BF16)

    def values(c, st):
        p_idx, dil, q0, k0 = chains[c]
        vb = ld(vf, q0, BLK, dil) if k0 is None else ld(vf, k0, 2 * BLK, dil)
        rows = rows_of(q0, BLK, dil)
        acc_s[p_idx, rows, :] = _dot(st.pop("p"), vb)
        m_s[p_idx, rows, :] = jnp.broadcast_to(st.pop("m"), (BLK, HEAD_DIM))
        l_s[p_idx, rows, :] = jnp.broadcast_to(st.pop("l"), (BLK, HEAD_DIM))

    _staggered(len(chains), [scores] + [None] * (lag - 1) + [softmax] + [None] * (lag - 1) + [values])

    chunk = MERGE_ROWS

    def merge(ci, carry):
        rows = pl.ds(pl.multiple_of(ci * chunk, chunk), chunk)
        m0, m1, m2 = m_s[0, rows, :], m_s[1, rows, :], m_s[2, rows, :]
        mm = jnp.maximum(jnp.maximum(m0, m1), m2)
        w0, w1, w2 = jnp.exp(m0 - mm), jnp.exp(m1 - mm), jnp.exp(m2 - mm)
        num = w0 * acc_s[0, rows, :] + w1 * acc_s[1, rows, :] + w2 * acc_s[2, rows, :]
        den = w0 * l_s[0, rows, :] + w1 * l_s[1, rows, :] + w2 * l_s[2, rows, :]
        o_ref[rows, :] = (num / den).astype(o_ref.dtype)
        return carry

    lax.fori_loop(0, seq // chunk, merge, 0)


def _dilated_attention(proj, bias_a, *, batch, seq):
    t = proj.shape[0]
    n_pat = len(DILATED_PATTERNS)
    blk = lambda off: pl.BlockSpec((seq, HEAD_DIM), lambda b, h: (b, off + h))
    return pl.pallas_call(
        functools.partial(_dilated_kernel, seq=seq, lag=DILATED_LAG),
        grid=(batch, N_HEADS_A),
        in_specs=[blk(_QA), blk(_KA), blk(_VA),
                  pl.BlockSpec((1, n_pat, BLK, 2 * BLK), lambda b, h: (h, 0, 0, 0))],
        out_specs=pl.BlockSpec((seq, HEAD_DIM), lambda b, h: (b, h)),
        out_shape=jax.ShapeDtypeStruct((t, N_HEADS_A * HEAD_DIM), BF16),
        scratch_shapes=[pltpu.VMEM((seq, HEAD_DIM), F32)] * 3
                       + [pltpu.VMEM((n_pat, seq, HEAD_DIM), F32)] * 3,
        compiler_params=_cparams(("parallel", "parallel")),
        name="dilated_attn",
    )(proj, proj, proj, bias_a)


def _diff_kernel(q_ref, k_ref, v_ref, bias_ref, lam_ref, g_ref, o_ref, vt_ref, *, tq, n_heads, seq, lam_init):
    i = pl.program_id(1)
    half = HEAD_DIM // 2
    heads = [slice(g * HEAD_DIM, (g + 1) * HEAD_DIM) for g in range(n_heads)]

    @pl.when(i == 0)
    def _():
        r = lax.broadcasted_iota(jnp.int32, (HEAD_DIM, HEAD_DIM), 0)
        c = lax.broadcasted_iota(jnp.int32, (HEAD_DIM, HEAD_DIM), 1)
        eye = (r == c).astype(BF16)

        def transpose_block(jb, carry):
            rows = pl.ds(pl.multiple_of(jb * tq, tq), tq)
            for g, hs in enumerate(heads):
                vt_ref[g, jb, :HEAD_DIM] = _dot_nt(eye, v_ref[rows, hs]).astype(BF16)
                vt_ref[g, jb, HEAD_DIM:] = jnp.ones((ONES_ROWS, tq), BF16)
            return carry

        lax.fori_loop(0, seq // tq, transpose_block, 0)

    lane = lax.broadcasted_iota(jnp.int32, (tq, HEAD_DIM), 1)
    qqs = []
    for hs in heads:
        q = q_ref[:, hs]
        zero = jnp.zeros_like(q)
        qqs.append(jnp.concatenate([jnp.where(lane < half, q, zero), jnp.where(lane >= half, q, zero)], axis=0))
    key = lax.broadcasted_iota(jnp.int32, (tq, 2 * tq), 0)
    qry = lax.broadcasted_iota(jnp.int32, (tq, 2 * tq), 1)
    causal = key <= jnp.where(qry >= tq, qry - tq, qry)

    def step(js, carries, diagonal_last=False):
        n_chains = len(js) * n_heads
        states = [dict() for _ in range(n_chains)]

        def carry(c, name, idx):
            return carries[c][idx] if c < n_heads else states[c - n_heads][name]

        def scores(c, st):
            j, g = js[c // n_heads], c % n_heads
            rows = pl.ds(pl.multiple_of(j * tq, tq), tq)
            bias = bias_ref[g, i - j]
            s = _dot_nt(k_ref[rows, heads[g]], qqs[g]) + jnp.concatenate([bias, bias], axis=1)
            masked = diagonal_last and c // n_heads == len(js) - 1
            st["s"] = jnp.where(causal, s, NEG_INF) if masked else s

        def softmax(c, st):
            m = carry(c, "m", 0)
            s = st.pop("s")
            st["m"] = jnp.maximum(m, jnp.max(s, axis=0, keepdims=True))
            st["alpha"] = jnp.exp(m - st["m"])
            st["p"] = jnp.exp(s - st["m"]).astype(BF16)

        def values(c, st):
            vt = vt_ref[c % n_heads, js[c // n_heads]]
            st["acc"] = st.pop("alpha") * carry(c, "acc", 1) + _dot(vt, st.pop("p"))

        _staggered(n_chains, [scores, softmax, values], states)
        return tuple((st["m"], st["acc"]) for st in states[-n_heads:])

    init = tuple((jnp.full((1, 2 * tq), NEG_INF, F32), jnp.zeros((HEAD_DIM + ONES_ROWS, 2 * tq), F32))
                 for _ in heads)
    carries = lax.fori_loop(0, i // 2, lambda t, cr: step([2 * t, 2 * t + 1], cr), init)
    carries = lax.cond(i % 2 == 1, lambda cr: step([i - 1], cr), lambda cr: cr, carries)
    carries = step([i], carries, diagonal_last=True)

    lp = lam_ref[...]
    lam = (jnp.exp(jnp.sum(lp[0:1] * lp[1:2], axis=-1, keepdims=True))
           - jnp.exp(jnp.sum(lp[2:3] * lp[3:4], axis=-1, keepdims=True)) + lam_init)
    for hs, (_, acc) in zip(heads, carries):
        o = acc[:HEAD_DIM] / acc[HEAD_DIM:HEAD_DIM + 1]
        out = o[:, :tq] - lam * o[:, tq:]
        ms = jnp.mean(out * out, axis=0, keepdims=True)
        y = out * lax.rsqrt(ms + EPS) * g_ref[...] * (1.0 - lam_init)
        o_ref[:, hs] = y.T.astype(o_ref.dtype)


def _diff_attention(proj, bias_b, lam_params, g, *, batch, seq, tq, lam_init):
    t = proj.shape[0]
    nq = seq // tq
    nh = N_HEADS_B
    width = nh * HEAD_DIM
    assert _QB % nh == 0 and _KB % nh == 0 and _VB % nh == 0
    kv = lambda off: pl.BlockSpec((seq, width), lambda b, i: (b, off // nh))
    return pl.pallas_call(
        functools.partial(_diff_kernel, tq=tq, n_heads=nh, seq=seq, lam_init=lam_init),
        grid=(batch, nq),
        in_specs=[pl.BlockSpec((tq, width), lambda b, i: (b * nq + i, _QB // nh)),
                  kv(_KB), kv(_VB),
                  pl.BlockSpec((nh, nq, tq, tq), lambda b, i: (0, 0, 0, 0)),
                  pl.BlockSpec((4, HEAD_DIM // 2), lambda b, i: (0, 0)),
                  pl.BlockSpec((HEAD_DIM, 1), lambda b, i: (0, 0))],
        out_specs=pl.BlockSpec((tq, width), lambda b, i: (b * nq + i, 0)),
        out_shape=jax.ShapeDtypeStruct((t, width), BF16),
        scratch_shapes=[pltpu.VMEM((nh, nq, HEAD_DIM + ONES_ROWS, tq), BF16)],
        compiler_params=_cparams(("parallel", "arbitrary")),
        name="diff_attn",
    )(proj, proj, proj, bias_b, lam_params, g)


def _stick_kernel(q_ref, k_ref, v_ref, o_ref, *, tq, n_heads):
    i = pl.program_id(2)
    row = lax.broadcasted_iota(jnp.int32, (tq, tq), 0)
    col = lax.broadcasted_iota(jnp.int32, (tq, tq), 1)
    strict = col < row
    suffix = (row >= col).astype(BF16)
    suffix2 = jnp.concatenate([suffix, suffix], axis=0)
    heads = [slice(g * HEAD_DIM, (g + 1) * HEAD_DIM) for g in range(n_heads)]
    qs = [q_ref[:, hs] for hs in heads]

    def blocks(js, carries, diagonal_first=False):
        n_chains = len(js) * n_heads
        states = [dict() for _ in range(n_chains)]
        rows_of = [pl.ds(pl.multiple_of(j * tq, tq), tq) for j in js]

        def masked(c):
            return diagonal_first and c < n_heads

        def carry_c(c):
            return carries[c][0] if c < n_heads else states[c - n_heads]["c"]

        def carry_acc(c):
            return carries[c][1] if c < n_heads else states[c - n_heads]["acc"]

        def scores(c, st):
            st["z"] = _dot_nt(qs[c % n_heads], k_ref[rows_of[c // n_heads], heads[c % n_heads]])

        def log_break(c, st):
            z = st["z"]
            neg_abs = lax.bitcast_convert_type(
                lax.bitcast_convert_type(z, jnp.uint32) | jnp.uint32(F32_SIGN_BIT), F32)
            w = jnp.maximum(z, 0.0) + jnp.log(1.0 + jnp.exp(neg_abs))
            if masked(c):
                w = jnp.where(strict, w, 0.0)
            hi = w.astype(BF16)
            st["hilo"] = jnp.concatenate([hi, (w - hi.astype(F32)).astype(BF16)], axis=1)

        def suffix_sums(c, st):
            st["incl"] = _dot(st.pop("hilo"), suffix2)
            st["c"] = carry_c(c) + st["incl"][:, 0:1]

        def weights(c, st):
            a = jnp.exp((st.pop("z") - carry_c(c)) - st.pop("incl"))
            if masked(c):
                a = jnp.where(strict, a, 0.0)
            st["a"] = a.astype(BF16)

        def values(c, st):
            st["acc"] = carry_acc(c) + _dot(st.pop("a"), v_ref[rows_of[c // n_heads], heads[c % n_heads]])

        _staggered(n_chains, [scores, log_break, suffix_sums, weights, values], states)
        return tuple((st["c"], st["acc"]) for st in states[-n_heads:])

    init = tuple((jnp.zeros((tq, 1), F32), jnp.zeros((tq, HEAD_DIM), F32)) for _ in heads)

    def several_blocks(cr):
        cr = blocks([i, i - 1], cr, diagonal_first=True)
        cr = lax.fori_loop(0, (i - 1) // 2, lambda t, c: blocks([i - 2 - 2 * t, i - 3 - 2 * t], c), cr)
        return lax.cond((i - 1) % 2 == 1, lambda c: blocks([0], c), lambda c: c, cr)

    carries = lax.cond(i == 0, lambda cr: blocks([i], cr, diagonal_first=True), several_blocks, init)
    for hs, (_, acc) in zip(heads, carries):
        o_ref[:, hs] = acc.astype(o_ref.dtype)


def _stick_attention(proj, *, batch, seq, tq, heads_per_step):
    t = proj.shape[0]
    nq = seq // tq
    g = heads_per_step
    width = g * HEAD_DIM
    kv = lambda off: pl.BlockSpec((seq, width), lambda b, h, i: (b, off // g + h))
    return pl.pallas_call(
        functools.partial(_stick_kernel, tq=tq, n_heads=g),
        grid=(batch, N_HEADS_C // g, nq),
        in_specs=[pl.BlockSpec((tq, width), lambda b, h, i: (b * nq + i, _QC // g + h)),
                  kv(_KC), kv(_VC)],
        out_specs=pl.BlockSpec((tq, width), lambda b, h, i: (b * nq + i, h)),
        out_shape=jax.ShapeDtypeStruct((t, N_HEADS_C * HEAD_DIM), BF16),
        compiler_params=_cparams(("parallel", "parallel", "arbitrary")),
        name="stick_attn",
    )(proj, proj, proj)


def _out_proj_kernel(x_ref, a_ref, b_ref, c_ref, w_ref, o_ref):
    ka, kb = a_ref.shape[1], b_ref.shape[1]
    acc = (_dot(a_ref[...], w_ref[:ka].astype(BF16)) + _dot(b_ref[...], w_ref[ka:ka + kb].astype(BF16))
           + _dot(c_ref[...], w_ref[ka + kb:].astype(BF16)))
    o_ref[...] = x_ref[...] + acc


def _out_proj(x, ma, mb, mc, w, layer, *, tm=OUT_PROJ_TM):
    t, d = x.shape
    act = lambda k: pl.BlockSpec((tm, k), lambda i: (i, 0))
    return pl.pallas_call(
        _out_proj_kernel,
        grid=(t // tm,),
        in_specs=[act(d), act(ma.shape[1]), act(mb.shape[1]), act(mc.shape[1]),
                  pl.BlockSpec((None,) + w.shape[1:], lambda i: (layer, 0, 0))],
        out_specs=act(d),
        out_shape=jax.ShapeDtypeStruct((t, d), F32),
        compiler_params=_cparams(("parallel",)),
        name="out_proj",
    )(x, ma, mb, mc, w)


def _mlp_kernel(x_ref, g_ref, w1_ref, w2_ref, gf_ref, o_ref, xn_ref, *, final_norm):
    f = pl.program_id(1)

    def hidden_tile_update(xn):
        h = jnp.maximum(_dot(xn, w1_ref[...].astype(BF16)), 0.0)
        return _dot((h * h).astype(BF16), w2_ref[...].astype(BF16))

    @pl.when(f == 0)
    def _():
        x = x_ref[...]
        ms = jnp.mean(x * x, axis=-1, keepdims=True)
        xn = (x * lax.rsqrt(ms + EPS) * g_ref[...]).astype(BF16)
        xn_ref[...] = xn
        o_ref[...] = x + hidden_tile_update(xn)

    @pl.when(f > 0)
    def _():
        o_ref[...] += hidden_tile_update(xn_ref[...])

    if final_norm:
        @pl.when(f == pl.num_programs(1) - 1)
        def _():
            y = o_ref[...]
            ms = jnp.mean(y * y, axis=-1, keepdims=True)
            o_ref[...] = y * lax.rsqrt(ms + EPS) * gf_ref[...]


def _mlp(x, g, w1, w2, layer, g_final, *, final_norm, tm=MLP_TM, tf=MLP_TF):
    t, d = x.shape
    dff = w1.shape[2]
    return pl.pallas_call(
        functools.partial(_mlp_kernel, final_norm=final_norm),
        grid=(t // tm, dff // tf),
        in_specs=[pl.BlockSpec((tm, d), lambda i, f: (i, 0)),
                  pl.BlockSpec((1, d), lambda i, f: (0, 0)),
                  pl.BlockSpec((None, d, tf), lambda i, f: (layer, 0, f)),
                  pl.BlockSpec((None, tf, d), lambda i, f: (layer, f, 0)),
                  pl.BlockSpec((1, d), lambda i, f: (0, 0))],
        out_specs=pl.BlockSpec((tm, d), lambda i, f: (i, 0)),
        out_shape=jax.ShapeDtypeStruct((t, d), F32),
        scratch_shapes=[pltpu.VMEM((tm, d), BF16)],
        compiler_params=_cparams(("parallel", "arbitrary")),
        name="mlp",
    )(x, g, w1, w2, g_final)


def kernel(x, w_in, w_out, g_attn, g_mlp, w_mlp_in, w_mlp_out, rel_bias_table,
           diff_lam_q1, diff_lam_k1, diff_lam_q2, diff_lam_k2, diff_subln_g, g_final):
    batch, seq, d_model = x.shape
    depth = w_in.shape[0]

    i = np.arange(BLK)[:, None]
    c = np.arange(2 * BLK)[None, :]
    steps = i + BLK - c
    buckets_a = np.stack([_t5_bucket_np(steps * dil) for _, dil in DILATED_PATTERNS])
    dist = (np.arange(seq // TQ_DIFF)[:, None, None] * TQ_DIFF + np.arange(TQ_DIFF)[None, None, :]
            - np.arange(TQ_DIFF)[None, :, None])
    buckets_b = _t5_bucket_np(dist)
    bias_a = _build_bias(rel_bias_table[:, :N_HEADS_A], buckets_a)
    bias_b = _build_bias(rel_bias_table[:, N_HEADS_A:], buckets_b)

    colscale = np.ones((1, w_in.shape[2]), np.float32)
    colscale[0, _QA * HEAD_DIM:_KA * HEAD_DIM] = 1.0 / math.sqrt(HEAD_DIM)
    colscale[0, _QB * HEAD_DIM:_KB * HEAD_DIM] = 1.0 / math.sqrt(HEAD_DIM // 2)
    colscale[0, _QC * HEAD_DIM:_KC * HEAD_DIM] = 1.0 / math.sqrt(HEAD_DIM)
    colscale = jnp.asarray(colscale)

    tile = N_HEADS_A * HEAD_DIM
    assert N_HEADS_C == N_HEADS_A and (3 * N_HEADS_B * HEAD_DIM) % tile == 0
    tiles_a, tiles_b = 3, 3 * N_HEADS_B * HEAD_DIM // tile
    src_tiles = (tuple(range(tiles_a)) + tuple(range(tiles_a + tiles_b, 2 * tiles_a + tiles_b))
                 + tuple(range(tiles_a, tiles_a + tiles_b)))

    xf = x.reshape(batch * seq, d_model)
    for l in range(depth):
        lam_init = 0.8 - 0.6 * math.exp(-0.3 * l)
        lam_params = jnp.stack([diff_lam_q1[l], diff_lam_k1[l], diff_lam_q2[l], diff_lam_k2[l]]).astype(F32)
        proj = _rms_proj(xf, g_attn[l][None, :], w_in, l, colscale, src_tiles)
        ma = _dilated_attention(proj, bias_a, batch=batch, seq=seq)
        mb = _diff_attention(proj, bias_b, lam_params, diff_subln_g[l][:, None],
                             batch=batch, seq=seq, tq=TQ_DIFF, lam_init=lam_init)
        mc = _stick_attention(proj, batch=batch, seq=seq, tq=TQ_STICK, heads_per_step=STICK_HEADS_PER_STEP)
        xf = _out_proj(xf, ma, mb, mc, w_out, l)
        xf = _mlp(xf, g_mlp[l][None, :], w_mlp_in, w_mlp_out, l, g_final[None, :], final_norm=(l == depth - 1))
    return xf.reshape(batch, seq, d_model)
```

```python
import functools
import math

import jax
import jax.numpy as jnp
import numpy as np
from jax import lax
from jax.experimental import pallas as pl
from jax.experimental.pallas import tpu as pltpu

HEAD_DIM = 128
N_HEADS_A = 6
N_HEADS_B = 4
N_HEADS_C = 6
DILATED_PATTERNS = ((128, 1), (512, 4), (2048, 16))
BLK = 128
NUM_BUCKETS = 32
MAX_DISTANCE = 2048
EPS = 1e-6
NEG_INF = -1e30

F32 = jnp.float32
BF16 = jnp.bfloat16

V7X_VMEM_BYTES = 64 * 1024 * 1024
BF16_SUBLANE_TILE = 16
F32_SIGN_BIT = 0x80000000

ONES_ROWS = BF16_SUBLANE_TILE

_VMEM_LIMIT = V7X_VMEM_BYTES * 7 // 8
PROJ_TM = 1024
OUT_PROJ_TM = 512
MLP_TM, MLP_TF = 1024, 512
TQ_DIFF = 256
TQ_STICK = 256
STICK_HEADS_PER_STEP = 6
DILATED_LAG = 3
MERGE_ROWS = 256

_QA, _KA, _VA = 0, N_HEADS_A, 2 * N_HEADS_A
_QC = 3 * N_HEADS_A
_KC, _VC = _QC + N_HEADS_C, _QC + 2 * N_HEADS_C
_QB = _QC + 3 * N_HEADS_C
_KB, _VB = _QB + N_HEADS_B, _QB + 2 * N_HEADS_B


def _cparams(sem):
    return pltpu.CompilerParams(dimension_semantics=sem, vmem_limit_bytes=_VMEM_LIMIT)


def _dot_nt(a, b):
    return lax.dot_general(a, b, (((1,), (1,)), ((), ())), preferred_element_type=F32)


def _dot(a, b):
    return jnp.dot(a, b, preferred_element_type=F32)


def _staggered(n_chains, stages, states=None):
    states = [dict() for _ in range(n_chains)] if states is None else states
    for t in range(n_chains + len(stages) - 1):
        for k, stage in enumerate(stages):
            c = t - k
            if stage is not None and 0 <= c < n_chains:
                stage(c, states[c])
    return states


def _t5_bucket_np(dist):
    n = np.maximum(dist, 0)
    max_exact = NUM_BUCKETS // 2
    nf = np.maximum(n, max_exact).astype(np.float32)
    large = max_exact + (np.log(nf / np.float32(max_exact)) / np.float32(math.log(MAX_DISTANCE / max_exact))
                         * np.float32(NUM_BUCKETS - max_exact)).astype(np.int32)
    large = np.minimum(large, NUM_BUCKETS - 1)
    return np.where(n < max_exact, n, large).astype(np.int32)


def _bias_kernel(tab_ref, bkt_ref, o_ref, *, n_heads, tile_buckets):
    for t, present in enumerate(tile_buckets):
        b = bkt_ref[t]
        for h in range(n_heads):
            acc = jnp.full(b.shape, tab_ref[present[0], h], F32)
            for k in present[1:]:
                acc = jnp.where(b == k, tab_ref[k, h], acc)
            o_ref[h, t] = acc


def _build_bias(table, buckets):
    n_heads = table.shape[1]
    n, r, c = buckets.shape
    tile_buckets = tuple(tuple(int(k) for k in np.unique(buckets[t])) for t in range(n))
    return pl.pallas_call(
        functools.partial(_bias_kernel, n_heads=n_heads, tile_buckets=tile_buckets),
        in_specs=[pl.BlockSpec(memory_space=pltpu.SMEM),
                  pl.BlockSpec(memory_space=pltpu.VMEM)],
        out_specs=pl.BlockSpec(memory_space=pltpu.VMEM),
        out_shape=jax.ShapeDtypeStruct((n_heads, n, r, c), F32),
        compiler_params=pltpu.CompilerParams(vmem_limit_bytes=_VMEM_LIMIT),
        name="bias_table",
    )(table, jnp.asarray(buckets))


def _rms_proj_kernel(x_ref, g_ref, w_ref, cs_ref, o_ref, xn_ref):
    def project(xn):
        acc = _dot(xn, w_ref[...].astype(BF16))
        o_ref[...] = (acc * cs_ref[...]).astype(o_ref.dtype)

    @pl.when(pl.program_id(1) == 0)
    def _():
        x = x_ref[...]
        ms = jnp.mean(x * x, axis=-1, keepdims=True)
        xn = (x * lax.rsqrt(ms + EPS) * g_ref[...]).astype(BF16)
        xn_ref[...] = xn
        project(xn)

    @pl.when(pl.program_id(1) > 0)
    def _():
        project(xn_ref[...])


def _rms_proj(x, g, w, layer, colscale, src_tiles, *, tm=PROJ_TM):
    t, d = x.shape
    n = w.shape[2]
    tn = n // len(src_tiles)

    def src_tile(j):
        idx = src_tiles[-1]
        for k in reversed(range(len(src_tiles) - 1)):
            idx = jnp.where(j == k, src_tiles[k], idx)
        return idx

    return pl.pallas_call(
        _rms_proj_kernel,
        grid=(t // tm, n // tn),
        in_specs=[pl.BlockSpec((tm, d), lambda i, j: (i, 0)),
                  pl.BlockSpec((1, d), lambda i, j: (0, 0)),
                  pl.BlockSpec((None, d, tn), lambda i, j: (layer, 0, src_tile(j))),
                  pl.BlockSpec((1, tn), lambda i, j: (0, j))],
        out_specs=pl.BlockSpec((tm, tn), lambda i, j: (i, j)),
        out_shape=jax.ShapeDtypeStruct((t, n), BF16),
        scratch_shapes=[pltpu.VMEM((tm, d), BF16)],
        compiler_params=_cparams(("parallel", "arbitrary")),
        name="rms_proj",
    )(x, g, w, colscale)


def _dilated_kernel(q_ref, k_ref, v_ref, bias_ref, o_ref, qf, kf, vf, acc_s, m_s, l_s, *, seq, lag):
    qf[...] = q_ref[...].astype(F32)
    kf[...] = k_ref[...].astype(F32)
    vf[...] = v_ref[...].astype(F32)

    row = lax.broadcasted_iota(jnp.int32, (BLK, 2 * BLK), 0)
    col = lax.broadcasted_iota(jnp.int32, (BLK, 2 * BLK), 1)
    band_mask = (col >= row) & (col <= row + BLK)
    row0 = lax.broadcasted_iota(jnp.int32, (BLK, BLK), 0)
    col0 = lax.broadcasted_iota(jnp.int32, (BLK, BLK), 1)
    diag_mask = col0 <= row0

    def rows_of(start, size, dil):
        return pl.ds(start, size, stride=dil) if dil > 1 else pl.ds(start, size)

    def ld(ref, start, size, dil):
        return ref[rows_of(start, size, dil), :].astype(BF16)

    chains = []
    for p_idx, (window, dil) in enumerate(DILATED_PATTERNS):
        assert window // dil == BLK
        for r in range(dil):
            for n in range(seq // dil // BLK):
                chains.append((p_idx, dil, r + dil * BLK * n, None if n == 0 else r + dil * BLK * (n - 1)))

    def scores(c, st):
        p_idx, dil, q0, k0 = chains[c]
        qb = ld(qf, q0, BLK, dil)
        if k0 is None:
            s = _dot_nt(qb, ld(kf, q0, BLK, dil)) + bias_ref[0, p_idx, :, BLK:]
            st["s"] = jnp.where(diag_mask, s, NEG_INF)
        else:
            s = _dot_nt(qb, ld(kf, k0, 2 * BLK, dil)) + bias_ref[0, p_idx]
            st["s"] = jnp.where(band_mask, s, NEG_INF)

    def softmax(c, st):
        s = st.pop("s")
        st["m"] = jnp.max(s, axis=-1, keepdims=True)
        p = jnp.exp(s - st["m"])
        st["l"] = jnp.sum(p, axis=-1, keepdims=True)
        st["p"] = p.astype(BF16)

    def values(c, st):
        p_idx, dil, q0, k0 = chains[c]
        vb = ld(vf, q0, BLK, dil) if k0 is None else ld(vf, k0, 2 * BLK, dil)
        rows = rows_of(q0, BLK, dil)
        acc_s[p_idx, rows, :] = _dot(st.pop("p"), vb)
        m_s[p_idx, rows, :] = jnp.broadcast_to(st.pop("m"), (BLK, HEAD_DIM))
        l_s[p_idx, rows, :] = jnp.broadcast_to(st.pop("l"), (BLK, HEAD_DIM))

    _staggered(len(chains), [scores] + [None] * (lag - 1) + [softmax] + [None] * (lag - 1) + [values])

    chunk = MERGE_ROWS

    def merge(ci, carry):
        rows = pl.ds(pl.multiple_of(ci * chunk, chunk), chunk)
        m0, m1, m2 = m_s[0, rows, :], m_s[1, rows, :], m_s[2, rows, :]
        mm = jnp.maximum(jnp.maximum(m0, m1), m2)
        w0, w1, w2 = jnp.exp(m0 - mm), jnp.exp(m1 - mm), jnp.exp(m2 - mm)
        num = w0 * acc_s[0, rows, :] + w1 * acc_s[1, rows, :] + w2 * acc_s[2, rows, :]
        den = w0 * l_s[0, rows, :] + w1 * l_s[1, rows, :] + w2 * l_s[2, rows, :]
        o_ref[rows, :] = (num / den).astype(o_ref.dtype)
        return carry

    lax.fori_loop(0, seq // chunk, merge, 0)


def _dilated_attention(proj, bias_a, *, batch, seq):
    t = proj.shape[0]
    n_pat = len(DILATED_PATTERNS)
    blk = lambda off: pl.BlockSpec((seq, HEAD_DIM), lambda b, h: (b, off + h))
    return pl.pallas_call(
        functools.partial(_dilated_kernel, seq=seq, lag=DILATED_LAG),
        grid=(batch, N_HEADS_A),
        in_specs=[blk(_QA), blk(_KA), blk(_VA),
                  pl.BlockSpec((1, n_pat, BLK, 2 * BLK), lambda b, h: (h, 0, 0, 0))],
        out_specs=pl.BlockSpec((seq, HEAD_DIM), lambda b, h: (b, h)),
        out_shape=jax.ShapeDtypeStruct((t, N_HEADS_A * HEAD_DIM), BF16),
        scratch_shapes=[pltpu.VMEM((seq, HEAD_DIM), F32)] * 3
                       + [pltpu.VMEM((n_pat, seq, HEAD_DIM), F32)] * 3,
        compiler_params=_cparams(("parallel", "parallel")),
        name="dilated_attn",
    )(proj, proj, proj, bias_a)


def _diff_kernel(q_ref, k_ref, v_ref, bias_ref, lam_ref, g_ref, o_ref, vt_ref, m_ref, acc_ref, *, tq, n_heads, seq,
                 lam_init):
    i = pl.program_id(1)
    half = HEAD_DIM // 2
    heads = [slice(g * HEAD_DIM, (g + 1) * HEAD_DIM) for g in range(n_heads)]

    @pl.when(i == 0)
    def _():
        r = lax.broadcasted_iota(jnp.int32, (HEAD_DIM, HEAD_DIM), 0)
        c = lax.broadcasted_iota(jnp.int32, (HEAD_DIM, HEAD_DIM), 1)
        eye = (r == c).astype(BF16)

        def transpose_block(jb, carry):
            rows = pl.ds(pl.multiple_of(jb * tq, tq), tq)
            for g, hs in enumerate(heads):
                vt_ref[g, jb, :HEAD_DIM] = _dot_nt(eye, v_ref[rows, hs]).astype(BF16)
                vt_ref[g, jb, HEAD_DIM:] = jnp.ones((ONES_ROWS, tq), BF16)
            return carry

        lax.fori_loop(0, seq // tq, transpose_block, 0)

    lane = lax.broadcasted_iota(jnp.int32, (tq, HEAD_DIM), 1)
    qqs = []
    for hs in heads:
        q = q_ref[:, hs]
        zero = jnp.zeros_like(q)
        qqs.append(jnp.concatenate([jnp.where(lane < half, q, zero), jnp.where(lane >= half, q, zero)], axis=0))
    key = lax.broadcasted_iota(jnp.int32, (tq, 2 * tq), 0)
    qry = lax.broadcasted_iota(jnp.int32, (tq, 2 * tq), 1)
    causal = key <= jnp.where(qry >= tq, qry - tq, qry)

    def step(js, diagonal_last=False):
        n_chains = len(js) * n_heads

        def scores(c, st):
            j, g = js[c // n_heads], c % n_heads
            rows = pl.ds(pl.multiple_of(j * tq, tq), tq)
            bias = bias_ref[g, i - j]
            s = _dot_nt(k_ref[rows, heads[g]], qqs[g]) + jnp.concatenate([bias, bias], axis=1)
            masked = diagonal_last and c // n_heads == len(js) - 1
            st["s"] = jnp.where(causal, s, NEG_INF) if masked else s

        def softmax(c, st):
            g = c % n_heads
            m = m_ref[g]
            s = st.pop("s")
            m_new = jnp.maximum(m, jnp.max(s, axis=0, keepdims=True))
            m_ref[g] = m_new
            st["alpha"] = jnp.exp(m - m_new)
            st["p"] = jnp.exp(s - m_new).astype(BF16)

        def values(c, st):
            g = c % n_heads
            vt = vt_ref[g, js[c // n_heads]]
            acc_ref[g] = st.pop("alpha") * acc_ref[g] + _dot(vt, st.pop("p"))

        _staggered(n_chains, [scores, softmax, values])

    m_ref[...] = jnp.full(m_ref.shape, NEG_INF, F32)
    acc_ref[...] = jnp.zeros(acc_ref.shape, F32)

    def pair(t, carry):
        step([2 * t, 2 * t + 1])
        return carry

    lax.fori_loop(0, i // 2, pair, 0)

    @pl.when(i % 2 == 1)
    def _():
        step([i - 1])

    step([i], diagonal_last=True)

    lp = lam_ref[...]
    lam = (jnp.exp(jnp.sum(lp[0:1] * lp[1:2], axis=-1, keepdims=True))
           - jnp.exp(jnp.sum(lp[2:3] * lp[3:4], axis=-1, keepdims=True)) + lam_init)
    for g, hs in enumerate(heads):
        acc = acc_ref[g]
        o = acc[:HEAD_DIM] / acc[HEAD_DIM:HEAD_DIM + 1]
        out = o[:, :tq] - lam * o[:, tq:]
        ms = jnp.mean(out * out, axis=0, keepdims=True)
        y = out * lax.rsqrt(ms + EPS) * g_ref[...] * (1.0 - lam_init)
        o_ref[:, hs] = y.T.astype(o_ref.dtype)


def _diff_attention(proj, bias_b, lam_params, g, *, batch, seq, tq, lam_init):
    t = proj.shape[0]
    nq = seq // tq
    nh = N_HEADS_B
    width = nh * HEAD_DIM
    assert _QB % nh == 0 and _KB % nh == 0 and _VB % nh == 0
    kv = lambda off: pl.BlockSpec((seq, width), lambda b, i: (b, off // nh))
    return pl.pallas_call(
        functools.partial(_diff_kernel, tq=tq, n_heads=nh, seq=seq, lam_init=lam_init),
        grid=(batch, nq),
        in_specs=[pl.BlockSpec((tq, width), lambda b, i: (b * nq + i, _QB // nh)),
                  kv(_KB), kv(_VB),
                  pl.BlockSpec((nh, nq, tq, tq), lambda b, i: (0, 0, 0, 0)),
                  pl.BlockSpec((4, HEAD_DIM // 2), lambda b, i: (0, 0)),
                  pl.BlockSpec((HEAD_DIM, 1), lambda b, i: (0, 0))],
        out_specs=pl.BlockSpec((tq, width), lambda b, i: (b * nq + i, 0)),
        out_shape=jax.ShapeDtypeStruct((t, width), BF16),
        scratch_shapes=[pltpu.VMEM((nh, nq, HEAD_DIM + ONES_ROWS, tq), BF16),
                        pltpu.VMEM((nh, 1, 2 * tq), F32),
                        pltpu.VMEM((nh, HEAD_DIM + ONES_ROWS, 2 * tq), F32)],
        compiler_params=_cparams(("parallel", "arbitrary")),
        name="diff_attn",
    )(proj, proj, proj, bias_b, lam_params, g)


def _stick_kernel(q_ref, k_ref, v_ref, o_ref, c_ref, acc_ref, *, tq, n_heads):
    i = pl.program_id(2)
    row = lax.broadcasted_iota(jnp.int32, (tq, tq), 0)
    col = lax.broadcasted_iota(jnp.int32, (tq, tq), 1)
    strict = col < row
    suffix = (row >= col).astype(BF16)
    suffix2 = jnp.concatenate([suffix, suffix], axis=0)
    heads = [slice(g * HEAD_DIM, (g + 1) * HEAD_DIM) for g in range(n_heads)]
    qs = [q_ref[:, hs] for hs in heads]

    def blocks(js, diagonal_first=False):
        n_chains = len(js) * n_heads
        rows_of = [pl.ds(pl.multiple_of(j * tq, tq), tq) for j in js]

        def masked(c):
            return diagonal_first and c < n_heads

        def scores(c, st):
            st["z"] = _dot_nt(qs[c % n_heads], k_ref[rows_of[c // n_heads], heads[c % n_heads]])

        def log_break(c, st):
            z = st.pop("z")
            neg_abs = lax.bitcast_convert_type(
                lax.bitcast_convert_type(z, jnp.uint32) | jnp.uint32(F32_SIGN_BIT), F32)
            w = jnp.maximum(z, 0.0) + jnp.log(1.0 + jnp.exp(neg_abs))
            if masked(c):
                w = jnp.where(strict, w, 0.0)
            hi = w.astype(BF16)
            st["hilo"] = jnp.concatenate([hi, (w - hi.astype(F32)).astype(BF16)], axis=1)
            st["zc"] = z - c_ref[c % n_heads]

        def suffix_sums(c, st):
            st["incl"] = _dot(st.pop("hilo"), suffix2)
            c_ref[c % n_heads] += st["incl"][:, 0:1]

        def weights(c, st):
            a = jnp.exp(st.pop("zc") - st.pop("incl"))
            if masked(c):
                a = jnp.where(strict, a, 0.0)
            st["a"] = a.astype(BF16)

        def values(c, st):
            acc_ref[c % n_heads] += _dot(st.pop("a"), v_ref[rows_of[c // n_heads], heads[c % n_heads]])

        _staggered(n_chains, [scores, log_break, suffix_sums, weights, values])

    c_ref[...] = jnp.zeros(c_ref.shape, F32)
    acc_ref[...] = jnp.zeros(acc_ref.shape, F32)

    @pl.when(i == 0)
    def _():
        blocks([i], diagonal_first=True)

    @pl.when(i > 0)
    def _():
        blocks([i, i - 1], diagonal_first=True)

        def pair(t, carry):
            blocks([i - 2 - 2 * t, i - 3 - 2 * t])
            return carry

        lax.fori_loop(0, (i - 1) // 2, pair, 0)

        @pl.when((i - 1) % 2 == 1)
        def _():
            blocks([0])

    for g, hs in enumerate(heads):
        o_ref[:, hs] = acc_ref[g].astype(o_ref.dtype)


def _stick_attention(proj, *, batch, seq, tq, heads_per_step):
    t = proj.shape[0]
    nq = seq // tq
    g = heads_per_step
    assert g >= 2
    width = g * HEAD_DIM
    kv = lambda off: pl.BlockSpec((seq, width), lambda b, h, i: (b, off // g + h))
    return pl.pallas_call(
        functools.partial(_stick_kernel, tq=tq, n_heads=g),
        grid=(batch, N_HEADS_C // g, nq),
        in_specs=[pl.BlockSpec((tq, width), lambda b, h, i: (b * nq + i, _QC // g + h)),
                  kv(_KC), kv(_VC)],
        out_specs=pl.BlockSpec((tq, width), lambda b, h, i: (b * nq + i, h)),
        out_shape=jax.ShapeDtypeStruct((t, N_HEADS_C * HEAD_DIM), BF16),
        scratch_shapes=[pltpu.VMEM((g, tq, 1), F32), pltpu.VMEM((g, tq, HEAD_DIM), F32)],
        compiler_params=_cparams(("parallel", "parallel", "arbitrary")),
        name="stick_attn",
    )(proj, proj, proj)


def _out_proj_kernel(x_ref, a_ref, b_ref, c_ref, w_ref, o_ref):
    ka, kb = a_ref.shape[1], b_ref.shape[1]
    acc = (_dot(a_ref[...], w_ref[:ka].astype(BF16)) + _dot(b_ref[...], w_ref[ka:ka + kb].astype(BF16))
           + _dot(c_ref[...], w_ref[ka + kb:].astype(BF16)))
    o_ref[...] = x_ref[...] + acc


def _out_proj(x, ma, mb, mc, w, layer, *, tm=OUT_PROJ_TM):
    t, d = x.shape
    act = lambda k: pl.BlockSpec((tm, k), lambda i: (i, 0))
    return pl.pallas_call(
        _out_proj_kernel,
        grid=(t // tm,),
        in_specs=[act(d), act(ma.shape[1]), act(mb.shape[1]), act(mc.shape[1]),
                  pl.BlockSpec((None,) + w.shape[1:], lambda i: (layer, 0, 0))],
        out_specs=act(d),
        out_shape=jax.ShapeDtypeStruct((t, d), F32),
        compiler_params=_cparams(("parallel",)),
        name="out_proj",
    )(x, ma, mb, mc, w)


def _mlp_kernel(x_ref, g_ref, w1_ref, w2_ref, gf_ref, o_ref, xn_ref, *, final_norm):
    f = pl.program_id(1)

    def hidden_tile_update(xn):
        h = jnp.maximum(_dot(xn, w1_ref[...].astype(BF16)), 0.0)
        return _dot((h * h).astype(BF16), w2_ref[...].astype(BF16))

    @pl.when(f == 0)
    def _():
        x = x_ref[...]
        ms = jnp.mean(x * x, axis=-1, keepdims=True)
        xn = (x * lax.rsqrt(ms + EPS) * g_ref[...]).astype(BF16)
        xn_ref[...] = xn
        o_ref[...] = x + hidden_tile_update(xn)

    @pl.when(f > 0)
    def _():
        o_ref[...] += hidden_tile_update(xn_ref[...])

    if final_norm:
        @pl.when(f == pl.num_programs(1) - 1)
        def _():
            y = o_ref[...]
            ms = jnp.mean(y * y, axis=-1, keepdims=True)
            o_ref[...] = y * lax.rsqrt(ms + EPS) * gf_ref[...]


def _mlp(x, g, w1, w2, layer, g_final, *, final_norm, tm=MLP_TM, tf=MLP_TF):
    t, d = x.shape
    dff = w1.shape[2]
    return pl.pallas_call(
        functools.partial(_mlp_kernel, final_norm=final_norm),
        grid=(t // tm, dff // tf),
        in_specs=[pl.BlockSpec((tm, d), lambda i, f: (i, 0)),
                  pl.BlockSpec((1, d), lambda i, f: (0, 0)),
                  pl.BlockSpec((None, d, tf), lambda i, f: (layer, 0, f)),
                  pl.BlockSpec((None, tf, d), lambda i, f: (layer, f, 0)),
                  pl.BlockSpec((1, d), lambda i, f: (0, 0))],
        out_specs=pl.BlockSpec((tm, d), lambda i, f: (i, 0)),
        out_shape=jax.ShapeDtypeStruct((t, d), F32),
        scratch_shapes=[pltpu.VMEM((tm, d), BF16)],
        compiler_params=_cparams(("parallel", "arbitrary")),
        name="mlp",
    )(x, g, w1, w2, g_final)


def kernel(x, w_in, w_out, g_attn, g_mlp, w_mlp_in, w_mlp_out, rel_bias_table,
           diff_lam_q1, diff_lam_k1, diff_lam_q2, diff_lam_k2, diff_subln_g, g_final):
    batch, seq, d_model = x.shape
    depth = w_in.shape[0]

    i = np.arange(BLK)[:, None]
    c = np.arange(2 * BLK)[None, :]
    steps = i + BLK - c
    buckets_a = np.stack([_t5_bucket_np(steps * dil) for _, dil in DILATED_PATTERNS])
    dist = (np.arange(seq // TQ_DIFF)[:, None, None] * TQ_DIFF + np.arange(TQ_DIFF)[None, None, :]
            - np.arange(TQ_DIFF)[None, :, None])
    buckets_b = _t5_bucket_np(dist)
    bias_a = _build_bias(rel_bias_table[:, :N_HEADS_A], buckets_a)
    bias_b = _build_bias(rel_bias_table[:, N_HEADS_A:], buckets_b)

    colscale = np.ones((1, w_in.shape[2]), np.float32)
    colscale[0, _QA * HEAD_DIM:_KA * HEAD_DIM] = 1.0 / math.sqrt(HEAD_DIM)
    colscale[0, _QB * HEAD_DIM:_KB * HEAD_DIM] = 1.0 / math.sqrt(HEAD_DIM // 2)
    colscale[0, _QC * HEAD_DIM:_KC * HEAD_DIM] = 1.0 / math.sqrt(HEAD_DIM)
    colscale = jnp.asarray(colscale)

    tile = N_HEADS_A * HEAD_DIM
    assert N_HEADS_C == N_HEADS_A and (3 * N_HEADS_B * HEAD_DIM) % tile == 0
    tiles_a, tiles_b = 3, 3 * N_HEADS_B * HEAD_DIM // tile
    src_tiles = (tuple(range(tiles_a)) + tuple(range(tiles_a + tiles_b, 2 * tiles_a + tiles_b))
                 + tuple(range(tiles_a, tiles_a + tiles_b)))

    xf = x.reshape(batch * seq, d_model)
    for l in range(depth):
        lam_init = 0.8 - 0.6 * math.exp(-0.3 * l)
        lam_params = jnp.stack([diff_lam_q1[l], diff_lam_k1[l], diff_lam_q2[l], diff_lam_k2[l]]).astype(F32)
        proj = _rms_proj(xf, g_attn[l][None, :], w_in, l, colscale, src_tiles)
        ma = _dilated_attention(proj, bias_a, batch=batch, seq=seq)
        mb = _diff_attention(proj, bias_b, lam_params, diff_subln_g[l][:, None],
                             batch=batch, seq=seq, tq=TQ_DIFF, lam_init=lam_init)
        mc = _stick_attention(proj, batch=batch, seq=seq, tq=TQ_STICK, heads_per_step=STICK_HEADS_PER_STEP)
        xf = _out_proj(xf, ma, mb, mc, w_out, l)
        xf = _mlp(xf, g_mlp[l][None, :], w_mlp_in, w_mlp_out, l, g_final[None, :], final_norm=(l == depth - 1))
    return xf.reshape(batch, seq, d_model)
```

```python
import functools
import math

import jax
import jax.numpy as jnp
import numpy as np
from jax import lax
from jax.experimental import pallas as pl
from jax.experimental.pallas import tpu as pltpu

HEAD_DIM = 128
N_HEADS_A = 6
N_HEADS_B = 4
N_HEADS_C = 6
DILATED_PATTERNS = ((128, 1), (512, 4), (2048, 16))
BLK = 128
NUM_BUCKETS = 32
MAX_DISTANCE = 2048
EPS = 1e-6
NEG_INF = -1e30

F32 = jnp.float32
BF16 = jnp.bfloat16

V7X_VMEM_BYTES = 64 * 1024 * 1024
BF16_SUBLANE_TILE = 16
F32_SIGN_BIT = 0x80000000

ONES_ROWS = BF16_SUBLANE_TILE

_VMEM_LIMIT = V7X_VMEM_BYTES * 7 // 8
PROJ_TM = 1024
OUT_PROJ_TM = 512
MLP_TM, MLP_TF = 1024, 512
TQ_DIFF = 256
TQ_STICK = 256
STICK_HEADS_PER_STEP = 6
DILATED_LAG = 3
MERGE_ROWS = 256

_QA, _KA, _VA = 0, N_HEADS_A, 2 * N_HEADS_A
_QC = 3 * N_HEADS_A
_KC, _VC = _QC + N_HEADS_C, _QC + 2 * N_HEADS_C
_QB = _QC + 3 * N_HEADS_C
_KB, _VB = _QB + N_HEADS_B, _QB + 2 * N_HEADS_B


def _cparams(sem):
    return pltpu.CompilerParams(dimension_semantics=sem, vmem_limit_bytes=_VMEM_LIMIT)


def _dot_nt(a, b):
    return lax.dot_general(a, b, (((1,), (1,)), ((), ())), preferred_element_type=F32)


def _dot(a, b):
    return jnp.dot(a, b, preferred_element_type=F32)


def _staggered(n_chains, stages, states=None):
    states = [dict() for _ in range(n_chains)] if states is None else states
    for t in range(n_chains + len(stages) - 1):
        for k, stage in enumerate(stages):
            c = t - k
            if stage is not None and 0 <= c < n_chains:
                stage(c, states[c])
    return states


def _t5_bucket_np(dist):
    n = np.maximum(dist, 0)
    max_exact = NUM_BUCKETS // 2
    nf = np.maximum(n, max_exact).astype(np.float32)
    large = max_exact + (np.log(nf / np.float32(max_exact)) / np.float32(math.log(MAX_DISTANCE / max_exact))
                         * np.float32(NUM_BUCKETS - max_exact)).astype(np.int32)
    large = np.minimum(large, NUM_BUCKETS - 1)
    return np.where(n < max_exact, n, large).astype(np.int32)


def _bias_kernel(tab_ref, bkt_ref, o_ref, *, n_heads, tile_buckets):
    for t, present in enumerate(tile_buckets):
        b = bkt_ref[t]
        for h in range(n_heads):
            acc = jnp.full(b.shape, tab_ref[present[0], h], F32)
            for k in present[1:]:
                acc = jnp.where(b == k, tab_ref[k, h], acc)
            o_ref[h, t] = acc


def _build_bias(table, buckets):
    n_heads = table.shape[1]
    n, r, c = buckets.shape
    tile_buckets = tuple(tuple(int(k) for k in np.unique(buckets[t])) for t in range(n))
    return pl.pallas_call(
        functools.partial(_bias_kernel, n_heads=n_heads, tile_buckets=tile_buckets),
        in_specs=[pl.BlockSpec(memory_space=pltpu.SMEM),
                  pl.BlockSpec(memory_space=pltpu.VMEM)],
        out_specs=pl.BlockSpec(memory_space=pltpu.VMEM),
        out_shape=jax.ShapeDtypeStruct((n_heads, n, r, c), F32),
        compiler_params=pltpu.CompilerParams(vmem_limit_bytes=_VMEM_LIMIT),
        name="bias_table",
    )(table, jnp.asarray(buckets))


def _rms_proj_kernel(x_ref, g_ref, w_ref, cs_ref, o_ref, xn_ref):
    def project(xn):
        acc = _dot(xn, w_ref[...].astype(BF16))
        o_ref[...] = (acc * cs_ref[...]).astype(o_ref.dtype)

    @pl.when(pl.program_id(1) == 0)
    def _():
        x = x_ref[...]
        ms = jnp.mean(x * x, axis=-1, keepdims=True)
        xn = (x * lax.rsqrt(ms + EPS) * g_ref[...]).astype(BF16)
        xn_ref[...] = xn
        project(xn)

    @pl.when(pl.program_id(1) > 0)
    def _():
        project(xn_ref[...])


def _rms_proj(x, g, w, layer, colscale, src_tiles, *, tm=PROJ_TM):
    t, d = x.shape
    n = w.shape[2]
    tn = n // len(src_tiles)

    def src_tile(j):
        idx = src_tiles[-1]
        for k in reversed(range(len(src_tiles) - 1)):
            idx = jnp.where(j == k, src_tiles[k], idx)
        return idx

    return pl.pallas_call(
        _rms_proj_kernel,
        grid=(t // tm, n // tn),
        in_specs=[pl.BlockSpec((tm, d), lambda i, j: (i, 0)),
                  pl.BlockSpec((1, d), lambda i, j: (0, 0)),
                  pl.BlockSpec((None, d, tn), lambda i, j: (layer, 0, src_tile(j))),
                  pl.BlockSpec((1, tn), lambda i, j: (0, j))],
        out_specs=pl.BlockSpec((tm, tn), lambda i, j: (i, j)),
        out_shape=jax.ShapeDtypeStruct((t, n), BF16),
        scratch_shapes=[pltpu.VMEM((tm, d), BF16)],
        compiler_params=_cparams(("parallel", "arbitrary")),
        name="rms_proj",
    )(x, g, w, colscale)


def _dilated_kernel(q_ref, k_ref, v_ref, bias_ref, o_ref, qf, kf, vf, acc_s, m_s, l_s, *, seq, lag):
    qf[...] = q_ref[...].astype(F32)
    kf[...] = k_ref[...].astype(F32)
    vf[...] = v_ref[...].astype(F32)

    row = lax.broadcasted_iota(jnp.int32, (BLK, 2 * BLK), 0)
    col = lax.broadcasted_iota(jnp.int32, (BLK, 2 * BLK), 1)
    band_mask = (col >= row) & (col <= row + BLK)
    row0 = lax.broadcasted_iota(jnp.int32, (BLK, BLK), 0)
    col0 = lax.broadcasted_iota(jnp.int32, (BLK, BLK), 1)
    diag_mask = col0 <= row0

    def rows_of(start, size, dil):
        return pl.ds(start, size, stride=dil) if dil > 1 else pl.ds(start, size)

    def ld(ref, start, size, dil):
        return ref[rows_of(start, size, dil), :].astype(BF16)

    chains = []
    for p_idx, (window, dil) in enumerate(DILATED_PATTERNS):
        assert window // dil == BLK
        for r in range(dil):
            for n in range(seq // dil // BLK):
                chains.append((p_idx, dil, r + dil * BLK * n, None if n == 0 else r + dil * BLK * (n - 1)))

    def scores(c, st):
        p_idx, dil, q0, k0 = chains[c]
        qb = ld(qf, q0, BLK, dil)
        if k0 is None:
            s = _dot_nt(qb, ld(kf, q0, BLK, dil)) + bias_ref[0, p_idx, :, BLK:]
            st["s"] = jnp.where(diag_mask, s, NEG_INF)
        else:
            s = _dot_nt(qb, ld(kf, k0, 2 * BLK, dil)) + bias_ref[0, p_idx]
            st["s"] = jnp.where(band_mask, s, NEG_INF)

    def softmax(c, st):
        s = st.pop("s")
        st["m"] = jnp.max(s, axis=-1, keepdims=True)
        p = jnp.exp(s - st["m"])
        st["l"] = jnp.sum(p, axis=-1, keepdims=True)
        st["p"] = p.astype(BF16)

    def values(c, st):
        p_idx, dil, q0, k0 = chains[c]
        vb = ld(vf, q0, BLK, dil) if k0 is None else ld(vf, k0, 2 * BLK, dil)
        rows = rows_of(q0, BLK, dil)
        acc_s[p_idx, rows, :] = _dot(st.pop("p"), vb)
        m_s[p_idx, rows, :] = jnp.broadcast_to(st.pop("m"), (BLK, HEAD_DIM))
        l_s[p_idx, rows, :] = jnp.broadcast_to(st.pop("l"), (BLK, HEAD_DIM))

    _staggered(len(chains), [scores] + [None] * (lag - 1) + [softmax] + [None] * (lag - 1) + [values])

    chunk = MERGE_ROWS

    def merge(ci, carry):
        rows = pl.ds(pl.multiple_of(ci * chunk, chunk), chunk)
        m0, m1, m2 = m_s[0, rows, :], m_s[1, rows, :], m_s[2, rows, :]
        mm = jnp.maximum(jnp.maximum(m0, m1), m2)
        w0, w1, w2 = jnp.exp(m0 - mm), jnp.exp(m1 - mm), jnp.exp(m2 - mm)
        num = w0 * acc_s[0, rows, :] + w1 * acc_s[1, rows, :] + w2 * acc_s[2, rows, :]
        den = w0 * l_s[0, rows, :] + w1 * l_s[1, rows, :] + w2 * l_s[2, rows, :]
        o_ref[rows, :] = (num / den).astype(o_ref.dtype)
        return carry

    lax.fori_loop(0, seq // chunk, merge, 0)


def _dilated_attention(proj, bias_a, *, batch, seq):
    t = proj.shape[0]
    n_pat = len(DILATED_PATTERNS)
    blk = lambda off: pl.BlockSpec((seq, HEAD_DIM), lambda b, h: (b, off + h))
    return pl.pallas_call(
        functools.partial(_dilated_kernel, seq=seq, lag=DILATED_LAG),
        grid=(batch, N_HEADS_A),
        in_specs=[blk(_QA), blk(_KA), blk(_VA),
                  pl.BlockSpec((1, n_pat, BLK, 2 * BLK), lambda b, h: (h, 0, 0, 0))],
        out_specs=pl.BlockSpec((seq, HEAD_DIM), lambda b, h: (b, h)),
        out_shape=jax.ShapeDtypeStruct((t, N_HEADS_A * HEAD_DIM), BF16),
        scratch_shapes=[pltpu.VMEM((seq, HEAD_DIM), F32)] * 3
                       + [pltpu.VMEM((n_pat, seq, HEAD_DIM), F32)] * 3,
        compiler_params=_cparams(("parallel", "parallel")),
        name="dilated_attn",
    )(proj, proj, proj, bias_a)


def _diff_kernel(q_ref, k_ref, v_ref, bias_ref, lam_ref, g_ref, o_ref, vt_ref, m_ref, acc_ref, *, tq, n_heads, seq,
                 lam_init):
    i = pl.program_id(1)
    half = HEAD_DIM // 2
    heads = [slice(g * HEAD_DIM, (g + 1) * HEAD_DIM) for g in range(n_heads)]

    @pl.when(i == 0)
    def _():
        r = lax.broadcasted_iota(jnp.int32, (HEAD_DIM, HEAD_DIM), 0)
        c = lax.broadcasted_iota(jnp.int32, (HEAD_DIM, HEAD_DIM), 1)
        eye = (r == c).astype(BF16)

        def transpose_block(jb, carry):
            rows = pl.ds(pl.multiple_of(jb * tq, tq), tq)
            for g, hs in enumerate(heads):
                vt_ref[g, jb, :HEAD_DIM] = _dot_nt(eye, v_ref[rows, hs]).astype(BF16)
                vt_ref[g, jb, HEAD_DIM:] = jnp.ones((ONES_ROWS, tq), BF16)
            return carry

        lax.fori_loop(0, seq // tq, transpose_block, 0)

    lane = lax.broadcasted_iota(jnp.int32, (tq, HEAD_DIM), 1)
    qqs = []
    for hs in heads:
        q = q_ref[:, hs]
        zero = jnp.zeros_like(q)
        qqs.append(jnp.concatenate([jnp.where(lane < half, q, zero), jnp.where(lane >= half, q, zero)], axis=0))
    key = lax.broadcasted_iota(jnp.int32, (tq, 2 * tq), 0)
    qry = lax.broadcasted_iota(jnp.int32, (tq, 2 * tq), 1)
    causal = key <= jnp.where(qry >= tq, qry - tq, qry)

    def step(js, diagonal_last=False):
        n_chains = len(js) * n_heads

        def scores(c, st):
            j, g = js[c // n_heads], c % n_heads
            rows = pl.ds(pl.multiple_of(j * tq, tq), tq)
            bias = bias_ref[g, i - j]
            s = _dot_nt(k_ref[rows, heads[g]], qqs[g]) + jnp.concatenate([bias, bias], axis=1)
            masked = diagonal_last and c // n_heads == len(js) - 1
            st["s"] = jnp.where(causal, s, NEG_INF) if masked else s

        def softmax(c, st):
            g = c % n_heads
            m = m_ref[g]
            s = st.pop("s")
            m_new = jnp.maximum(m, jnp.max(s, axis=0, keepdims=True))
            m_ref[g] = m_new
            st["alpha"] = jnp.exp(m - m_new)
            st["p"] = jnp.exp(s - m_new).astype(BF16)

        def values(c, st):
            g = c % n_heads
            vt = vt_ref[g, js[c // n_heads]]
            acc_ref[g] = st.pop("alpha") * acc_ref[g] + _dot(vt, st.pop("p"))

        _staggered(n_chains, [scores, softmax, values])

    m_ref[...] = jnp.full(m_ref.shape, NEG_INF, F32)
    acc_ref[...] = jnp.zeros(acc_ref.shape, F32)

    def triple(t, carry):
        step([3 * t, 3 * t + 1, 3 * t + 2])
        return carry

    lax.fori_loop(0, i // 3, triple, 0)

    @pl.when(i % 3 == 1)
    def _():
        step([i - 1])

    @pl.when(i % 3 == 2)
    def _():
        step([i - 2, i - 1])

    step([i], diagonal_last=True)

    lp = lam_ref[...]
    lam = (jnp.exp(jnp.sum(lp[0:1] * lp[1:2], axis=-1, keepdims=True))
           - jnp.exp(jnp.sum(lp[2:3] * lp[3:4], axis=-1, keepdims=True)) + lam_init)
    for g, hs in enumerate(heads):
        acc = acc_ref[g]
        o = acc[:HEAD_DIM] / acc[HEAD_DIM:HEAD_DIM + 1]
        out = o[:, :tq] - lam * o[:, tq:]
        ms = jnp.mean(out * out, axis=0, keepdims=True)
        y = out * lax.rsqrt(ms + EPS) * g_ref[...] * (1.0 - lam_init)
        o_ref[:, hs] = y.T.astype(o_ref.dtype)


def _diff_attention(proj, bias_b, lam_params, g, *, batch, seq, tq, lam_init):
    t = proj.shape[0]
    nq = seq // tq
    nh = N_HEADS_B
    width = nh * HEAD_DIM
    assert _QB % nh == 0 and _KB % nh == 0 and _VB % nh == 0
    kv = lambda off: pl.BlockSpec((seq, width), lambda b, i: (b, off // nh))
    return pl.pallas_call(
        functools.partial(_diff_kernel, tq=tq, n_heads=nh, seq=seq, lam_init=lam_init),
        grid=(batch, nq),
        in_specs=[pl.BlockSpec((tq, width), lambda b, i: (b * nq + i, _QB // nh)),
                  kv(_KB), kv(_VB),
                  pl.BlockSpec((nh, nq, tq, tq), lambda b, i: (0, 0, 0, 0)),
                  pl.BlockSpec((4, HEAD_DIM // 2), lambda b, i: (0, 0)),
                  pl.BlockSpec((HEAD_DIM, 1), lambda b, i: (0, 0))],
        out_specs=pl.BlockSpec((tq, width), lambda b, i: (b * nq + i, 0)),
        out_shape=jax.ShapeDtypeStruct((t, width), BF16),
        scratch_shapes=[pltpu.VMEM((nh, nq, HEAD_DIM + ONES_ROWS, tq), BF16),
                        pltpu.VMEM((nh, 1, 2 * tq), F32),
                        pltpu.VMEM((nh, HEAD_DIM + ONES_ROWS, 2 * tq), F32)],
        compiler_params=_cparams(("parallel", "arbitrary")),
        name="diff_attn",
    )(proj, proj, proj, bias_b, lam_params, g)


def _stick_kernel(q_ref, k_ref, v_ref, o_ref, c_ref, acc_ref, *, tq, n_heads):
    i = pl.program_id(2)
    row = lax.broadcasted_iota(jnp.int32, (tq, tq), 0)
    col = lax.broadcasted_iota(jnp.int32, (tq, tq), 1)
    strict = col < row
    suffix = (row >= col).astype(BF16)
    suffix2 = jnp.concatenate([suffix, suffix], axis=0)
    heads = [slice(g * HEAD_DIM, (g + 1) * HEAD_DIM) for g in range(n_heads)]
    qs = [q_ref[:, hs] for hs in heads]

    def blocks(js, diagonal_first=False):
        n_chains = len(js) * n_heads
        rows_of = [pl.ds(pl.multiple_of(j * tq, tq), tq) for j in js]

        def masked(c):
            return diagonal_first and c < n_heads

        def scores(c, st):
            st["z"] = _dot_nt(qs[c % n_heads], k_ref[rows_of[c // n_heads], heads[c % n_heads]])

        def log_break(c, st):
            z = st.pop("z")
            neg_abs = lax.bitcast_convert_type(
                lax.bitcast_convert_type(z, jnp.uint32) | jnp.uint32(F32_SIGN_BIT), F32)
            w = jnp.maximum(z, 0.0) + jnp.log(1.0 + jnp.exp(neg_abs))
            if masked(c):
                w = jnp.where(strict, w, 0.0)
            hi = w.astype(BF16)
            st["hilo"] = jnp.concatenate([hi, (w - hi.astype(F32)).astype(BF16)], axis=1)
            st["zc"] = z - c_ref[c % n_heads]

        def suffix_sums(c, st):
            st["incl"] = _dot(st.pop("hilo"), suffix2)
            c_ref[c % n_heads] += st["incl"][:, 0:1]

        def weights(c, st):
            a = jnp.exp(st.pop("zc") - st.pop("incl"))
            if masked(c):
                a = jnp.where(strict, a, 0.0)
            st["a"] = a.astype(BF16)

        def values(c, st):
            acc_ref[c % n_heads] += _dot(st.pop("a"), v_ref[rows_of[c // n_heads], heads[c % n_heads]])

        _staggered(n_chains, [scores, log_break, suffix_sums, weights, values])

    c_ref[...] = jnp.zeros(c_ref.shape, F32)
    acc_ref[...] = jnp.zeros(acc_ref.shape, F32)

    @pl.when(i == 0)
    def _():
        blocks([i], diagonal_first=True)

    @pl.when(i == 1)
    def _():
        blocks([i, i - 1], diagonal_first=True)

    @pl.when(i > 1)
    def _():
        blocks([i, i - 1, i - 2], diagonal_first=True)

        def triple(t, carry):
            blocks([i - 3 - 3 * t, i - 4 - 3 * t, i - 5 - 3 * t])
            return carry

        lax.fori_loop(0, (i - 2) // 3, triple, 0)

        @pl.when((i - 2) % 3 == 1)
        def _():
            blocks([0])

        @pl.when((i - 2) % 3 == 2)
        def _():
            blocks([1, 0])

    for g, hs in enumerate(heads):
        o_ref[:, hs] = acc_ref[g].astype(o_ref.dtype)


def _stick_attention(proj, *, batch, seq, tq, heads_per_step):
    t = proj.shape[0]
    nq = seq // tq
    g = heads_per_step
    assert g >= 2
    width = g * HEAD_DIM
    kv = lambda off: pl.BlockSpec((seq, width), lambda b, h, i: (b, off // g + h))
    return pl.pallas_call(
        functools.partial(_stick_kernel, tq=tq, n_heads=g),
        grid=(batch, N_HEADS_C // g, nq),
        in_specs=[pl.BlockSpec((tq, width), lambda b, h, i: (b * nq + i, _QC // g + h)),
                  kv(_KC), kv(_VC)],
        out_specs=pl.BlockSpec((tq, width), lambda b, h, i: (b * nq + i, h)),
        out_shape=jax.ShapeDtypeStruct((t, N_HEADS_C * HEAD_DIM), BF16),
        scratch_shapes=[pltpu.VMEM((g, tq, 1), F32), pltpu.VMEM((g, tq, HEAD_DIM), F32)],
        compiler_params=_cparams(("parallel", "parallel", "arbitrary")),
        name="stick_attn",
    )(proj, proj, proj)


def _out_proj_kernel(x_ref, a_ref, b_ref, c_ref, w_ref, o_ref):
    ka, kb = a_ref.shape[1], b_ref.shape[1]
    acc = (_dot(a_ref[...], w_ref[:ka].astype(BF16)) + _dot(b_ref[...], w_ref[ka:ka + kb].astype(BF16))
           + _dot(c_ref[...], w_ref[ka + kb:].astype(BF16)))
    o_ref[...] = x_ref[...] + acc


def _out_proj(x, ma, mb, mc, w, layer, *, tm=OUT_PROJ_TM):
    t, d = x.shape
    act = lambda k: pl.BlockSpec((tm, k), lambda i: (i, 0))
    return pl.pallas_call(
        _out_proj_kernel,
        grid=(t // tm,),
        in_specs=[act(d), act(ma.shape[1]), act(mb.shape[1]), act(mc.shape[1]),
                  pl.BlockSpec((None,) + w.shape[1:], lambda i: (layer, 0, 0))],
        out_specs=act(d),
        out_shape=jax.ShapeDtypeStruct((t, d), F32),
        compiler_params=_cparams(("parallel",)),
        name="out_proj",
    )(x, ma, mb, mc, w)


def _mlp_kernel(x_ref, g_ref, w1_ref, w2_ref, gf_ref, o_ref, xn_ref, *, final_norm):
    f = pl.program_id(1)

    def hidden_tile_update(xn):
        h = jnp.maximum(_dot(xn, w1_ref[...].astype(BF16)), 0.0)
        return _dot((h * h).astype(BF16), w2_ref[...].astype(BF16))

    @pl.when(f == 0)
    def _():
        x = x_ref[...]
        ms = jnp.mean(x * x, axis=-1, keepdims=True)
        xn = (x * lax.rsqrt(ms + EPS) * g_ref[...]).astype(BF16)
        xn_ref[...] = xn
        o_ref[...] = x + hidden_tile_update(xn)

    @pl.when(f > 0)
    def _():
        o_ref[...] += hidden_tile_update(xn_ref[...])

    if final_norm:
        @pl.when(f == pl.num_programs(1) - 1)
        def _():
            y = o_ref[...]
            ms = jnp.mean(y * y, axis=-1, keepdims=True)
            o_ref[...] = y * lax.rsqrt(ms + EPS) * gf_ref[...]


def _mlp(x, g, w1, w2, layer, g_final, *, final_norm, tm=MLP_TM, tf=MLP_TF):
    t, d = x.shape
    dff = w1.shape[2]
    return pl.pallas_call(
        functools.partial(_mlp_kernel, final_norm=final_norm),
        grid=(t // tm, dff // tf),
        in_specs=[pl.BlockSpec((tm, d), lambda i, f: (i, 0)),
                  pl.BlockSpec((1, d), lambda i, f: (0, 0)),
                  pl.BlockSpec((None, d, tf), lambda i, f: (layer, 0, f)),
                  pl.BlockSpec((None, tf, d), lambda i, f: (layer, f, 0)),
                  pl.BlockSpec((1, d), lambda i, f: (0, 0))],
        out_specs=pl.BlockSpec((tm, d), lambda i, f: (i, 0)),
        out_shape=jax.ShapeDtypeStruct((t, d), F32),
        scratch_shapes=[pltpu.VMEM((tm, d), BF16)],
        compiler_params=_cparams(("parallel", "arbitrary")),
        name="mlp",
    )(x, g, w1, w2, g_final)


def kernel(x, w_in, w_out, g_attn, g_mlp, w_mlp_in, w_mlp_out, rel_bias_table,
           diff_lam_q1, diff_lam_k1, diff_lam_q2, diff_lam_k2, diff_subln_g, g_final):
    batch, seq, d_model = x.shape
    depth = w_in.shape[0]

    i = np.arange(BLK)[:, None]
    c = np.arange(2 * BLK)[None, :]
    steps = i + BLK - c
    buckets_a = np.stack([_t5_bucket_np(steps * dil) for _, dil in DILATED_PATTERNS])
    dist = (np.arange(seq // TQ_DIFF)[:, None, None] * TQ_DIFF + np.arange(TQ_DIFF)[None, None, :]
            - np.arange(TQ_DIFF)[None, :, None])
    buckets_b = _t5_bucket_np(dist)
    bias_a = _build_bias(rel_bias_table[:, :N_HEADS_A], buckets_a)
    bias_b = _build_bias(rel_bias_table[:, N_HEADS_A:], buckets_b)

    colscale = np.ones((1, w_in.shape[2]), np.float32)
    colscale[0, _QA * HEAD_DIM:_KA * HEAD_DIM] = 1.0 / math.sqrt(HEAD_DIM)
    colscale[0, _QB * HEAD_DIM:_KB * HEAD_DIM] = 1.0 / math.sqrt(HEAD_DIM // 2)
    colscale[0, _QC * HEAD_DIM:_KC * HEAD_DIM] = 1.0 / math.sqrt(HEAD_DIM)
    colscale = jnp.asarray(colscale)

    tile = N_HEADS_A * HEAD_DIM
    assert N_HEADS_C == N_HEADS_A and (3 * N_HEADS_B * HEAD_DIM) % tile == 0
    tiles_a, tiles_b = 3, 3 * N_HEADS_B * HEAD_DIM // tile
    src_tiles = (tuple(range(tiles_a)) + tuple(range(tiles_a + tiles_b, 2 * tiles_a + tiles_b))
                 + tuple(range(tiles_a, tiles_a + tiles_b)))

    xf = x.reshape(batch * seq, d_model)
    for l in range(depth):
        lam_init = 0.8 - 0.6 * math.exp(-0.3 * l)
        lam_params = jnp.stack([diff_lam_q1[l], diff_lam_k1[l], diff_lam_q2[l], diff_lam_k2[l]]).astype(F32)
        proj = _rms_proj(xf, g_attn[l][None, :], w_in, l, colscale, src_tiles)
        ma = _dilated_attention(proj, bias_a, batch=batch, seq=seq)
        mb = _diff_attention(proj, bias_b, lam_params, diff_subln_g[l][:, None],
                             batch=batch, seq=seq, tq=TQ_DIFF, lam_init=lam_init)
        mc = _stick_attention(proj, batch=batch, seq=seq, tq=TQ_STICK, heads_per_step=STICK_HEADS_PER_STEP)
        xf = _out_proj(xf, ma, mb, mc, w_out, l)
        xf = _mlp(xf, g_mlp[l][None, :], w_mlp_in, w_mlp_out, l, g_final[None, :], final_norm=(l == depth - 1))
    return xf.reshape(batch, seq, d_model)
```

```python
import functools
import math

import jax
import jax.numpy as jnp
import numpy as np
from jax import lax
from jax.experimental import pallas as pl
from jax.experimental.pallas import tpu as pltpu

HEAD_DIM = 128
N_HEADS_A = 6
N_HEADS_B = 4
N_HEADS_C = 6
DILATED_PATTERNS = ((128, 1), (512, 4), (2048, 16))
BLK = 128
NUM_BUCKETS = 32
MAX_DISTANCE = 2048
EPS = 1e-6
NEG_INF = -1e30

F32 = jnp.float32
BF16 = jnp.bfloat16

V7X_VMEM_BYTES = 64 * 1024 * 1024
BF16_SUBLANE_TILE = 16
F32_SIGN_BIT = 0x80000000

ONES_ROWS = BF16_SUBLANE_TILE

_VMEM_LIMIT = V7X_VMEM_BYTES * 7 // 8
PROJ_TM = 1024
PROJ_WEIGHT_SLOTS = 3
OUT_PROJ_TM = 512
MLP_TM, MLP_TF = 1024, 512
TQ_DIFF = 256
TQ_STICK = 256
STICK_HEADS_PER_STEP = 6
DILATED_LAG = 3
MERGE_ROWS = 256

_QA, _KA, _VA = 0, N_HEADS_A, 2 * N_HEADS_A
_QC = 3 * N_HEADS_A
_KC, _VC = _QC + N_HEADS_C, _QC + 2 * N_HEADS_C
_QB = _QC + 3 * N_HEADS_C
_KB, _VB = _QB + N_HEADS_B, _QB + 2 * N_HEADS_B


def _cparams(sem):
    return pltpu.CompilerParams(dimension_semantics=sem, vmem_limit_bytes=_VMEM_LIMIT)


def _dot_nt(a, b):
    return lax.dot_general(a, b, (((1,), (1,)), ((), ())), preferred_element_type=F32)


def _dot(a, b):
    return jnp.dot(a, b, preferred_element_type=F32)


def _staggered(n_chains, stages, states=None):
    states = [dict() for _ in range(n_chains)] if states is None else states
    for t in range(n_chains + len(stages) - 1):
        for k, stage in enumerate(stages):
            c = t - k
            if stage is not None and 0 <= c < n_chains:
                stage(c, states[c])
    return states


def _t5_bucket_np(dist):
    n = np.maximum(dist, 0)
    max_exact = NUM_BUCKETS // 2
    nf = np.maximum(n, max_exact).astype(np.float32)
    large = max_exact + (np.log(nf / np.float32(max_exact)) / np.float32(math.log(MAX_DISTANCE / max_exact))
                         * np.float32(NUM_BUCKETS - max_exact)).astype(np.int32)
    large = np.minimum(large, NUM_BUCKETS - 1)
    return np.where(n < max_exact, n, large).astype(np.int32)


def _bias_kernel(tab_ref, bkt_ref, o_ref, *, n_heads, tile_buckets):
    for t, present in enumerate(tile_buckets):
        b = bkt_ref[t]
        for h in range(n_heads):
            acc = jnp.full(b.shape, tab_ref[present[0], h], F32)
            for k in present[1:]:
                acc = jnp.where(b == k, tab_ref[k, h], acc)
            o_ref[h, t] = acc


def _build_bias(table, buckets):
    n_heads = table.shape[1]
    n, r, c = buckets.shape
    tile_buckets = tuple(tuple(int(k) for k in np.unique(buckets[t])) for t in range(n))
    return pl.pallas_call(
        functools.partial(_bias_kernel, n_heads=n_heads, tile_buckets=tile_buckets),
        in_specs=[pl.BlockSpec(memory_space=pltpu.SMEM),
                  pl.BlockSpec(memory_space=pltpu.VMEM)],
        out_specs=pl.BlockSpec(memory_space=pltpu.VMEM),
        out_shape=jax.ShapeDtypeStruct((n_heads, n, r, c), F32),
        compiler_params=pltpu.CompilerParams(vmem_limit_bytes=_VMEM_LIMIT),
        name="bias_table",
    )(table, jnp.asarray(buckets))


def _rms_proj_kernel(x_ref, g_ref, w_hbm, cs_ref, o_ref, xn_ref, w_ring, sems, *, layer, src_tiles):
    nj, depth = len(src_tiles), w_ring.shape[0]
    tn = w_ring.shape[2]
    step = pl.program_id(0) * nj + pl.program_id(1)
    n_steps = pl.num_programs(0) * nj

    def tile_copy(s):
        j = s % nj
        src = src_tiles[-1]
        for k in reversed(range(nj - 1)):
            src = jnp.where(j == k, src_tiles[k], src)
        slot = s % depth
        cols = pl.ds(pl.multiple_of(src * tn, tn), tn)
        return pltpu.make_async_copy(w_hbm.at[layer, :, cols], w_ring.at[slot], sems.at[slot])

    @pl.when(step == 0)
    def _():
        for s in range(depth - 1):
            tile_copy(s).start()

    @pl.when(step + depth - 1 < n_steps)
    def _():
        tile_copy(step + depth - 1).start()

    tile_copy(step).wait()

    def project(xn):
        acc = _dot(xn, w_ring[step % depth].astype(BF16))
        o_ref[...] = (acc * cs_ref[...]).astype(o_ref.dtype)

    @pl.when(pl.program_id(1) == 0)
    def _():
        x = x_ref[...]
        ms = jnp.mean(x * x, axis=-1, keepdims=True)
        xn = (x * lax.rsqrt(ms + EPS) * g_ref[...]).astype(BF16)
        xn_ref[...] = xn
        project(xn)

    @pl.when(pl.program_id(1) > 0)
    def _():
        project(xn_ref[...])


def _rms_proj(x, g, w, layer, colscale, src_tiles, *, tm=PROJ_TM):
    t, d = x.shape
    n = w.shape[2]
    tn = n // len(src_tiles)
    return pl.pallas_call(
        functools.partial(_rms_proj_kernel, layer=layer, src_tiles=src_tiles),
        grid=(t // tm, n // tn),
        in_specs=[pl.BlockSpec((tm, d), lambda i, j: (i, 0)),
                  pl.BlockSpec((1, d), lambda i, j: (0, 0)),
                  pl.BlockSpec(memory_space=pl.ANY),
                  pl.BlockSpec((1, tn), lambda i, j: (0, j))],
        out_specs=pl.BlockSpec((tm, tn), lambda i, j: (i, j)),
        out_shape=jax.ShapeDtypeStruct((t, n), BF16),
        scratch_shapes=[pltpu.VMEM((tm, d), BF16),
                        pltpu.VMEM((PROJ_WEIGHT_SLOTS, d, tn), w.dtype),
                        pltpu.SemaphoreType.DMA((PROJ_WEIGHT_SLOTS,))],
        compiler_params=_cparams(("arbitrary", "arbitrary")),
        name="rms_proj",
    )(x, g, w, colscale)


def _dilated_kernel(q_ref, k_ref, v_ref, bias_ref, o_ref, qf, kf, vf, acc_s, m_s, l_s, *, seq, lag):
    qf[...] = q_ref[...].astype(F32)
    kf[...] = k_ref[...].astype(F32)
    vf[...] = v_ref[...].astype(F32)

    row = lax.broadcasted_iota(jnp.int32, (BLK, 2 * BLK), 0)
    col = lax.broadcasted_iota(jnp.int32, (BLK, 2 * BLK), 1)
    band_mask = (col >= row) & (col <= row + BLK)
    row0 = lax.broadcasted_iota(jnp.int32, (BLK, BLK), 0)
    col0 = lax.broadcasted_iota(jnp.int32, (BLK, BLK), 1)
    diag_mask = col0 <= row0

    def rows_of(start, size, dil):
        return pl.ds(start, size, stride=dil) if dil > 1 else pl.ds(start, size)

    def ld(ref, start, size, dil):
        return ref[rows_of(start, size, dil), :].astype(BF16)

    chains = []
    for p_idx, (window, dil) in enumerate(DILATED_PATTERNS):
        assert window // dil == BLK
        for r in range(dil):
            for n in range(seq // dil // BLK):
                chains.append((p_idx, dil, r + dil * BLK * n, None if n == 0 else r + dil * BLK * (n - 1)))

    def scores(c, st):
        p_idx, dil, q0, k0 = chains[c]
        qb = ld(qf, q0, BLK, dil)
        if k0 is None:
            s = _dot_nt(qb, ld(kf, q0, BLK, dil)) + bias_ref[0, p_idx, :, BLK:]
            st["s"] = jnp.where(diag_mask, s, NEG_INF)
        else:
            s = _dot_nt(qb, ld(kf, k0, 2 * BLK, dil)) + bias_ref[0, p_idx]
            st["s"] = jnp.where(band_mask, s, NEG_INF)

    def softmax(c, st):
        s = st.pop("s")
        st["m"] = jnp.max(s, axis=-1, keepdims=True)
        p = jnp.exp(s - st["m"])
        st["l"] = jnp.sum(p, axis=-1, keepdims=True)
        st["p"] = p.astype(BF16)

    def values(c, st):
        p_idx, dil, q0, k0 = chains[c]
        vb = ld(vf, q0, BLK, dil) if k0 is None else ld(vf, k0, 2 * BLK, dil)
        rows = rows_of(q0, BLK, dil)
        acc_s[p_idx, rows, :] = _dot(st.pop("p"), vb)
        m_s[p_idx, rows, :] = jnp.broadcast_to(st.pop("m"), (BLK, HEAD_DIM))
        l_s[p_idx, rows, :] = jnp.broadcast_to(st.pop("l"), (BLK, HEAD_DIM))

    _staggered(len(chains), [scores] + [None] * (lag - 1) + [softmax] + [None] * (lag - 1) + [values])

    chunk = MERGE_ROWS

    def merge(ci, carry):
        rows = pl.ds(pl.multiple_of(ci * chunk, chunk), chunk)
        m0, m1, m2 = m_s[0, rows, :], m_s[1, rows, :], m_s[2, rows, :]
        mm = jnp.maximum(jnp.maximum(m0, m1), m2)
        w0, w1, w2 = jnp.exp(m0 - mm), jnp.exp(m1 - mm), jnp.exp(m2 - mm)
        num = w0 * acc_s[0, rows, :] + w1 * acc_s[1, rows, :] + w2 * acc_s[2, rows, :]
        den = w0 * l_s[0, rows, :] + w1 * l_s[1, rows, :] + w2 * l_s[2, rows, :]
        o_ref[rows, :] = (num / den).astype(o_ref.dtype)
        return carry

    lax.fori_loop(0, seq // chunk, merge, 0)


def _dilated_attention(proj, bias_a, *, batch, seq):
    t = proj.shape[0]
    n_pat = len(DILATED_PATTERNS)
    blk = lambda off: pl.BlockSpec((seq, HEAD_DIM), lambda b, h: (b, off + h))
    return pl.pallas_call(
        functools.partial(_dilated_kernel, seq=seq, lag=DILATED_LAG),
        grid=(batch, N_HEADS_A),
        in_specs=[blk(_QA), blk(_KA), blk(_VA),
                  pl.BlockSpec((1, n_pat, BLK, 2 * BLK), lambda b, h: (h, 0, 0, 0))],
        out_specs=pl.BlockSpec((seq, HEAD_DIM), lambda b, h: (b, h)),
        out_shape=jax.ShapeDtypeStruct((t, N_HEADS_A * HEAD_DIM), BF16),
        scratch_shapes=[pltpu.VMEM((seq, HEAD_DIM), F32)] * 3
                       + [pltpu.VMEM((n_pat, seq, HEAD_DIM), F32)] * 3,
        compiler_params=_cparams(("parallel", "parallel")),
        name="dilated_attn",
    )(proj, proj, proj, bias_a)


def _diff_kernel(q_ref, k_ref, v_ref, bias_ref, lam_ref, g_ref, o_ref, vt_ref, m_ref, acc_ref, *, tq, n_heads, seq,
                 lam_init):
    i = pl.program_id(1)
    half = HEAD_DIM // 2
    heads = [slice(g * HEAD_DIM, (g + 1) * HEAD_DIM) for g in range(n_heads)]

    @pl.when(i == 0)
    def _():
        r = lax.broadcasted_iota(jnp.int32, (HEAD_DIM, HEAD_DIM), 0)
        c = lax.broadcasted_iota(jnp.int32, (HEAD_DIM, HEAD_DIM), 1)
        eye = (r == c).astype(BF16)

        def transpose_block(jb, carry):
            rows = pl.ds(pl.multiple_of(jb * tq, tq), tq)
            for g, hs in enumerate(heads):
                vt_ref[g, jb, :HEAD_DIM] = _dot_nt(eye, v_ref[rows, hs]).astype(BF16)
                vt_ref[g, jb, HEAD_DIM:] = jnp.ones((ONES_ROWS, tq), BF16)
            return carry

        lax.fori_loop(0, seq // tq, transpose_block, 0)

    lane = lax.broadcasted_iota(jnp.int32, (tq, HEAD_DIM), 1)
    qqs = []
    for hs in heads:
        q = q_ref[:, hs]
        zero = jnp.zeros_like(q)
        qqs.append(jnp.concatenate([jnp.where(lane < half, q, zero), jnp.where(lane >= half, q, zero)], axis=0))
    key = lax.broadcasted_iota(jnp.int32, (tq, 2 * tq), 0)
    qry = lax.broadcasted_iota(jnp.int32, (tq, 2 * tq), 1)
    causal = key <= jnp.where(qry >= tq, qry - tq, qry)

    def step(js, diagonal_last=False):
        n_chains = len(js) * n_heads

        def scores(c, st):
            j, g = js[c // n_heads], c % n_heads
            rows = pl.ds(pl.multiple_of(j * tq, tq), tq)
            bias = bias_ref[g, i - j]
            s = _dot_nt(k_ref[rows, heads[g]], qqs[g]) + jnp.concatenate([bias, bias], axis=1)
            masked = diagonal_last and c // n_heads == len(js) - 1
            st["s"] = jnp.where(causal, s, NEG_INF) if masked else s

        def softmax(c, st):
            g = c % n_heads
            m = m_ref[g]
            s = st.pop("s")
            m_new = jnp.maximum(m, jnp.max(s, axis=0, keepdims=True))
            m_ref[g] = m_new
            st["alpha"] = jnp.exp(m - m_new)
            st["p"] = jnp.exp(s - m_new).astype(BF16)

        def values(c, st):
            g = c % n_heads
            vt = vt_ref[g, js[c // n_heads]]
            acc_ref[g] = st.pop("alpha") * acc_ref[g] + _dot(vt, st.pop("p"))

        _staggered(n_chains, [scores, softmax, values])

    m_ref[...] = jnp.full(m_ref.shape, NEG_INF, F32)
    acc_ref[...] = jnp.zeros(acc_ref.shape, F32)

    def triple(t, carry):
        step([3 * t, 3 * t + 1, 3 * t + 2])
        return carry

    lax.fori_loop(0, i // 3, triple, 0)

    @pl.when(i % 3 == 1)
    def _():
        step([i - 1])

    @pl.when(i % 3 == 2)
    def _():
        step([i - 2, i - 1])

    step([i], diagonal_last=True)

    lp = lam_ref[...]
    lam = (jnp.exp(jnp.sum(lp[0:1] * lp[1:2], axis=-1, keepdims=True))
           - jnp.exp(jnp.sum(lp[2:3] * lp[3:4], axis=-1, keepdims=True)) + lam_init)
    for g, hs in enumerate(heads):
        acc = acc_ref[g]
        o = acc[:HEAD_DIM] / acc[HEAD_DIM:HEAD_DIM + 1]
        out = o[:, :tq] - lam * o[:, tq:]
        ms = jnp.mean(out * out, axis=0, keepdims=True)
        y = out * lax.rsqrt(ms + EPS) * g_ref[...] * (1.0 - lam_init)
        o_ref[:, hs] = y.T.astype(o_ref.dtype)


def _diff_attention(proj, bias_b, lam_params, g, *, batch, seq, tq, lam_init):
    t = proj.shape[0]
    nq = seq // tq
    nh = N_HEADS_B
    width = nh * HEAD_DIM
    assert _QB % nh == 0 and _KB % nh == 0 and _VB % nh == 0
    kv = lambda off: pl.BlockSpec((seq, width), lambda b, i: (b, off // nh))
    return pl.pallas_call(
        functools.partial(_diff_kernel, tq=tq, n_heads=nh, seq=seq, lam_init=lam_init),
        grid=(batch, nq),
        in_specs=[pl.BlockSpec((tq, width), lambda b, i: (b * nq + i, _QB // nh)),
                  kv(_KB), kv(_VB),
                  pl.BlockSpec((nh, nq, tq, tq), lambda b, i: (0, 0, 0, 0)),
                  pl.BlockSpec((4, HEAD_DIM // 2), lambda b, i: (0, 0)),
                  pl.BlockSpec((HEAD_DIM, 1), lambda b, i: (0, 0))],
        out_specs=pl.BlockSpec((tq, width), lambda b, i: (b * nq + i, 0)),
        out_shape=jax.ShapeDtypeStruct((t, width), BF16),
        scratch_shapes=[pltpu.VMEM((nh, nq, HEAD_DIM + ONES_ROWS, tq), BF16),
                        pltpu.VMEM((nh, 1, 2 * tq), F32),
                        pltpu.VMEM((nh, HEAD_DIM + ONES_ROWS, 2 * tq), F32)],
        compiler_params=_cparams(("parallel", "arbitrary")),
        name="diff_attn",
    )(proj, proj, proj, bias_b, lam_params, g)


def _stick_kernel(q_ref, k_ref, v_ref, o_ref, c_ref, acc_ref, *, tq, n_heads):
    i = pl.program_id(2)
    row = lax.broadcasted_iota(jnp.int32, (tq, tq), 0)
    col = lax.broadcasted_iota(jnp.int32, (tq, tq), 1)
    strict = col < row
    suffix = (row >= col).astype(BF16)
    suffix2 = jnp.concatenate([suffix, suffix], axis=0)
    heads = [slice(g * HEAD_DIM, (g + 1) * HEAD_DIM) for g in range(n_heads)]
    qs = [q_ref[:, hs] for hs in heads]

    def blocks(js, diagonal_first=False):
        n_chains = len(js) * n_heads
        rows_of = [pl.ds(pl.multiple_of(j * tq, tq), tq) for j in js]

        def masked(c):
            return diagonal_first and c < n_heads

        def scores(c, st):
            st["z"] = _dot_nt(qs[c % n_heads], k_ref[rows_of[c // n_heads], heads[c % n_heads]])

        def log_break(c, st):
            z = st.pop("z")
            neg_abs = lax.bitcast_convert_type(
                lax.bitcast_convert_type(z, jnp.uint32) | jnp.uint32(F32_SIGN_BIT), F32)
            w = jnp.maximum(z, 0.0) + jnp.log(1.0 + jnp.exp(neg_abs))
            if masked(c):
                w = jnp.where(strict, w, 0.0)
            hi = w.astype(BF16)
            st["hilo"] = jnp.concatenate([hi, (w - hi.astype(F32)).astype(BF16)], axis=1)
            st["zc"] = z - c_ref[c % n_heads]

        def suffix_sums(c, st):
            st["incl"] = _dot(st.pop("hilo"), suffix2)
            c_ref[c % n_heads] += st["incl"][:, 0:1]

        def weights(c, st):
            a = jnp.exp(st.pop("zc") - st.pop("incl"))
            if masked(c):
                a = jnp.where(strict, a, 0.0)
            st["a"] = a.astype(BF16)

        def values(c, st):
            acc_ref[c % n_heads] += _dot(st.pop("a"), v_ref[rows_of[c // n_heads], heads[c % n_heads]])

        _staggered(n_chains, [scores, log_break, suffix_sums, weights, values])

    c_ref[...] = jnp.zeros(c_ref.shape, F32)
    acc_ref[...] = jnp.zeros(acc_ref.shape, F32)

    @pl.when(i == 0)
    def _():
        blocks([i], diagonal_first=True)

    @pl.when(i == 1)
    def _():
        blocks([i, i - 1], diagonal_first=True)

    @pl.when(i > 1)
    def _():
        blocks([i, i - 1, i - 2], diagonal_first=True)

        def triple(t, carry):
            blocks([i - 3 - 3 * t, i - 4 - 3 * t, i - 5 - 3 * t])
            return carry

        lax.fori_loop(0, (i - 2) // 3, triple, 0)

        @pl.when((i - 2) % 3 == 1)
        def _():
            blocks([0])

        @pl.when((i - 2) % 3 == 2)
        def _():
            blocks([1, 0])

    for g, hs in enumerate(heads):
        o_ref[:, hs] = acc_ref[g].astype(o_ref.dtype)


def _stick_attention(proj, *, batch, seq, tq, heads_per_step):
    t = proj.shape[0]
    nq = seq // tq
    g = heads_per_step
    assert g >= 2
    width = g * HEAD_DIM
    kv = lambda off: pl.BlockSpec((seq, width), lambda b, h, i: (b, off // g + h))
    return pl.pallas_call(
        functools.partial(_stick_kernel, tq=tq, n_heads=g),
        grid=(batch, N_HEADS_C // g, nq),
        in_specs=[pl.BlockSpec((tq, width), lambda b, h, i: (b * nq + i, _QC // g + h)),
                  kv(_KC), kv(_VC)],
        out_specs=pl.BlockSpec((tq, width), lambda b, h, i: (b * nq + i, h)),
        out_shape=jax.ShapeDtypeStruct((t, N_HEADS_C * HEAD_DIM), BF16),
        scratch_shapes=[pltpu.VMEM((g, tq, 1), F32), pltpu.VMEM((g, tq, HEAD_DIM), F32)],
        compiler_params=_cparams(("parallel", "parallel", "arbitrary")),
        name="stick_attn",
    )(proj, proj, proj)


def _out_proj_kernel(x_ref, a_ref, b_ref, c_ref, w_ref, o_ref):
    ka, kb = a_ref.shape[1], b_ref.shape[1]
    acc = (_dot(a_ref[...], w_ref[:ka].astype(BF16)) + _dot(b_ref[...], w_ref[ka:ka + kb].astype(BF16))
           + _dot(c_ref[...], w_ref[ka + kb:].astype(BF16)))
    o_ref[...] = x_ref[...] + acc


def _out_proj(x, ma, mb, mc, w, layer, *, tm=OUT_PROJ_TM):
    t, d = x.shape
    act = lambda k: pl.BlockSpec((tm, k), lambda i: (i, 0))
    return pl.pallas_call(
        _out_proj_kernel,
        grid=(t // tm,),
        in_specs=[act(d), act(ma.shape[1]), act(mb.shape[1]), act(mc.shape[1]),
                  pl.BlockSpec((None,) + w.shape[1:], lambda i: (layer, 0, 0))],
        out_specs=act(d),
        out_shape=jax.ShapeDtypeStruct((t, d), F32),
        compiler_params=_cparams(("parallel",)),
        name="out_proj",
    )(x, ma, mb, mc, w)


def _mlp_kernel(x_hbm, g_ref, w1_ref, w2_ref, gf_ref, o_ref, xn_ref, x_buf, x_sem, *, final_norm):
    i, f = pl.program_id(0), pl.program_id(1)
    n_i, n_f = pl.num_programs(0), pl.num_programs(1)
    tm = x_buf.shape[0]

    def x_copy(tile):
        return pltpu.make_async_copy(x_hbm.at[pl.ds(pl.multiple_of(tile * tm, tm), tm), :], x_buf, x_sem)

    @pl.when((i == 0) & (f == 0))
    def _():
        x_copy(0).start()

    def hidden_tile_update(xn):
        h = jnp.maximum(_dot(xn, w1_ref[...].astype(BF16)), 0.0)
        return _dot((h * h).astype(BF16), w2_ref[...].astype(BF16))

    @pl.when(f == 0)
    def _():
        x_copy(i).wait()
        x = x_buf[...]
        ms = jnp.mean(x * x, axis=-1, keepdims=True)
        xn = (x * lax.rsqrt(ms + EPS) * g_ref[...]).astype(BF16)
        xn_ref[...] = xn
        o_ref[...] = x + hidden_tile_update(xn)

    @pl.when(f > 0)
    def _():
        o_ref[...] += hidden_tile_update(xn_ref[...])

    @pl.when((f == n_f // 2) & (i + 1 < n_i))
    def _():
        x_copy(i + 1).start()

    if final_norm:
        @pl.when(f == pl.num_programs(1) - 1)
        def _():
            y = o_ref[...]
            ms = jnp.mean(y * y, axis=-1, keepdims=True)
            o_ref[...] = y * lax.rsqrt(ms + EPS) * gf_ref[...]


def _mlp(x, g, w1, w2, layer, g_final, *, final_norm, tm=MLP_TM, tf=MLP_TF):
    t, d = x.shape
    dff = w1.shape[2]
    return pl.pallas_call(
        functools.partial(_mlp_kernel, final_norm=final_norm),
        grid=(t // tm, dff // tf),
        in_specs=[pl.BlockSpec(memory_space=pl.ANY),
                  pl.BlockSpec((1, d), lambda i, f: (0, 0)),
                  pl.BlockSpec((None, d, tf), lambda i, f: (layer, 0, f)),
                  pl.BlockSpec((None, tf, d), lambda i, f: (layer, f, 0)),
                  pl.BlockSpec((1, d), lambda i, f: (0, 0))],
        out_specs=pl.BlockSpec((tm, d), lambda i, f: (i, 0)),
        out_shape=jax.ShapeDtypeStruct((t, d), F32),
        scratch_shapes=[pltpu.VMEM((tm, d), BF16), pltpu.VMEM((tm, d), x.dtype),
                        pltpu.SemaphoreType.DMA(())],
        compiler_params=_cparams(("arbitrary", "arbitrary")),
        name="mlp",
    )(x, g, w1, w2, g_final)


def kernel(x, w_in, w_out, g_attn, g_mlp, w_mlp_in, w_mlp_out, rel_bias_table,
           diff_lam_q1, diff_lam_k1, diff_lam_q2, diff_lam_k2, diff_subln_g, g_final):
    batch, seq, d_model = x.shape
    depth = w_in.shape[0]

    i = np.arange(BLK)[:, None]
    c = np.arange(2 * BLK)[None, :]
    steps = i + BLK - c
    buckets_a = np.stack([_t5_bucket_np(steps * dil) for _, dil in DILATED_PATTERNS])
    dist = (np.arange(seq // TQ_DIFF)[:, None, None] * TQ_DIFF + np.arange(TQ_DIFF)[None, None, :]
            - np.arange(TQ_DIFF)[None, :, None])
    buckets_b = _t5_bucket_np(dist)
    bias_a = _build_bias(rel_bias_table[:, :N_HEADS_A], buckets_a)
    bias_b = _build_bias(rel_bias_table[:, N_HEADS_A:], buckets_b)

    colscale = np.ones((1, w_in.shape[2]), np.float32)
    colscale[0, _QA * HEAD_DIM:_KA * HEAD_DIM] = 1.0 / math.sqrt(HEAD_DIM)
    colscale[0, _QB * HEAD_DIM:_KB * HEAD_DIM] = 1.0 / math.sqrt(HEAD_DIM // 2)
    colscale[0, _QC * HEAD_DIM:_KC * HEAD_DIM] = 1.0 / math.sqrt(HEAD_DIM)
    colscale = jnp.asarray(colscale)

    tile = N_HEADS_A * HEAD_DIM
    assert N_HEADS_C == N_HEADS_A and (3 * N_HEADS_B * HEAD_DIM) % tile == 0
    tiles_a, tiles_b = 3, 3 * N_HEADS_B * HEAD_DIM // tile
    src_tiles = (tuple(range(tiles_a)) + tuple(range(tiles_a + tiles_b, 2 * tiles_a + tiles_b))
                 + tuple(range(tiles_a, tiles_a + tiles_b)))

    xf = x.reshape(batch * seq, d_model)
    for l in range(depth):
        lam_init = 0.8 - 0.6 * math.exp(-0.3 * l)
        lam_params = jnp.stack([diff_lam_q1[l], diff_lam_k1[l], diff_lam_q2[l], diff_lam_k2[l]]).astype(F32)
        proj = _rms_proj(xf, g_attn[l][None, :], w_in, l, colscale, src_tiles)
        ma = _dilated_attention(proj, bias_a, batch=batch, seq=seq)
        mb = _diff_attention(proj, bias_b, lam_params, diff_subln_g[l][:, None],
                             batch=batch, seq=seq, tq=TQ_DIFF, lam_init=lam_init)
        mc = _stick_attention(proj, batch=batch, seq=seq, tq=TQ_STICK, heads_per_step=STICK_HEADS_PER_STEP)
        xf = _out_proj(xf, ma, mb, mc, w_out, l)
        xf = _mlp(xf, g_mlp[l][None, :], w_mlp_in, w_mlp_out, l, g_final[None, :], final_norm=(l == depth - 1))
    return xf.reshape(batch, seq, d_model)
```

```python
import functools
import math

import jax
import jax.numpy as jnp
import numpy as np
from jax import lax
from jax.experimental import pallas as pl
from jax.experimental.pallas import tpu as pltpu

HEAD_DIM = 128
N_HEADS_A = 6
N_HEADS_B = 4
N_HEADS_C = 6
DILATED_PATTERNS = ((128, 1), (512, 4), (2048, 16))
BLK = 128
NUM_BUCKETS = 32
MAX_DISTANCE = 2048
EPS = 1e-6
NEG_INF = -1e30

F32 = jnp.float32
BF16 = jnp.bfloat16

V7X_VMEM_BYTES = 64 * 1024 * 1024
BF16_SUBLANE_TILE = 16
F32_SIGN_BIT = 0x80000000

ONES_ROWS = BF16_SUBLANE_TILE

_VMEM_LIMIT = V7X_VMEM_BYTES * 7 // 8
PROJ_TM = 1024
PROJ_WEIGHT_SLOTS = 3
OUT_PROJ_TM = 512
MLP_TM, MLP_TF = 1024, 512
MLP_X_CHUNKS = 8
MLP_WEIGHT_SLOTS = 3
TQ_DIFF = 256
TQ_STICK = 256
STICK_HEADS_PER_STEP = 6
DILATED_LAG = 3
MERGE_ROWS = 256

_QA, _KA, _VA = 0, N_HEADS_A, 2 * N_HEADS_A
_QC = 3 * N_HEADS_A
_KC, _VC = _QC + N_HEADS_C, _QC + 2 * N_HEADS_C
_QB = _QC + 3 * N_HEADS_C
_KB, _VB = _QB + N_HEADS_B, _QB + 2 * N_HEADS_B


def _cparams(sem):
    return pltpu.CompilerParams(dimension_semantics=sem, vmem_limit_bytes=_VMEM_LIMIT)


def _dot_nt(a, b):
    return lax.dot_general(a, b, (((1,), (1,)), ((), ())), preferred_element_type=F32)


def _dot(a, b):
    return jnp.dot(a, b, preferred_element_type=F32)


def _staggered(n_chains, stages, states=None):
    states = [dict() for _ in range(n_chains)] if states is None else states
    for t in range(n_chains + len(stages) - 1):
        for k, stage in enumerate(stages):
            c = t - k
            if stage is not None and 0 <= c < n_chains:
                stage(c, states[c])
    return states


def _t5_bucket_np(dist):
    n = np.maximum(dist, 0)
    max_exact = NUM_BUCKETS // 2
    nf = np.maximum(n, max_exact).astype(np.float32)
    large = max_exact + (np.log(nf / np.float32(max_exact)) / np.float32(math.log(MAX_DISTANCE / max_exact))
                         * np.float32(NUM_BUCKETS - max_exact)).astype(np.int32)
    large = np.minimum(large, NUM_BUCKETS - 1)
    return np.where(n < max_exact, n, large).astype(np.int32)


def _bias_kernel(tab_ref, bkt_ref, o_ref, *, n_heads, tile_buckets):
    for t, present in enumerate(tile_buckets):
        b = bkt_ref[t]
        for h in range(n_heads):
            acc = jnp.full(b.shape, tab_ref[present[0], h], F32)
            for k in present[1:]:
                acc = jnp.where(b == k, tab_ref[k, h], acc)
            o_ref[h, t] = acc


def _build_bias(table, buckets):
    n_heads = table.shape[1]
    n, r, c = buckets.shape
    tile_buckets = tuple(tuple(int(k) for k in np.unique(buckets[t])) for t in range(n))
    return pl.pallas_call(
        functools.partial(_bias_kernel, n_heads=n_heads, tile_buckets=tile_buckets),
        in_specs=[pl.BlockSpec(memory_space=pltpu.SMEM),
                  pl.BlockSpec(memory_space=pltpu.VMEM)],
        out_specs=pl.BlockSpec(memory_space=pltpu.VMEM),
        out_shape=jax.ShapeDtypeStruct((n_heads, n, r, c), F32),
        compiler_params=pltpu.CompilerParams(vmem_limit_bytes=_VMEM_LIMIT),
        name="bias_table",
    )(table, jnp.asarray(buckets))


def _rms_proj_kernel(x_ref, g_ref, w_hbm, cs_ref, o_ref, xn_ref, w_ring, sems, *, layer, src_tiles):
    nj, depth = len(src_tiles), w_ring.shape[0]
    tn = w_ring.shape[2]
    step = pl.program_id(0) * nj + pl.program_id(1)
    n_steps = pl.num_programs(0) * nj

    def tile_copy(s):
        j = s % nj
        src = src_tiles[-1]
        for k in reversed(range(nj - 1)):
            src = jnp.where(j == k, src_tiles[k], src)
        slot = s % depth
        cols = pl.ds(pl.multiple_of(src * tn, tn), tn)
        return pltpu.make_async_copy(w_hbm.at[layer, :, cols], w_ring.at[slot], sems.at[slot])

    @pl.when(step == 0)
    def _():
        for s in range(depth - 1):
            tile_copy(s).start()

    @pl.when(step + depth - 1 < n_steps)
    def _():
        tile_copy(step + depth - 1).start()

    tile_copy(step).wait()

    def project(xn):
        acc = _dot(xn, w_ring[step % depth].astype(BF16))
        o_ref[...] = (acc * cs_ref[...]).astype(o_ref.dtype)

    @pl.when(pl.program_id(1) == 0)
    def _():
        x = x_ref[...]
        ms = jnp.mean(x * x, axis=-1, keepdims=True)
        xn = (x * lax.rsqrt(ms + EPS) * g_ref[...]).astype(BF16)
        xn_ref[...] = xn
        project(xn)

    @pl.when(pl.program_id(1) > 0)
    def _():
        project(xn_ref[...])


def _rms_proj(x, g, w, layer, colscale, src_tiles, *, tm=PROJ_TM):
    t, d = x.shape
    n = w.shape[2]
    tn = n // len(src_tiles)
    return pl.pallas_call(
        functools.partial(_rms_proj_kernel, layer=layer, src_tiles=src_tiles),
        grid=(t // tm, n // tn),
        in_specs=[pl.BlockSpec((tm, d), lambda i, j: (i, 0)),
                  pl.BlockSpec((1, d), lambda i, j: (0, 0)),
                  pl.BlockSpec(memory_space=pl.ANY),
                  pl.BlockSpec((1, tn), lambda i, j: (0, j))],
        out_specs=pl.BlockSpec((tm, tn), lambda i, j: (i, j)),
        out_shape=jax.ShapeDtypeStruct((t, n), BF16),
        scratch_shapes=[pltpu.VMEM((tm, d), BF16),
                        pltpu.VMEM((PROJ_WEIGHT_SLOTS, d, tn), w.dtype),
                        pltpu.SemaphoreType.DMA((PROJ_WEIGHT_SLOTS,))],
        compiler_params=_cparams(("arbitrary", "arbitrary")),
        name="rms_proj",
    )(x, g, w, colscale)


def _dilated_kernel(q_ref, k_ref, v_ref, bias_ref, o_ref, qf, kf, vf, acc_s, m_s, l_s, *, seq, lag):
    qf[...] = q_ref[...].astype(F32)
    kf[...] = k_ref[...].astype(F32)
    vf[...] = v_ref[...].astype(F32)

    row = lax.broadcasted_iota(jnp.int32, (BLK, 2 * BLK), 0)
    col = lax.broadcasted_iota(jnp.int32, (BLK, 2 * BLK), 1)
    band_mask = (col >= row) & (col <= row + BLK)
    row0 = lax.broadcasted_iota(jnp.int32, (BLK, BLK), 0)
    col0 = lax.broadcasted_iota(jnp.int32, (BLK, BLK), 1)
    diag_mask = col0 <= row0

    def rows_of(start, size, dil):
        return pl.ds(start, size, stride=dil) if dil > 1 else pl.ds(start, size)

    def ld(ref, start, size, dil):
        return ref[rows_of(start, size, dil), :].astype(BF16)

    chains = []
    for p_idx, (window, dil) in enumerate(DILATED_PATTERNS):
        assert window // dil == BLK
        for r in range(dil):
            for n in range(seq // dil // BLK):
                chains.append((p_idx, dil, r + dil * BLK * n, None if n == 0 else r + dil * BLK * (n - 1)))

    def scores(c, st):
        p_idx, dil, q0, k0 = chains[c]
        qb = ld(qf, q0, BLK, dil)
        if k0 is None:
            s = _dot_nt(qb, ld(kf, q0, BLK, dil)) + bias_ref[0, p_idx, :, BLK:]
            st["s"] = jnp.where(diag_mask, s, NEG_INF)
        else:
            s = _dot_nt(qb, ld(kf, k0, 2 * BLK, dil)) + bias_ref[0, p_idx]
            st["s"] = jnp.where(band_mask, s, NEG_INF)

    def softmax(c, st):
        s = st.pop("s")
        st["m"] = jnp.max(s, axis=-1, keepdims=True)
        p = jnp.exp(s - st["m"])
        st["l"] = jnp.sum(p, axis=-1, keepdims=True)
        st["p"] = p.astype(BF16)

    def values(c, st):
        p_idx, dil, q0, k0 = chains[c]
        vb = ld(vf, q0, BLK, dil) if k0 is None else ld(vf, k0, 2 * BLK, dil)
        rows = rows_of(q0, BLK, dil)
        acc_s[p_idx, rows, :] = _dot(st.pop("p"), vb)
        m_s[p_idx, rows, :] = jnp.broadcast_to(st.pop("m"), (BLK, HEAD_DIM))
        l_s[p_idx, rows, :] = jnp.broadcast_to(st.pop("l"), (BLK, HEAD_DIM))

    _staggered(len(chains), [scores] + [None] * (lag - 1) + [softmax] + [None] * (lag - 1) + [values])

    chunk = MERGE_ROWS

    def merge(ci, carry):
        rows = pl.ds(pl.multiple_of(ci * chunk, chunk), chunk)
        m0, m1, m2 = m_s[0, rows, :], m_s[1, rows, :], m_s[2, rows, :]
        mm = jnp.maximum(jnp.maximum(m0, m1), m2)
        w0, w1, w2 = jnp.exp(m0 - mm), jnp.exp(m1 - mm), jnp.exp(m2 - mm)
        num = w0 * acc_s[0, rows, :] + w1 * acc_s[1, rows, :] + w2 * acc_s[2, rows, :]
        den = w0 * l_s[0, rows, :] + w1 * l_s[1, rows, :] + w2 * l_s[2, rows, :]
        o_ref[rows, :] = (num / den).astype(o_ref.dtype)
        return carry

    lax.fori_loop(0, seq // chunk, merge, 0)


def _dilated_attention(proj, bias_a, *, batch, seq):
    t = proj.shape[0]
    n_pat = len(DILATED_PATTERNS)
    blk = lambda off: pl.BlockSpec((seq, HEAD_DIM), lambda b, h: (b, off + h))
    return pl.pallas_call(
        functools.partial(_dilated_kernel, seq=seq, lag=DILATED_LAG),
        grid=(batch, N_HEADS_A),
        in_specs=[blk(_QA), blk(_KA), blk(_VA),
                  pl.BlockSpec((1, n_pat, BLK, 2 * BLK), lambda b, h: (h, 0, 0, 0))],
        out_specs=pl.BlockSpec((seq, HEAD_DIM), lambda b, h: (b, h)),
        out_shape=jax.ShapeDtypeStruct((t, N_HEADS_A * HEAD_DIM), BF16),
        scratch_shapes=[pltpu.VMEM((seq, HEAD_DIM), F32)] * 3
                       + [pltpu.VMEM((n_pat, seq, HEAD_DIM), F32)] * 3,
        compiler_params=_cparams(("parallel", "parallel")),
        name="dilated_attn",
    )(proj, proj, proj, bias_a)


def _diff_kernel(q_ref, k_ref, v_ref, bias_ref, lam_ref, g_ref, o_ref, vt_ref, m_ref, acc_ref, *, tq, n_heads, seq,
                 lam_init):
    i = pl.program_id(1)
    half = HEAD_DIM // 2
    heads = [slice(g * HEAD_DIM, (g + 1) * HEAD_DIM) for g in range(n_heads)]

    @pl.when(i == 0)
    def _():
        r = lax.broadcasted_iota(jnp.int32, (HEAD_DIM, HEAD_DIM), 0)
        c = lax.broadcasted_iota(jnp.int32, (HEAD_DIM, HEAD_DIM), 1)
        eye = (r == c).astype(BF16)

        def transpose_block(jb, carry):
            rows = pl.ds(pl.multiple_of(jb * tq, tq), tq)
            for g, hs in enumerate(heads):
                vt_ref[g, jb, :HEAD_DIM] = _dot_nt(eye, v_ref[rows, hs]).astype(BF16)
                vt_ref[g, jb, HEAD_DIM:] = jnp.ones((ONES_ROWS, tq), BF16)
            return carry

        lax.fori_loop(0, seq // tq, transpose_block, 0)

    lane = lax.broadcasted_iota(jnp.int32, (tq, HEAD_DIM), 1)
    qqs = []
    for hs in heads:
        q = q_ref[:, hs]
        zero = jnp.zeros_like(q)
        qqs.append(jnp.concatenate([jnp.where(lane < half, q, zero), jnp.where(lane >= half, q, zero)], axis=0))
    key = lax.broadcasted_iota(jnp.int32, (tq, 2 * tq), 0)
    qry = lax.broadcasted_iota(jnp.int32, (tq, 2 * tq), 1)
    causal = key <= jnp.where(qry >= tq, qry - tq, qry)

    def step(js, diagonal_last=False):
        n_chains = len(js) * n_heads

        def scores(c, st):
            j, g = js[c // n_heads], c % n_heads
            rows = pl.ds(pl.multiple_of(j * tq, tq), tq)
            bias = bias_ref[g, i - j]
            s = _dot_nt(k_ref[rows, heads[g]], qqs[g]) + jnp.concatenate([bias, bias], axis=1)
            masked = diagonal_last and c // n_heads == len(js) - 1
            st["s"] = jnp.where(causal, s, NEG_INF) if masked else s

        def softmax(c, st):
            g = c % n_heads
            m = m_ref[g]
            s = st.pop("s")
            m_new = jnp.maximum(m, jnp.max(s, axis=0, keepdims=True))
            m_ref[g] = m_new
            st["alpha"] = jnp.exp(m - m_new)
            st["p"] = jnp.exp(s - m_new).astype(BF16)

        def values(c, st):
            g = c % n_heads
            vt = vt_ref[g, js[c // n_heads]]
            acc_ref[g] = st.pop("alpha") * acc_ref[g] + _dot(vt, st.pop("p"))

        _staggered(n_chains, [scores, softmax, values])

    m_ref[...] = jnp.full(m_ref.shape, NEG_INF, F32)
    acc_ref[...] = jnp.zeros(acc_ref.shape, F32)

    def triple(t, carry):
        step([3 * t, 3 * t + 1, 3 * t + 2])
        return carry

    lax.fori_loop(0, i // 3, triple, 0)

    @pl.when(i % 3 == 1)
    def _():
        step([i - 1])

    @pl.when(i % 3 == 2)
    def _():
        step([i - 2, i - 1])

    step([i], diagonal_last=True)

    lp = lam_ref[...]
    lam = (jnp.exp(jnp.sum(lp[0:1] * lp[1:2], axis=-1, keepdims=True))
           - jnp.exp(jnp.sum(lp[2:3] * lp[3:4], axis=-1, keepdims=True)) + lam_init)
    for g, hs in enumerate(heads):
        acc = acc_ref[g]
        o = acc[:HEAD_DIM] / acc[HEAD_DIM:HEAD_DIM + 1]
        out = o[:, :tq] - lam * o[:, tq:]
        ms = jnp.mean(out * out, axis=0, keepdims=True)
        y = out * lax.rsqrt(ms + EPS) * g_ref[...] * (1.0 - lam_init)
        o_ref[:, hs] = y.T.astype(o_ref.dtype)


def _diff_attention(proj, bias_b, lam_params, g, *, batch, seq, tq, lam_init):
    t = proj.shape[0]
    nq = seq // tq
    nh = N_HEADS_B
    width = nh * HEAD_DIM
    assert _QB % nh == 0 and _KB % nh == 0 and _VB % nh == 0
    kv = lambda off: pl.BlockSpec((seq, width), lambda b, i: (b, off // nh))
    return pl.pallas_call(
        functools.partial(_diff_kernel, tq=tq, n_heads=nh, seq=seq, lam_init=lam_init),
        grid=(batch, nq),
        in_specs=[pl.BlockSpec((tq, width), lambda b, i: (b * nq + i, _QB // nh)),
                  kv(_KB), kv(_VB),
                  pl.BlockSpec((nh, nq, tq, tq), lambda b, i: (0, 0, 0, 0)),
                  pl.BlockSpec((4, HEAD_DIM // 2), lambda b, i: (0, 0)),
                  pl.BlockSpec((HEAD_DIM, 1), lambda b, i: (0, 0))],
        out_specs=pl.BlockSpec((tq, width), lambda b, i: (b * nq + i, 0)),
        out_shape=jax.ShapeDtypeStruct((t, width), BF16),
        scratch_shapes=[pltpu.VMEM((nh, nq, HEAD_DIM + ONES_ROWS, tq), BF16),
                        pltpu.VMEM((nh, 1, 2 * tq), F32),
                        pltpu.VMEM((nh, HEAD_DIM + ONES_ROWS, 2 * tq), F32)],
        compiler_params=_cparams(("parallel", "arbitrary")),
        name="diff_attn",
    )(proj, proj, proj, bias_b, lam_params, g)


def _stick_kernel(q_ref, k_ref, v_ref, o_ref, c_ref, acc_ref, *, tq, n_heads):
    i = pl.program_id(2)
    row = lax.broadcasted_iota(jnp.int32, (tq, tq), 0)
    col = lax.broadcasted_iota(jnp.int32, (tq, tq), 1)
    strict = col < row
    suffix = (row >= col).astype(BF16)
    suffix2 = jnp.concatenate([suffix, suffix], axis=0)
    heads = [slice(g * HEAD_DIM, (g + 1) * HEAD_DIM) for g in range(n_heads)]
    qs = [q_ref[:, hs] for hs in heads]

    def blocks(js, diagonal_first=False):
        n_chains = len(js) * n_heads
        rows_of = [pl.ds(pl.multiple_of(j * tq, tq), tq) for j in js]

        def masked(c):
            return diagonal_first and c < n_heads

        def scores(c, st):
            st["z"] = _dot_nt(qs[c % n_heads], k_ref[rows_of[c // n_heads], heads[c % n_heads]])

        def log_break(c, st):
            z = st.pop("z")
            neg_abs = lax.bitcast_convert_type(
                lax.bitcast_convert_type(z, jnp.uint32) | jnp.uint32(F32_SIGN_BIT), F32)
            w = jnp.maximum(z, 0.0) + jnp.log(1.0 + jnp.exp(neg_abs))
            if masked(c):
                w = jnp.where(strict, w, 0.0)
            hi = w.astype(BF16)
            st["hilo"] = jnp.concatenate([hi, (w - hi.astype(F32)).astype(BF16)], axis=1)
            st["zc"] = z - c_ref[c % n_heads]

        def suffix_sums(c, st):
            st["incl"] = _dot(st.pop("hilo"), suffix2)
            c_ref[c % n_heads] += st["incl"][:, 0:1]

        def weights(c, st):
            a = jnp.exp(st.pop("zc") - st.pop("incl"))
            if masked(c):
                a = jnp.where(strict, a, 0.0)
            st["a"] = a.astype(BF16)

        def values(c, st):
            acc_ref[c % n_heads] += _dot(st.pop("a"), v_ref[rows_of[c // n_heads], heads[c % n_heads]])

        _staggered(n_chains, [scores, log_break, suffix_sums, weights, values])

    c_ref[...] = jnp.zeros(c_ref.shape, F32)
    acc_ref[...] = jnp.zeros(acc_ref.shape, F32)

    @pl.when(i == 0)
    def _():
        blocks([i], diagonal_first=True)

    @pl.when(i == 1)
    def _():
        blocks([i, i - 1], diagonal_first=True)

    @pl.when(i > 1)
    def _():
        blocks([i, i - 1, i - 2], diagonal_first=True)

        def triple(t, carry):
            blocks([i - 3 - 3 * t, i - 4 - 3 * t, i - 5 - 3 * t])
            return carry

        lax.fori_loop(0, (i - 2) // 3, triple, 0)

        @pl.when((i - 2) % 3 == 1)
        def _():
            blocks([0])

        @pl.when((i - 2) % 3 == 2)
        def _():
            blocks([1, 0])

    for g, hs in enumerate(heads):
        o_ref[:, hs] = acc_ref[g].astype(o_ref.dtype)


def _stick_attention(proj, *, batch, seq, tq, heads_per_step):
    t = proj.shape[0]
    nq = seq // tq
    g = heads_per_step
    assert g >= 2
    width = g * HEAD_DIM
    kv = lambda off: pl.BlockSpec((seq, width), lambda b, h, i: (b, off // g + h))
    return pl.pallas_call(
        functools.partial(_stick_kernel, tq=tq, n_heads=g),
        grid=(batch, N_HEADS_C // g, nq),
        in_specs=[pl.BlockSpec((tq, width), lambda b, h, i: (b * nq + i, _QC // g + h)),
                  kv(_KC), kv(_VC)],
        out_specs=pl.BlockSpec((tq, width), lambda b, h, i: (b * nq + i, h)),
        out_shape=jax.ShapeDtypeStruct((t, N_HEADS_C * HEAD_DIM), BF16),
        scratch_shapes=[pltpu.VMEM((g, tq, 1), F32), pltpu.VMEM((g, tq, HEAD_DIM), F32)],
        compiler_params=_cparams(("parallel", "parallel", "arbitrary")),
        name="stick_attn",
    )(proj, proj, proj)


def _out_proj_kernel(x_ref, a_ref, b_ref, c_ref, w_ref, o_ref):
    ka, kb = a_ref.shape[1], b_ref.shape[1]
    acc = (_dot(a_ref[...], w_ref[:ka].astype(BF16)) + _dot(b_ref[...], w_ref[ka:ka + kb].astype(BF16))
           + _dot(c_ref[...], w_ref[ka + kb:].astype(BF16)))
    o_ref[...] = x_ref[...] + acc


def _out_proj(x, ma, mb, mc, w, layer, *, tm=OUT_PROJ_TM):
    t, d = x.shape
    act = lambda k: pl.BlockSpec((tm, k), lambda i: (i, 0))
    return pl.pallas_call(
        _out_proj_kernel,
        grid=(t // tm,),
        in_specs=[act(d), act(ma.shape[1]), act(mb.shape[1]), act(mc.shape[1]),
                  pl.BlockSpec((None,) + w.shape[1:], lambda i: (layer, 0, 0))],
        out_specs=act(d),
        out_shape=jax.ShapeDtypeStruct((t, d), F32),
        compiler_params=_cparams(("parallel",)),
        name="out_proj",
    )(x, ma, mb, mc, w)


def _mlp_kernel(x_hbm, g_ref, w1_hbm, w2_hbm, gf_ref, o_ref, xn_ref, x_buf, x_sem, w1_ring, w2_ring, w_sems, *,
                layer, final_norm):
    i, f = pl.program_id(0), pl.program_id(1)
    n_i, n_f = pl.num_programs(0), pl.num_programs(1)
    tm = x_buf.shape[0]
    rows = tm // MLP_X_CHUNKS
    depth, tf = w1_ring.shape[0], w1_ring.shape[2]
    step, n_steps = i * n_f + f, n_i * n_f

    def w_copies(s):
        cols = pl.ds(pl.multiple_of((s % n_f) * tf, tf), tf)
        slot = s % depth
        return (pltpu.make_async_copy(w1_hbm.at[layer, :, cols], w1_ring.at[slot], w_sems.at[0, slot]),
                pltpu.make_async_copy(w2_hbm.at[layer, cols, :], w2_ring.at[slot], w_sems.at[1, slot]))

    @pl.when(step == 0)
    def _():
        for s in range(depth - 1):
            for copy in w_copies(s):
                copy.start()

    @pl.when(step + depth - 1 < n_steps)
    def _():
        for copy in w_copies(step + depth - 1):
            copy.start()

    for copy in w_copies(step):
        copy.wait()
    w1_ref, w2_ref = w1_ring.at[step % depth], w2_ring.at[step % depth]

    def x_copy(tile, chunk):
        src = pl.ds(pl.multiple_of(tile * tm + chunk * rows, rows), rows)
        dst = pl.ds(pl.multiple_of(chunk * rows, rows), rows)
        return pltpu.make_async_copy(x_hbm.at[src, :], x_buf.at[dst, :], x_sem)

    @pl.when((i == 0) & (f == 0))
    def _():
        for chunk in range(MLP_X_CHUNKS):
            x_copy(0, chunk).start()

    def hidden_tile_update(xn):
        h = jnp.maximum(_dot(xn, w1_ref[...].astype(BF16)), 0.0)
        return _dot((h * h).astype(BF16), w2_ref[...].astype(BF16))

    @pl.when(f == 0)
    def _():
        for chunk in range(MLP_X_CHUNKS):
            x_copy(i, chunk).wait()
        x = x_buf[...]
        ms = jnp.mean(x * x, axis=-1, keepdims=True)
        xn = (x * lax.rsqrt(ms + EPS) * g_ref[...]).astype(BF16)
        xn_ref[...] = xn
        o_ref[...] = x + hidden_tile_update(xn)

    @pl.when(f > 0)
    def _():
        o_ref[...] += hidden_tile_update(xn_ref[...])

    first = n_f - MLP_X_CHUNKS - 1

    @pl.when((f >= first) & (f < first + MLP_X_CHUNKS) & (i + 1 < n_i))
    def _():
        x_copy(i + 1, f - first).start()

    if final_norm:
        @pl.when(f == pl.num_programs(1) - 1)
        def _():
            y = o_ref[...]
            ms = jnp.mean(y * y, axis=-1, keepdims=True)
            o_ref[...] = y * lax.rsqrt(ms + EPS) * gf_ref[...]


def _mlp(x, g, w1, w2, layer, g_final, *, final_norm, tm=MLP_TM, tf=MLP_TF):
    t, d = x.shape
    dff = w1.shape[2]
    assert dff // tf >= MLP_X_CHUNKS + 2
    return pl.pallas_call(
        functools.partial(_mlp_kernel, layer=layer, final_norm=final_norm),
        grid=(t // tm, dff // tf),
        in_specs=[pl.BlockSpec(memory_space=pl.ANY),
                  pl.BlockSpec((1, d), lambda i, f: (0, 0)),
                  pl.BlockSpec(memory_space=pl.ANY),
                  pl.BlockSpec(memory_space=pl.ANY),
                  pl.BlockSpec((1, d), lambda i, f: (0, 0))],
        out_specs=pl.BlockSpec((tm, d), lambda i, f: (i, 0)),
        out_shape=jax.ShapeDtypeStruct((t, d), F32),
        scratch_shapes=[pltpu.VMEM((tm, d), BF16), pltpu.VMEM((tm, d), x.dtype),
                        pltpu.SemaphoreType.DMA(()),
                        pltpu.VMEM((MLP_WEIGHT_SLOTS, d, tf), w1.dtype),
                        pltpu.VMEM((MLP_WEIGHT_SLOTS, tf, d), w2.dtype),
                        pltpu.SemaphoreType.DMA((2, MLP_WEIGHT_SLOTS))],
        compiler_params=_cparams(("arbitrary", "arbitrary")),
        name="mlp",
    )(x, g, w1, w2, g_final)


def kernel(x, w_in, w_out, g_attn, g_mlp, w_mlp_in, w_mlp_out, rel_bias_table,
           diff_lam_q1, diff_lam_k1, diff_lam_q2, diff_lam_k2, diff_subln_g, g_final):
    batch, seq, d_model = x.shape
    depth = w_in.shape[0]

    i = np.arange(BLK)[:, None]
    c = np.arange(2 * BLK)[None, :]
    steps = i + BLK - c
    buckets_a = np.stack([_t5_bucket_np(steps * dil) for _, dil in DILATED_PATTERNS])
    dist = (np.arange(seq // TQ_DIFF)[:, None, None] * TQ_DIFF + np.arange(TQ_DIFF)[None, None, :]
            - np.arange(TQ_DIFF)[None, :, None])
    buckets_b = _t5_bucket_np(dist)
    bias_a = _build_bias(rel_bias_table[:, :N_HEADS_A], buckets_a)
    bias_b = _build_bias(rel_bias_table[:, N_HEADS_A:], buckets_b)

    colscale = np.ones((1, w_in.shape[2]), np.float32)
    colscale[0, _QA * HEAD_DIM:_KA * HEAD_DIM] = 1.0 / math.sqrt(HEAD_DIM)
    colscale[0, _QB * HEAD_DIM:_KB * HEAD_DIM] = 1.0 / math.sqrt(HEAD_DIM // 2)
    colscale[0, _QC * HEAD_DIM:_KC * HEAD_DIM] = 1.0 / math.sqrt(HEAD_DIM)
    colscale = jnp.asarray(colscale)

    tile = N_HEADS_A * HEAD_DIM
    assert N_HEADS_C == N_HEADS_A and (3 * N_HEADS_B * HEAD_DIM) % tile == 0
    tiles_a, tiles_b = 3, 3 * N_HEADS_B * HEAD_DIM // tile
    src_tiles = (tuple(range(tiles_a)) + tuple(range(tiles_a + tiles_b, 2 * tiles_a + tiles_b))
                 + tuple(range(tiles_a, tiles_a + tiles_b)))

    xf = x.reshape(batch * seq, d_model)
    for l in range(depth):
        lam_init = 0.8 - 0.6 * math.exp(-0.3 * l)
        lam_params = jnp.stack([diff_lam_q1[l], diff_lam_k1[l], diff_lam_q2[l], diff_lam_k2[l]]).astype(F32)
        proj = _rms_proj(xf, g_attn[l][None, :], w_in, l, colscale, src_tiles)
        ma = _dilated_attention(proj, bias_a, batch=batch, seq=seq)
        mb = _diff_attention(proj, bias_b, lam_params, diff_subln_g[l][:, None],
                             batch=batch, seq=seq, tq=TQ_DIFF, lam_init=lam_init)
        mc = _stick_attention(proj, batch=batch, seq=seq, tq=TQ_STICK, heads_per_step=STICK_HEADS_PER_STEP)
        xf = _out_proj(xf, ma, mb, mc, w_out, l)
        xf = _mlp(xf, g_mlp[l][None, :], w_mlp_in, w_mlp_out, l, g_final[None, :], final_norm=(l == depth - 1))
    return xf.reshape(batch, seq, d_model)
```

```python
import functools
import math

import jax
import jax.numpy as jnp
import numpy as np
from jax import lax
from jax.experimental import pallas as pl
from jax.experimental.pallas import tpu as pltpu

HEAD_DIM = 128
N_HEADS_A = 6
N_HEADS_B = 4
N_HEADS_C = 6
DILATED_PATTERNS = ((128, 1), (512, 4), (2048, 16))
BLK = 128
NUM_BUCKETS = 32
MAX_DISTANCE = 2048
EPS = 1e-6
NEG_INF = -1e30

F32 = jnp.float32
BF16 = jnp.bfloat16

V7X_VMEM_BYTES = 64 * 1024 * 1024
BF16_SUBLANE_TILE = 16
F32_SIGN_BIT = 0x80000000

ONES_ROWS = BF16_SUBLANE_TILE

_VMEM_LIMIT = V7X_VMEM_BYTES * 7 // 8
PROJ_TM = 1024
PROJ_WEIGHT_SLOTS = 4
OUT_PROJ_TM = 512
MLP_TM, MLP_TF = 1024, 512
TQ_DIFF = 256
TQ_STICK = 256
STICK_HEADS_PER_STEP = 6
DILATED_LAG = 3
MERGE_ROWS = 256

_QA, _KA, _VA = 0, N_HEADS_A, 2 * N_HEADS_A
_QC = 3 * N_HEADS_A
_KC, _VC = _QC + N_HEADS_C, _QC + 2 * N_HEADS_C
_QB = _QC + 3 * N_HEADS_C
_KB, _VB = _QB + N_HEADS_B, _QB + 2 * N_HEADS_B


def _cparams(sem):
    return pltpu.CompilerParams(dimension_semantics=sem, vmem_limit_bytes=_VMEM_LIMIT)


def _dot_nt(a, b):
    return lax.dot_general(a, b, (((1,), (1,)), ((), ())), preferred_element_type=F32)


def _dot(a, b):
    return jnp.dot(a, b, preferred_element_type=F32)


def _staggered(n_chains, stages, states=None):
    states = [dict() for _ in range(n_chains)] if states is None else states
    for t in range(n_chains + len(stages) - 1):
        for k, stage in enumerate(stages):
            c = t - k
            if stage is not None and 0 <= c < n_chains:
                stage(c, states[c])
    return states


def _t5_bucket_np(dist):
    n = np.maximum(dist, 0)
    max_exact = NUM_BUCKETS // 2
    nf = np.maximum(n, max_exact).astype(np.float32)
    large = max_exact + (np.log(nf / np.float32(max_exact)) / np.float32(math.log(MAX_DISTANCE / max_exact))
                         * np.float32(NUM_BUCKETS - max_exact)).astype(np.int32)
    large = np.minimum(large, NUM_BUCKETS - 1)
    return np.where(n < max_exact, n, large).astype(np.int32)


def _bias_kernel(tab_ref, bkt_ref, o_ref, *, n_heads, tile_buckets):
    for t, present in enumerate(tile_buckets):
        b = bkt_ref[t]
        for h in range(n_heads):
            acc = jnp.full(b.shape, tab_ref[present[0], h], F32)
            for k in present[1:]:
                acc = jnp.where(b == k, tab_ref[k, h], acc)
            o_ref[h, t] = acc


def _build_bias(table, buckets):
    n_heads = table.shape[1]
    n, r, c = buckets.shape
    tile_buckets = tuple(tuple(int(k) for k in np.unique(buckets[t])) for t in range(n))
    return pl.pallas_call(
        functools.partial(_bias_kernel, n_heads=n_heads, tile_buckets=tile_buckets),
        in_specs=[pl.BlockSpec(memory_space=pltpu.SMEM),
                  pl.BlockSpec(memory_space=pltpu.VMEM)],
        out_specs=pl.BlockSpec(memory_space=pltpu.VMEM),
        out_shape=jax.ShapeDtypeStruct((n_heads, n, r, c), F32),
        compiler_params=pltpu.CompilerParams(vmem_limit_bytes=_VMEM_LIMIT),
        name="bias_table",
    )(table, jnp.asarray(buckets))


def _rms_proj_kernel(x_ref, g_ref, w_hbm, cs_ref, o_ref, xn_ref, w_ring, sems, *, layer, src_tiles):
    nj, depth = len(src_tiles), w_ring.shape[0]
    tn = w_ring.shape[2]
    step = pl.program_id(0) * nj + pl.program_id(1)
    n_steps = pl.num_programs(0) * nj

    def tile_copy(s):
        j = s % nj
        src = src_tiles[-1]
        for k in reversed(range(nj - 1)):
            src = jnp.where(j == k, src_tiles[k], src)
        slot = s % depth
        cols = pl.ds(pl.multiple_of(src * tn, tn), tn)
        return pltpu.make_async_copy(w_hbm.at[layer, :, cols], w_ring.at[slot], sems.at[slot])

    @pl.when(step == 0)
    def _():
        for s in range(depth - 1):
            tile_copy(s).start()

    @pl.when(step + depth - 1 < n_steps)
    def _():
        tile_copy(step + depth - 1).start()

    tile_copy(step).wait()

    def project(xn):
        acc = _dot(xn, w_ring[step % depth].astype(BF16))
        o_ref[...] = (acc * cs_ref[...]).astype(o_ref.dtype)

    @pl.when(pl.program_id(1) == 0)
    def _():
        x = x_ref[...]
        ms = jnp.mean(x * x, axis=-1, keepdims=True)
        xn = (x * lax.rsqrt(ms + EPS) * g_ref[...]).astype(BF16)
        xn_ref[...] = xn
        project(xn)

    @pl.when(pl.program_id(1) > 0)
    def _():
        project(xn_ref[...])


def _rms_proj(x, g, w, layer, colscale, src_tiles, *, tm=PROJ_TM):
    t, d = x.shape
    n = w.shape[2]
    tn = n // len(src_tiles)
    return pl.pallas_call(
        functools.partial(_rms_proj_kernel, layer=layer, src_tiles=src_tiles),
        grid=(t // tm, n // tn),
        in_specs=[pl.BlockSpec((tm, d), lambda i, j: (i, 0)),
                  pl.BlockSpec((1, d), lambda i, j: (0, 0)),
                  pl.BlockSpec(memory_space=pl.ANY),
                  pl.BlockSpec((1, tn), lambda i, j: (0, j))],
        out_specs=pl.BlockSpec((tm, tn), lambda i, j: (i, j)),
        out_shape=jax.ShapeDtypeStruct((t, n), BF16),
        scratch_shapes=[pltpu.VMEM((tm, d), BF16),
                        pltpu.VMEM((PROJ_WEIGHT_SLOTS, d, tn), w.dtype),
                        pltpu.SemaphoreType.DMA((PROJ_WEIGHT_SLOTS,))],
        compiler_params=_cparams(("arbitrary", "arbitrary")),
        name="rms_proj",
    )(x, g, w, colscale)


def _dilated_kernel(q_ref, k_ref, v_ref, bias_ref, o_ref, qf, kf, vf, acc_s, m_s, l_s, *, seq, lag):
    qf[...] = q_ref[...].astype(F32)
    kf[...] = k_ref[...].astype(F32)
    vf[...] = v_ref[...].astype(F32)

    row = lax.broadcasted_iota(jnp.int32, (BLK, 2 * BLK), 0)
    col = lax.broadcasted_iota(jnp.int32, (BLK, 2 * BLK), 1)
    band_mask = (col >= row) & (col <= row + BLK)
    row0 = lax.broadcasted_iota(jnp.int32, (BLK, BLK), 0)
    col0 = lax.broadcasted_iota(jnp.int32, (BLK, BLK), 1)
    diag_mask = col0 <= row0

    def rows_of(start, size, dil):
        return pl.ds(start, size, stride=dil) if dil > 1 else pl.ds(start, size)

    def ld(ref, start, size, dil):
        return ref[rows_of(start, size, dil), :].astype(BF16)

    chains = []
    for p_idx, (window, dil) in enumerate(DILATED_PATTERNS):
        assert window // dil == BLK
        for r in range(dil):
            for n in range(seq // dil // BLK):
                chains.append((p_idx, dil, r + dil * BLK * n, None if n == 0 else r + dil * BLK * (n - 1)))

    def scores(c, st):
        p_idx, dil, q0, k0 = chains[c]
        qb = ld(qf, q0, BLK, dil)
        if k0 is None:
            s = _dot_nt(qb, ld(kf, q0, BLK, dil)) + bias_ref[0, p_idx, :, BLK:]
            st["s"] = jnp.where(diag_mask, s, NEG_INF)
        else:
            s = _dot_nt(qb, ld(kf, k0, 2 * BLK, dil)) + bias_ref[0, p_idx]
            st["s"] = jnp.where(band_mask, s, NEG_INF)

    def softmax(c, st):
        s = st.pop("s")
        st["m"] = jnp.max(s, axis=-1, keepdims=True)
        p = jnp.exp(s - st["m"])
        st["l"] = jnp.sum(p, axis=-1, keepdims=True)
        st["p"] = p.astype(BF16)

    def values(c, st):
        p_idx, dil, q0, k0 = chains[c]
        vb = ld(vf, q0, BLK, dil) if k0 is None else ld(vf, k0, 2 * BLK, dil)
        rows = rows_of(q0, BLK, dil)
        acc_s[p_idx, rows, :] = _dot(st.pop("p"), vb)
        m_s[p_idx, rows, :] = jnp.broadcast_to(st.pop("m"), (BLK, HEAD_DIM))
        l_s[p_idx, rows, :] = jnp.broadcast_to(st.pop("l"), (BLK, HEAD_DIM))

    _staggered(len(chains), [scores] + [None] * (lag - 1) + [softmax] + [None] * (lag - 1) + [values])

    chunk = MERGE_ROWS

    def merge(ci, carry):
        rows = pl.ds(pl.multiple_of(ci * chunk, chunk), chunk)
        m0, m1, m2 = m_s[0, rows, :], m_s[1, rows, :], m_s[2, rows, :]
        mm = jnp.maximum(jnp.maximum(m0, m1), m2)
        w0, w1, w2 = jnp.exp(m0 - mm), jnp.exp(m1 - mm), jnp.exp(m2 - mm)
        num = w0 * acc_s[0, rows, :] + w1 * acc_s[1, rows, :] + w2 * acc_s[2, rows, :]
        den = w0 * l_s[0, rows, :] + w1 * l_s[1, rows, :] + w2 * l_s[2, rows, :]
        o_ref[rows, :] = (num / den).astype(o_ref.dtype)
        return carry

    lax.fori_loop(0, seq // chunk, merge, 0)


def _dilated_attention(proj, bias_a, *, batch, seq):
    t = proj.shape[0]
    n_pat = len(DILATED_PATTERNS)
    blk = lambda off: pl.BlockSpec((seq, HEAD_DIM), lambda b, h: (b, off + h))
    return pl.pallas_call(
        functools.partial(_dilated_kernel, seq=seq, lag=DILATED_LAG),
        grid=(batch, N_HEADS_A),
        in_specs=[blk(_QA), blk(_KA), blk(_VA),
                  pl.BlockSpec((1, n_pat, BLK, 2 * BLK), lambda b, h: (h, 0, 0, 0))],
        out_specs=pl.BlockSpec((seq, HEAD_DIM), lambda b, h: (b, h)),
        out_shape=jax.ShapeDtypeStruct((t, N_HEADS_A * HEAD_DIM), BF16),
        scratch_shapes=[pltpu.VMEM((seq, HEAD_DIM), F32)] * 3
                       + [pltpu.VMEM((n_pat, seq, HEAD_DIM), F32)] * 3,
        compiler_params=_cparams(("parallel", "parallel")),
        name="dilated_attn",
    )(proj, proj, proj, bias_a)


def _diff_kernel(q_ref, k_ref, v_ref, bias_ref, lam_ref, g_ref, o_ref, vt_ref, m_ref, acc_ref, *, tq, n_heads, seq,
                 lam_init):
    i = pl.program_id(1)
    half = HEAD_DIM // 2
    heads = [slice(g * HEAD_DIM, (g + 1) * HEAD_DIM) for g in range(n_heads)]

    @pl.when(i == 0)
    def _():
        r = lax.broadcasted_iota(jnp.int32, (HEAD_DIM, HEAD_DIM), 0)
        c = lax.broadcasted_iota(jnp.int32, (HEAD_DIM, HEAD_DIM), 1)
        eye = (r == c).astype(BF16)

        def transpose_block(jb, carry):
            rows = pl.ds(pl.multiple_of(jb * tq, tq), tq)
            for g, hs in enumerate(heads):
                vt_ref[g, jb, :HEAD_DIM] = _dot_nt(eye, v_ref[rows, hs]).astype(BF16)
                vt_ref[g, jb, HEAD_DIM:] = jnp.ones((ONES_ROWS, tq), BF16)
            return carry

        lax.fori_loop(0, seq // tq, transpose_block, 0)

    lane = lax.broadcasted_iota(jnp.int32, (tq, HEAD_DIM), 1)
    qqs = []
    for hs in heads:
        q = q_ref[:, hs]
        zero = jnp.zeros_like(q)
        qqs.append(jnp.concatenate([jnp.where(lane < half, q, zero), jnp.where(lane >= half, q, zero)], axis=0))
    key = lax.broadcasted_iota(jnp.int32, (tq, 2 * tq), 0)
    qry = lax.broadcasted_iota(jnp.int32, (tq, 2 * tq), 1)
    causal = key <= jnp.where(qry >= tq, qry - tq, qry)

    def step(js, diagonal_last=False):
        n_chains = len(js) * n_heads

        def scores(c, st):
            j, g = js[c // n_heads], c % n_heads
            rows = pl.ds(pl.multiple_of(j * tq, tq), tq)
            bias = bias_ref[g, i - j]
            s = _dot_nt(k_ref[rows, heads[g]], qqs[g]) + jnp.concatenate([bias, bias], axis=1)
            masked = diagonal_last and c // n_heads == len(js) - 1
            st["s"] = jnp.where(causal, s, NEG_INF) if masked else s

        def softmax(c, st):
            g = c % n_heads
            m = m_ref[g]
            s = st.pop("s")
            m_new = jnp.maximum(m, jnp.max(s, axis=0, keepdims=True))
            m_ref[g] = m_new
            st["alpha"] = jnp.exp(m - m_new)
            st["p"] = jnp.exp(s - m_new).astype(BF16)

        def values(c, st):
            g = c % n_heads
            vt = vt_ref[g, js[c // n_heads]]
            acc_ref[g] = st.pop("alpha") * acc_ref[g] + _dot(vt, st.pop("p"))

        _staggered(n_chains, [scores, softmax, values])

    m_ref[...] = jnp.full(m_ref.shape, NEG_INF, F32)
    acc_ref[...] = jnp.zeros(acc_ref.shape, F32)

    def triple(t, carry):
        step([3 * t, 3 * t + 1, 3 * t + 2])
        return carry

    lax.fori_loop(0, i // 3, triple, 0)

    @pl.when(i % 3 == 1)
    def _():
        step([i - 1])

    @pl.when(i % 3 == 2)
    def _():
        step([i - 2, i - 1])

    step([i], diagonal_last=True)

    lp = lam_ref[...]
    lam = (jnp.exp(jnp.sum(lp[0:1] * lp[1:2], axis=-1, keepdims=True))
           - jnp.exp(jnp.sum(lp[2:3] * lp[3:4], axis=-1, keepdims=True)) + lam_init)
    for g, hs in enumerate(heads):
        acc = acc_ref[g]
        o = acc[:HEAD_DIM] / acc[HEAD_DIM:HEAD_DIM + 1]
        out = o[:, :tq] - lam * o[:, tq:]
        ms = jnp.mean(out * out, axis=0, keepdims=True)
        y = out * lax.rsqrt(ms + EPS) * g_ref[...] * (1.0 - lam_init)
        o_ref[:, hs] = y.T.astype(o_ref.dtype)


def _diff_attention(proj, bias_b, lam_params, g, *, batch, seq, tq, lam_init):
    t = proj.shape[0]
    nq = seq // tq
    nh = N_HEADS_B
    width = nh * HEAD_DIM
    assert _QB % nh == 0 and _KB % nh == 0 and _VB % nh == 0
    kv = lambda off: pl.BlockSpec((seq, width), lambda b, i: (b, off // nh))
    return pl.pallas_call(
        functools.partial(_diff_kernel, tq=tq, n_heads=nh, seq=seq, lam_init=lam_init),
        grid=(batch, nq),
        in_specs=[pl.BlockSpec((tq, width), lambda b, i: (b * nq + i, _QB // nh)),
                  kv(_KB), kv(_VB),
                  pl.BlockSpec((nh, nq, tq, tq), lambda b, i: (0, 0, 0, 0)),
                  pl.BlockSpec((4, HEAD_DIM // 2), lambda b, i: (0, 0)),
                  pl.BlockSpec((HEAD_DIM, 1), lambda b, i: (0, 0))],
        out_specs=pl.BlockSpec((tq, width), lambda b, i: (b * nq + i, 0)),
        out_shape=jax.ShapeDtypeStruct((t, width), BF16),
        scratch_shapes=[pltpu.VMEM((nh, nq, HEAD_DIM + ONES_ROWS, tq), BF16),
                        pltpu.VMEM((nh, 1, 2 * tq), F32),
                        pltpu.VMEM((nh, HEAD_DIM + ONES_ROWS, 2 * tq), F32)],
        compiler_params=_cparams(("parallel", "arbitrary")),
        name="diff_attn",
    )(proj, proj, proj, bias_b, lam_params, g)


def _stick_kernel(q_ref, k_ref, v_ref, o_ref, c_ref, acc_ref, *, tq, n_heads):
    i = pl.program_id(2)
    row = lax.broadcasted_iota(jnp.int32, (tq, tq), 0)
    col = lax.broadcasted_iota(jnp.int32, (tq, tq), 1)
    strict = col < row
    suffix = (row >= col).astype(BF16)
    suffix2 = jnp.concatenate([suffix, suffix], axis=0)
    heads = [slice(g * HEAD_DIM, (g + 1) * HEAD_DIM) for g in range(n_heads)]
    qs = [q_ref[:, hs] for hs in heads]

    def blocks(js, diagonal_first=False):
        n_chains = len(js) * n_heads
        rows_of = [pl.ds(pl.multiple_of(j * tq, tq), tq) for j in js]

        def masked(c):
            return diagonal_first and c < n_heads

        def scores(c, st):
            st["z"] = _dot_nt(qs[c % n_heads], k_ref[rows_of[c // n_heads], heads[c % n_heads]])

        def log_break(c, st):
            z = st.pop("z")
            neg_abs = lax.bitcast_convert_type(
                lax.bitcast_convert_type(z, jnp.uint32) | jnp.uint32(F32_SIGN_BIT), F32)
            w = jnp.maximum(z, 0.0) + jnp.log(1.0 + jnp.exp(neg_abs))
            if masked(c):
                w = jnp.where(strict, w, 0.0)
            hi = w.astype(BF16)
            st["hilo"] = jnp.concatenate([hi, (w - hi.astype(F32)).astype(BF16)], axis=1)
            st["zc"] = z - c_ref[c % n_heads]

        def suffix_sums(c, st):
            st["incl"] = _dot(st.pop("hilo"), suffix2)
            c_ref[c % n_heads] += st["incl"][:, 0:1]

        def weights(c, st):
            a = jnp.exp(st.pop("zc") - st.pop("incl"))
            if masked(c):
                a = jnp.where(strict, a, 0.0)
            st["a"] = a.astype(BF16)

        def values(c, st):
            acc_ref[c % n_heads] += _dot(st.pop("a"), v_ref[rows_of[c // n_heads], heads[c % n_heads]])

        _staggered(n_chains, [scores, log_break, suffix_sums, weights, values])

    c_ref[...] = jnp.zeros(c_ref.shape, F32)
    acc_ref[...] = jnp.zeros(acc_ref.shape, F32)

    @pl.when(i == 0)
    def _():
        blocks([i], diagonal_first=True)

    @pl.when(i == 1)
    def _():
        blocks([i, i - 1], diagonal_first=True)

    @pl.when(i > 1)
    def _():
        blocks([i, i - 1, i - 2], diagonal_first=True)

        def triple(t, carry):
            blocks([i - 3 - 3 * t, i - 4 - 3 * t, i - 5 - 3 * t])
            return carry

        lax.fori_loop(0, (i - 2) // 3, triple, 0)

        @pl.when((i - 2) % 3 == 1)
        def _():
            blocks([0])

        @pl.when((i - 2) % 3 == 2)
        def _():
            blocks([1, 0])

    for g, hs in enumerate(heads):
        o_ref[:, hs] = acc_ref[g].astype(o_ref.dtype)


def _stick_attention(proj, *, batch, seq, tq, heads_per_step):
    t = proj.shape[0]
    nq = seq // tq
    g = heads_per_step
    assert g >= 2
    width = g * HEAD_DIM
    kv = lambda off: pl.BlockSpec((seq, width), lambda b, h, i: (b, off // g + h))
    return pl.pallas_call(
        functools.partial(_stick_kernel, tq=tq, n_heads=g),
        grid=(batch, N_HEADS_C // g, nq),
        in_specs=[pl.BlockSpec((tq, width), lambda b, h, i: (b * nq + i, _QC // g + h)),
                  kv(_KC), kv(_VC)],
        out_specs=pl.BlockSpec((tq, width), lambda b, h, i: (b * nq + i, h)),
        out_shape=jax.ShapeDtypeStruct((t, N_HEADS_C * HEAD_DIM), BF16),
        scratch_shapes=[pltpu.VMEM((g, tq, 1), F32), pltpu.VMEM((g, tq, HEAD_DIM), F32)],
        compiler_params=_cparams(("parallel", "parallel", "arbitrary")),
        name="stick_attn",
    )(proj, proj, proj)


def _out_proj_kernel(x_ref, a_ref, b_ref, c_ref, w_ref, o_ref):
    ka, kb = a_ref.shape[1], b_ref.shape[1]
    acc = (_dot(a_ref[...], w_ref[:ka].astype(BF16)) + _dot(b_ref[...], w_ref[ka:ka + kb].astype(BF16))
           + _dot(c_ref[...], w_ref[ka + kb:].astype(BF16)))
    o_ref[...] = x_ref[...] + acc


def _out_proj(x, ma, mb, mc, w, layer, *, tm=OUT_PROJ_TM):
    t, d = x.shape
    act = lambda k: pl.BlockSpec((tm, k), lambda i: (i, 0))
    return pl.pallas_call(
        _out_proj_kernel,
        grid=(t // tm,),
        in_specs=[act(d), act(ma.shape[1]), act(mb.shape[1]), act(mc.shape[1]),
                  pl.BlockSpec((None,) + w.shape[1:], lambda i: (layer, 0, 0))],
        out_specs=act(d),
        out_shape=jax.ShapeDtypeStruct((t, d), F32),
        compiler_params=_cparams(("parallel",)),
        name="out_proj",
    )(x, ma, mb, mc, w)


def _mlp_kernel(x_ref, g_ref, w1_ref, w2_ref, gf_ref, o_ref, xn_ref, *, final_norm):
    f = pl.program_id(1)

    def hidden_tile_update(xn):
        h = jnp.maximum(_dot(xn, w1_ref[...].astype(BF16)), 0.0)
        return _dot((h * h).astype(BF16), w2_ref[...].astype(BF16))

    @pl.when(f == 0)
    def _():
        x = x_ref[...]
        ms = jnp.mean(x * x, axis=-1, keepdims=True)
        xn = (x * lax.rsqrt(ms + EPS) * g_ref[...]).astype(BF16)
        xn_ref[...] = xn
        o_ref[...] = x + hidden_tile_update(xn)

    @pl.when(f > 0)
    def _():
        o_ref[...] += hidden_tile_update(xn_ref[...])

    if final_norm:
        @pl.when(f == pl.num_programs(1) - 1)
        def _():
            y = o_ref[...]
            ms = jnp.mean(y * y, axis=-1, keepdims=True)
            o_ref[...] = y * lax.rsqrt(ms + EPS) * gf_ref[...]


def _mlp(x, g, w1, w2, layer, g_final, *, final_norm, tm=MLP_TM, tf=MLP_TF):
    t, d = x.shape
    dff = w1.shape[2]
    return pl.pallas_call(
        functools.partial(_mlp_kernel, final_norm=final_norm),
        grid=(t // tm, dff // tf),
        in_specs=[pl.BlockSpec((tm, d), lambda i, f: (i, 0)),
                  pl.BlockSpec((1, d), lambda i, f: (0, 0)),
                  pl.BlockSpec((None, d, tf), lambda i, f: (layer, 0, f)),
                  pl.BlockSpec((None, tf, d), lambda i, f: (layer, f, 0)),
                  pl.BlockSpec((1, d), lambda i, f: (0, 0))],
        out_specs=pl.BlockSpec((tm, d), lambda i, f: (i, 0)),
        out_shape=jax.ShapeDtypeStruct((t, d), F32),
        scratch_shapes=[pltpu.VMEM((tm, d), BF16)],
        compiler_params=_cparams(("parallel", "arbitrary")),
        name="mlp",
    )(x, g, w1, w2, g_final)


def kernel(x, w_in, w_out, g_attn, g_mlp, w_mlp_in, w_mlp_out, rel_bias_table,
           diff_lam_q1, diff_lam_k1, diff_lam_q2, diff_lam_k2, diff_subln_g, g_final):
    batch, seq, d_model = x.shape
    depth = w_in.shape[0]

    i = np.arange(BLK)[:, None]
    c = np.arange(2 * BLK)[None, :]
    steps = i + BLK - c
    buckets_a = np.stack([_t5_bucket_np(steps * dil) for _, dil in DILATED_PATTERNS])
    dist = (np.arange(seq // TQ_DIFF)[:, None, None] * TQ_DIFF + np.arange(TQ_DIFF)[None, None, :]
            - np.arange(TQ_DIFF)[None, :, None])
    buckets_b = _t5_bucket_np(dist)
    bias_a = _build_bias(rel_bias_table[:, :N_HEADS_A], buckets_a)
    bias_b = _build_bias(rel_bias_table[:, N_HEADS_A:], buckets_b)

    colscale = np.ones((1, w_in.shape[2]), np.float32)
    colscale[0, _QA * HEAD_DIM:_KA * HEAD_DIM] = 1.0 / math.sqrt(HEAD_DIM)
    colscale[0, _QB * HEAD_DIM:_KB * HEAD_DIM] = 1.0 / math.sqrt(HEAD_DIM // 2)
    colscale[0, _QC * HEAD_DIM:_KC * HEAD_DIM] = 1.0 / math.sqrt(HEAD_DIM)
    colscale = jnp.asarray(colscale)

    tile = N_HEADS_A * HEAD_DIM
    assert N_HEADS_C == N_HEADS_A and (3 * N_HEADS_B * HEAD_DIM) % tile == 0
    tiles_a, tiles_b = 3, 3 * N_HEADS_B * HEAD_DIM // tile
    src_tiles = (tuple(range(tiles_a)) + tuple(range(tiles_a + tiles_b, 2 * tiles_a + tiles_b))
                 + tuple(range(tiles_a, tiles_a + tiles_b)))

    xf = x.reshape(batch * seq, d_model)
    for l in range(depth):
        lam_init = 0.8 - 0.6 * math.exp(-0.3 * l)
        lam_params = jnp.stack([diff_lam_q1[l], diff_lam_k1[l], diff_lam_q2[l], diff_lam_k2[l]]).astype(F32)
        proj = _rms_proj(xf, g_attn[l][None, :], w_in, l, colscale, src_tiles)
        ma = _dilated_attention(proj, bias_a, batch=batch, seq=seq)
        mb = _diff_attention(proj, bias_b, lam_params, diff_subln_g[l][:, None],
                             batch=batch, seq=seq, tq=TQ_DIFF, lam_init=lam_init)
        mc = _stick_attention(proj, batch=batch, seq=seq, tq=TQ_STICK, heads_per_step=STICK_HEADS_PER_STEP)
        xf = _out_proj(xf, ma, mb, mc, w_out, l)
        xf = _mlp(xf, g_mlp[l][None, :], w_mlp_in, w_mlp_out, l, g_final[None, :], final_norm=(l == depth - 1))
    return xf.reshape(batch, seq, d_model)
```

```python
import functools
import math

import jax
import jax.numpy as jnp
import numpy as np
from jax import lax
from jax.experimental import pallas as pl
from jax.experimental.pallas import tpu as pltpu

HEAD_DIM = 128
N_HEADS_A = 6
N_HEADS_B = 4
N_HEADS_C = 6
DILATED_PATTERNS = ((128, 1), (512, 4), (2048, 16))
BLK = 128
NUM_BUCKETS = 32
MAX_DISTANCE = 2048
EPS = 1e-6
NEG_INF = -1e30

F32 = jnp.float32
BF16 = jnp.bfloat16

V7X_VMEM_BYTES = 64 * 1024 * 1024
BF16_SUBLANE_TILE = 16
F32_SIGN_BIT = 0x80000000

ONES_ROWS = BF16_SUBLANE_TILE

_VMEM_LIMIT = V7X_VMEM_BYTES * 7 // 8
PROJ_TM = 1024
PROJ_WEIGHT_SLOTS = 3
OUT_PROJ_TM = 512
MLP_TM, MLP_TF = 1024, 512
TQ_DIFF = 256
TQ_STICK = 256
STICK_HEADS_PER_STEP = 6
DILATED_LAG = 3
MERGE_ROWS = 256

_QA, _KA, _VA = 0, N_HEADS_A, 2 * N_HEADS_A
_QC = 3 * N_HEADS_A
_KC, _VC = _QC + N_HEADS_C, _QC + 2 * N_HEADS_C
_QB = _QC + 3 * N_HEADS_C
_KB, _VB = _QB + N_HEADS_B, _QB + 2 * N_HEADS_B


def _cparams(sem):
    return pltpu.CompilerParams(dimension_semantics=sem, vmem_limit_bytes=_VMEM_LIMIT)


def _dot_nt(a, b):
    return lax.dot_general(a, b, (((1,), (1,)), ((), ())), preferred_element_type=F32)


def _dot(a, b):
    return jnp.dot(a, b, preferred_element_type=F32)


def _staggered(n_chains, stages, states=None):
    states = [dict() for _ in range(n_chains)] if states is None else states
    for t in range(n_chains + len(stages) - 1):
        for k, stage in enumerate(stages):
            c = t - k
            if stage is not None and 0 <= c < n_chains:
                stage(c, states[c])
    return states


def _t5_bucket_np(dist):
    n = np.maximum(dist, 0)
    max_exact = NUM_BUCKETS // 2
    nf = np.maximum(n, max_exact).astype(np.float32)
    large = max_exact + (np.log(nf / np.float32(max_exact)) / np.float32(math.log(MAX_DISTANCE / max_exact))
                         * np.float32(NUM_BUCKETS - max_exact)).astype(np.int32)
    large = np.minimum(large, NUM_BUCKETS - 1)
    return np.where(n < max_exact, n, large).astype(np.int32)


def _bias_kernel(tab_ref, bkt_ref, o_ref, *, n_heads, tile_buckets):
    for t, present in enumerate(tile_buckets):
        b = bkt_ref[t]
        for h in range(n_heads):
            acc = jnp.full(b.shape, tab_ref[present[0], h], F32)
            for k in present[1:]:
                acc = jnp.where(b == k, tab_ref[k, h], acc)
            o_ref[h, t] = acc


def _build_bias(table, buckets):
    n_heads = table.shape[1]
    n, r, c = buckets.shape
    tile_buckets = tuple(tuple(int(k) for k in np.unique(buckets[t])) for t in range(n))
    return pl.pallas_call(
        functools.partial(_bias_kernel, n_heads=n_heads, tile_buckets=tile_buckets),
        in_specs=[pl.BlockSpec(memory_space=pltpu.SMEM),
                  pl.BlockSpec(memory_space=pltpu.VMEM)],
        out_specs=pl.BlockSpec(memory_space=pltpu.VMEM),
        out_shape=jax.ShapeDtypeStruct((n_heads, n, r, c), F32),
        compiler_params=pltpu.CompilerParams(vmem_limit_bytes=_VMEM_LIMIT),
        name="bias_table",
    )(table, jnp.asarray(buckets))


def _rms_proj_kernel(x_ref, g_ref, w_hbm, cs_ref, o_ref, xn_ref, w_ring, sems, *, layer, src_tiles):
    nj, depth = len(src_tiles), w_ring.shape[0]
    tn = w_ring.shape[2]
    step = pl.program_id(0) * nj + pl.program_id(1)
    n_steps = pl.num_programs(0) * nj

    def tile_copy(s):
        j = s % nj
        src = src_tiles[-1]
        for k in reversed(range(nj - 1)):
            src = jnp.where(j == k, src_tiles[k], src)
        slot = s % depth
        cols = pl.ds(pl.multiple_of(src * tn, tn), tn)
        return pltpu.make_async_copy(w_hbm.at[layer, :, cols], w_ring.at[slot], sems.at[slot])

    @pl.when(step == 0)
    def _():
        for s in range(depth - 1):
            tile_copy(s).start()

    @pl.when(step + depth - 1 < n_steps)
    def _():
        tile_copy(step + depth - 1).start()

    tile_copy(step).wait()

    def project(xn):
        acc = _dot(xn, w_ring[step % depth].astype(BF16))
        o_ref[...] = (acc * cs_ref[...]).astype(o_ref.dtype)

    @pl.when(pl.program_id(1) == 0)
    def _():
        x = x_ref[...]
        ms = jnp.mean(x * x, axis=-1, keepdims=True)
        xn = (x * lax.rsqrt(ms + EPS) * g_ref[...]).astype(BF16)
        xn_ref[...] = xn
        project(xn)

    @pl.when(pl.program_id(1) > 0)
    def _():
        project(xn_ref[...])


def _rms_proj(x, g, w, layer, colscale, src_tiles, *, tm=PROJ_TM):
    t, d = x.shape
    n = w.shape[2]
    tn = n // len(src_tiles)
    return pl.pallas_call(
        functools.partial(_rms_proj_kernel, layer=layer, src_tiles=src_tiles),
        grid=(t // tm, n // tn),
        in_specs=[pl.BlockSpec((tm, d), lambda i, j: (i, 0)),
                  pl.BlockSpec((1, d), lambda i, j: (0, 0)),
                  pl.BlockSpec(memory_space=pl.ANY),
                  pl.BlockSpec((1, tn), lambda i, j: (0, j))],
        out_specs=pl.BlockSpec((tm, tn), lambda i, j: (i, j)),
        out_shape=jax.ShapeDtypeStruct((t, n), BF16),
        scratch_shapes=[pltpu.VMEM((tm, d), BF16),
                        pltpu.VMEM((PROJ_WEIGHT_SLOTS, d, tn), w.dtype),
                        pltpu.SemaphoreType.DMA((PROJ_WEIGHT_SLOTS,))],
        compiler_params=_cparams(("arbitrary", "arbitrary")),
        name="rms_proj",
    )(x, g, w, colscale)


def _dilated_kernel(q_ref, k_ref, v_ref, bias_ref, o_ref, qf, kf, vf, acc_s, m_s, l_s, *, seq, lag):
    qf[...] = q_ref[...].astype(F32)
    kf[...] = k_ref[...].astype(F32)
    vf[...] = v_ref[...].astype(F32)

    row = lax.broadcasted_iota(jnp.int32, (BLK, 2 * BLK), 0)
    col = lax.broadcasted_iota(jnp.int32, (BLK, 2 * BLK), 1)
    band_mask = (col >= row) & (col <= row + BLK)
    row0 = lax.broadcasted_iota(jnp.int32, (BLK, BLK), 0)
    col0 = lax.broadcasted_iota(jnp.int32, (BLK, BLK), 1)
    diag_mask = col0 <= row0

    def rows_of(start, size, dil):
        return pl.ds(start, size, stride=dil) if dil > 1 else pl.ds(start, size)

    def ld(ref, start, size, dil):
        return ref[rows_of(start, size, dil), :].astype(BF16)

    chains = []
    for p_idx, (window, dil) in enumerate(DILATED_PATTERNS):
        assert window // dil == BLK
        for r in range(dil):
            for n in range(seq // dil // BLK):
                chains.append((p_idx, dil, r + dil * BLK * n, None if n == 0 else r + dil * BLK * (n - 1)))

    def scores(c, st):
        p_idx, dil, q0, k0 = chains[c]
        qb = ld(qf, q0, BLK, dil)
        if k0 is None:
            s = _dot_nt(qb, ld(kf, q0, BLK, dil)) + bias_ref[0, p_idx, :, BLK:]
            st["s"] = jnp.where(diag_mask, s, NEG_INF)
        else:
            s = _dot_nt(qb, ld(kf, k0, 2 * BLK, dil)) + bias_ref[0, p_idx]
            st["s"] = jnp.where(band_mask, s, NEG_INF)

    def softmax(c, st):
        s = st.pop("s")
        st["m"] = jnp.max(s, axis=-1, keepdims=True)
        p = jnp.exp(s - st["m"])
        st["l"] = jnp.sum(p, axis=-1, keepdims=True)
        st["p"] = p.astype(BF16)

    def values(c, st):
        p_idx, dil, q0, k0 = chains[c]
        vb = ld(vf, q0, BLK, dil) if k0 is None else ld(vf, k0, 2 * BLK, dil)
        rows = pl.ds(q0 * (BLK + 1), BLK) if dil % 8 == 0 else rows_of(q0, BLK, dil)
        acc_s[p_idx, rows, :] = _dot(st.pop("p"), vb)
        m_s[p_idx, rows, :] = jnp.broadcast_to(st.pop("m"), (BLK, HEAD_DIM))
        l_s[p_idx, rows, :] = jnp.broadcast_to(st.pop("l"), (BLK, HEAD_DIM))

    _staggered(len(chains), [scores] + [None] * (lag - 1) + [softmax] + [None] * (lag - 1) + [values])

    chunk = MERGE_ROWS

    def merge(ci, carry):
        rows = pl.ds(pl.multiple_of(ci * chunk, chunk), chunk)
        def rows_in(ref, p_idx):
            dil = DILATED_PATTERNS[p_idx][1]
            if dil % 8:
                return ref[p_idx, rows, :]
            return jnp.concatenate([ref[p_idx, pl.ds(ci * (chunk // dil) + m, dil, stride=BLK + 1), :]
                                    for m in range(chunk // dil)], axis=0)

        m0, m1, m2 = (rows_in(m_s, p) for p in range(3))
        mm = jnp.maximum(jnp.maximum(m0, m1), m2)
        w0, w1, w2 = jnp.exp(m0 - mm), jnp.exp(m1 - mm), jnp.exp(m2 - mm)
        num = w0 * rows_in(acc_s, 0) + w1 * rows_in(acc_s, 1) + w2 * rows_in(acc_s, 2)
        den = w0 * rows_in(l_s, 0) + w1 * rows_in(l_s, 1) + w2 * rows_in(l_s, 2)
        o_ref[rows, :] = (num / den).astype(o_ref.dtype)
        return carry

    lax.fori_loop(0, seq // chunk, merge, 0)


def _dilated_attention(proj, bias_a, *, batch, seq):
    t = proj.shape[0]
    n_pat = len(DILATED_PATTERNS)
    blk = lambda off: pl.BlockSpec((seq, HEAD_DIM), lambda b, h: (b, off + h))
    return pl.pallas_call(
        functools.partial(_dilated_kernel, seq=seq, lag=DILATED_LAG),
        grid=(batch, N_HEADS_A),
        in_specs=[blk(_QA), blk(_KA), blk(_VA),
                  pl.BlockSpec((1, n_pat, BLK, 2 * BLK), lambda b, h: (h, 0, 0, 0))],
        out_specs=pl.BlockSpec((seq, HEAD_DIM), lambda b, h: (b, h)),
        out_shape=jax.ShapeDtypeStruct((t, N_HEADS_A * HEAD_DIM), BF16),
        scratch_shapes=[pltpu.VMEM((seq, HEAD_DIM), F32)] * 3
                       + [pltpu.VMEM((n_pat, seq + BLK, HEAD_DIM), F32)] * 3,
        compiler_params=_cparams(("parallel", "parallel")),
        name="dilated_attn",
    )(proj, proj, proj, bias_a)


def _diff_kernel(q_ref, k_ref, v_ref, bias_ref, lam_ref, g_ref, o_ref, vt_ref, m_ref, acc_ref, *, tq, n_heads, seq,
                 lam_init):
    i = pl.program_id(1)
    half = HEAD_DIM // 2
    heads = [slice(g * HEAD_DIM, (g + 1) * HEAD_DIM) for g in range(n_heads)]

    @pl.when(i == 0)
    def _():
        r = lax.broadcasted_iota(jnp.int32, (HEAD_DIM, HEAD_DIM), 0)
        c = lax.broadcasted_iota(jnp.int32, (HEAD_DIM, HEAD_DIM), 1)
        eye = (r == c).astype(BF16)

        def transpose_block(jb, carry):
            rows = pl.ds(pl.multiple_of(jb * tq, tq), tq)
            for g, hs in enumerate(heads):
                vt_ref[g, jb, :HEAD_DIM] = _dot_nt(eye, v_ref[rows, hs]).astype(BF16)
                vt_ref[g, jb, HEAD_DIM:] = jnp.ones((ONES_ROWS, tq), BF16)
            return carry

        lax.fori_loop(0, seq // tq, transpose_block, 0)

    lane = lax.broadcasted_iota(jnp.int32, (tq, HEAD_DIM), 1)
    qqs = []
    for hs in heads:
        q = q_ref[:, hs]
        zero = jnp.zeros_like(q)
        qqs.append(jnp.concatenate([jnp.where(lane < half, q, zero), jnp.where(lane >= half, q, zero)], axis=0))
    key = lax.broadcasted_iota(jnp.int32, (tq, 2 * tq), 0)
    qry = lax.broadcasted_iota(jnp.int32, (tq, 2 * tq), 1)
    causal = key <= jnp.where(qry >= tq, qry - tq, qry)

    def step(js, diagonal_last=False):
        n_chains = len(js) * n_heads

        def scores(c, st):
            j, g = js[c // n_heads], c % n_heads
            rows = pl.ds(pl.multiple_of(j * tq, tq), tq)
            bias = bias_ref[g, i - j]
            s = _dot_nt(k_ref[rows, heads[g]], qqs[g]) + jnp.concatenate([bias, bias], axis=1)
            masked = diagonal_last and c // n_heads == len(js) - 1
            st["s"] = jnp.where(causal, s, NEG_INF) if masked else s

        def softmax(c, st):
            g = c % n_heads
            m = m_ref[g]
            s = st.pop("s")
            m_new = jnp.maximum(m, jnp.max(s, axis=0, keepdims=True))
            m_ref[g] = m_new
            st["alpha"] = jnp.exp(m - m_new)
            st["p"] = jnp.exp(s - m_new).astype(BF16)

        def values(c, st):
            g = c % n_heads
            vt = vt_ref[g, js[c // n_heads]]
            acc_ref[g] = st.pop("alpha") * acc_ref[g] + _dot(vt, st.pop("p"))

        _staggered(n_chains, [scores, softmax, values])

    m_ref[...] = jnp.full(m_ref.shape, NEG_INF, F32)
    acc_ref[...] = jnp.zeros(acc_ref.shape, F32)

    def triple(t, carry):
        step([3 * t, 3 * t + 1, 3 * t + 2])
        return carry

    lax.fori_loop(0, i // 3, triple, 0)

    @pl.when(i % 3 == 1)
    def _():
        step([i - 1])

    @pl.when(i % 3 == 2)
    def _():
        step([i - 2, i - 1])

    step([i], diagonal_last=True)

    lp = lam_ref[...]
    lam = (jnp.exp(jnp.sum(lp[0:1] * lp[1:2], axis=-1, keepdims=True))
           - jnp.exp(jnp.sum(lp[2:3] * lp[3:4], axis=-1, keepdims=True)) + lam_init)
    for g, hs in enumerate(heads):
        acc = acc_ref[g]
        o = acc[:HEAD_DIM] / acc[HEAD_DIM:HEAD_DIM + 1]
        out = o[:, :tq] - lam * o[:, tq:]
        ms = jnp.mean(out * out, axis=0, keepdims=True)
        y = out * lax.rsqrt(ms + EPS) * g_ref[...] * (1.0 - lam_init)
        o_ref[:, hs] = y.T.astype(o_ref.dtype)


def _diff_attention(proj, bias_b, lam_params, g, *, batch, seq, tq, lam_init):
    t = proj.shape[0]
    nq = seq // tq
    nh = N_HEADS_B
    width = nh * HEAD_DIM
    assert _QB % nh == 0 and _KB % nh == 0 and _VB % nh == 0
    kv = lambda off: pl.BlockSpec((seq, width), lambda b, i: (b, off // nh))
    return pl.pallas_call(
        functools.partial(_diff_kernel, tq=tq, n_heads=nh, seq=seq, lam_init=lam_init),
        grid=(batch, nq),
        in_specs=[pl.BlockSpec((tq, width), lambda b, i: (b * nq + i, _QB // nh)),
                  kv(_KB), kv(_VB),
                  pl.BlockSpec((nh, nq, tq, tq), lambda b, i: (0, 0, 0, 0)),
                  pl.BlockSpec((4, HEAD_DIM // 2), lambda b, i: (0, 0)),
                  pl.BlockSpec((HEAD_DIM, 1), lambda b, i: (0, 0))],
        out_specs=pl.BlockSpec((tq, width), lambda b, i: (b * nq + i, 0)),
        out_shape=jax.ShapeDtypeStruct((t, width), BF16),
        scratch_shapes=[pltpu.VMEM((nh, nq, HEAD_DIM + ONES_ROWS, tq), BF16),
                        pltpu.VMEM((nh, 1, 2 * tq), F32),
                        pltpu.VMEM((nh, HEAD_DIM + ONES_ROWS, 2 * tq), F32)],
        compiler_params=_cparams(("parallel", "arbitrary")),
        name="diff_attn",
    )(proj, proj, proj, bias_b, lam_params, g)


def _stick_kernel(q_ref, k_ref, v_ref, o_ref, c_ref, acc_ref, *, tq, n_heads):
    i = pl.program_id(2)
    row = lax.broadcasted_iota(jnp.int32, (tq, tq), 0)
    col = lax.broadcasted_iota(jnp.int32, (tq, tq), 1)
    strict = col < row
    suffix = (row >= col).astype(BF16)
    suffix2 = jnp.concatenate([suffix, suffix], axis=0)
    heads = [slice(g * HEAD_DIM, (g + 1) * HEAD_DIM) for g in range(n_heads)]
    qs = [q_ref[:, hs] for hs in heads]

    def blocks(js, diagonal_first=False):
        n_chains = len(js) * n_heads
        rows_of = [pl.ds(pl.multiple_of(j * tq, tq), tq) for j in js]

        def masked(c):
            return diagonal_first and c < n_heads

        def scores(c, st):
            st["z"] = _dot_nt(qs[c % n_heads], k_ref[rows_of[c // n_heads], heads[c % n_heads]])

        def log_break(c, st):
            z = st.pop("z")
            neg_abs = lax.bitcast_convert_type(
                lax.bitcast_convert_type(z, jnp.uint32) | jnp.uint32(F32_SIGN_BIT), F32)
            w = jnp.maximum(z, 0.0) + jnp.log(1.0 + jnp.exp(neg_abs))
            if masked(c):
                w = jnp.where(strict, w, 0.0)
            hi = w.astype(BF16)
            st["hilo"] = jnp.concatenate([hi, (w - hi.astype(F32)).astype(BF16)], axis=1)
            st["zc"] = z - c_ref[c % n_heads]

        def suffix_sums(c, st):
            st["incl"] = _dot(st.pop("hilo"), suffix2)
            c_ref[c % n_heads] += st["incl"][:, 0:1]

        def weights(c, st):
            a = jnp.exp(st.pop("zc") - st.pop("incl"))
            if masked(c):
                a = jnp.where(strict, a, 0.0)
            st["a"] = a.astype(BF16)

        def values(c, st):
            acc_ref[c % n_heads] += _dot(st.pop("a"), v_ref[rows_of[c // n_heads], heads[c % n_heads]])

        _staggered(n_chains, [scores, log_break, suffix_sums, weights, values])

    c_ref[...] = jnp.zeros(c_ref.shape, F32)
    acc_ref[...] = jnp.zeros(acc_ref.shape, F32)

    @pl.when(i == 0)
    def _():
        blocks([i], diagonal_first=True)

    @pl.when(i == 1)
    def _():
        blocks([i, i - 1], diagonal_first=True)

    @pl.when(i > 1)
    def _():
        blocks([i, i - 1, i - 2], diagonal_first=True)

        def triple(t, carry):
            blocks([i - 3 - 3 * t, i - 4 - 3 * t, i - 5 - 3 * t])
            return carry

        lax.fori_loop(0, (i - 2) // 3, triple, 0)

        @pl.when((i - 2) % 3 == 1)
        def _():
            blocks([0])

        @pl.when((i - 2) % 3 == 2)
        def _():
            blocks([1, 0])

    for g, hs in enumerate(heads):
        o_ref[:, hs] = acc_ref[g].astype(o_ref.dtype)


def _stick_attention(proj, *, batch, seq, tq, heads_per_step):
    t = proj.shape[0]
    nq = seq // tq
    g = heads_per_step
    assert g >= 2
    width = g * HEAD_DIM
    kv = lambda off: pl.BlockSpec((seq, width), lambda b, h, i: (b, off // g + h))
    return pl.pallas_call(
        functools.partial(_stick_kernel, tq=tq, n_heads=g),
        grid=(batch, N_HEADS_C // g, nq),
        in_specs=[pl.BlockSpec((tq, width), lambda b, h, i: (b * nq + i, _QC // g + h)),
                  kv(_KC), kv(_VC)],
        out_specs=pl.BlockSpec((tq, width), lambda b, h, i: (b * nq + i, h)),
        out_shape=jax.ShapeDtypeStruct((t, N_HEADS_C * HEAD_DIM), BF16),
        scratch_shapes=[pltpu.VMEM((g, tq, 1), F32), pltpu.VMEM((g, tq, HEAD_DIM), F32)],
        compiler_params=_cparams(("parallel", "parallel", "arbitrary")),
        name="stick_attn",
    )(proj, proj, proj)


def _out_proj_kernel(x_ref, a_ref, b_ref, c_ref, w_ref, o_ref):
    ka, kb = a_ref.shape[1], b_ref.shape[1]
    acc = (_dot(a_ref[...], w_ref[:ka].astype(BF16)) + _dot(b_ref[...], w_ref[ka:ka + kb].astype(BF16))
           + _dot(c_ref[...], w_ref[ka + kb:].astype(BF16)))
    o_ref[...] = x_ref[...] + acc


def _out_proj(x, ma, mb, mc, w, layer, *, tm=OUT_PROJ_TM):
    t, d = x.shape
    act = lambda k: pl.BlockSpec((tm, k), lambda i: (i, 0))
    return pl.pallas_call(
        _out_proj_kernel,
        grid=(t // tm,),
        in_specs=[act(d), act(ma.shape[1]), act(mb.shape[1]), act(mc.shape[1]),
                  pl.BlockSpec((None,) + w.shape[1:], lambda i: (layer, 0, 0))],
        out_specs=act(d),
        out_shape=jax.ShapeDtypeStruct((t, d), F32),
        compiler_params=_cparams(("parallel",)),
        name="out_proj",
    )(x, ma, mb, mc, w)


def _mlp_kernel(x_ref, g_ref, w1_ref, w2_ref, gf_ref, o_ref, xn_ref, *, final_norm):
    f = pl.program_id(1)

    def hidden_tile_update(xn):
        h = jnp.maximum(_dot(xn, w1_ref[...].astype(BF16)), 0.0)
        return _dot((h * h).astype(BF16), w2_ref[...].astype(BF16))

    @pl.when(f == 0)
    def _():
        x = x_ref[...]
        ms = jnp.mean(x * x, axis=-1, keepdims=True)
        xn = (x * lax.rsqrt(ms + EPS) * g_ref[...]).astype(BF16)
        xn_ref[...] = xn
        o_ref[...] = x + hidden_tile_update(xn)

    @pl.when(f > 0)
    def _():
        o_ref[...] += hidden_tile_update(xn_ref[...])

    if final_norm:
        @pl.when(f == pl.num_programs(1) - 1)
        def _():
            y = o_ref[...]
            ms = jnp.mean(y * y, axis=-1, keepdims=True)
            o_ref[...] = y * lax.rsqrt(ms + EPS) * gf_ref[...]


def _mlp(x, g, w1, w2, layer, g_final, *, final_norm, tm=MLP_TM, tf=MLP_TF):
    t, d = x.shape
    dff = w1.shape[2]
    return pl.pallas_call(
        functools.partial(_mlp_kernel, final_norm=final_norm),
        grid=(t // tm, dff // tf),
        in_specs=[pl.BlockSpec((tm, d), lambda i, f: (i, 0)),
                  pl.BlockSpec((1, d), lambda i, f: (0, 0)),
                  pl.BlockSpec((None, d, tf), lambda i, f: (layer, 0, f)),
                  pl.BlockSpec((None, tf, d), lambda i, f: (layer, f, 0)),
                  pl.BlockSpec((1, d), lambda i, f: (0, 0))],
        out_specs=pl.BlockSpec((tm, d), lambda i, f: (i, 0)),
        out_shape=jax.ShapeDtypeStruct((t, d), F32),
        scratch_shapes=[pltpu.VMEM((tm, d), BF16)],
        compiler_params=_cparams(("parallel", "arbitrary")),
        name="mlp",
    )(x, g, w1, w2, g_final)


def kernel(x, w_in, w_out, g_attn, g_mlp, w_mlp_in, w_mlp_out, rel_bias_table,
           diff_lam_q1, diff_lam_k1, diff_lam_q2, diff_lam_k2, diff_subln_g, g_final):
    batch, seq, d_model = x.shape
    depth = w_in.shape[0]

    i = np.arange(BLK)[:, None]
    c = np.arange(2 * BLK)[None, :]
    steps = i + BLK - c
    buckets_a = np.stack([_t5_bucket_np(steps * dil) for _, dil in DILATED_PATTERNS])
    dist = (np.arange(seq // TQ_DIFF)[:, None, None] * TQ_DIFF + np.arange(TQ_DIFF)[None, None, :]
            - np.arange(TQ_DIFF)[None, :, None])
    buckets_b = _t5_bucket_np(dist)
    bias_a = _build_bias(rel_bias_table[:, :N_HEADS_A], buckets_a)
    bias_b = _build_bias(rel_bias_table[:, N_HEADS_A:], buckets_b)

    colscale = np.ones((1, w_in.shape[2]), np.float32)
    colscale[0, _QA * HEAD_DIM:_KA * HEAD_DIM] = 1.0 / math.sqrt(HEAD_DIM)
    colscale[0, _QB * HEAD_DIM:_KB * HEAD_DIM] = 1.0 / math.sqrt(HEAD_DIM // 2)
    colscale[0, _QC * HEAD_DIM:_KC * HEAD_DIM] = 1.0 / math.sqrt(HEAD_DIM)
    colscale = jnp.asarray(colscale)

    tile = N_HEADS_A * HEAD_DIM
    assert N_HEADS_C == N_HEADS_A and (3 * N_HEADS_B * HEAD_DIM) % tile == 0
    tiles_a, tiles_b = 3, 3 * N_HEADS_B * HEAD_DIM // tile
    src_tiles = (tuple(range(tiles_a)) + tuple(range(tiles_a + tiles_b, 2 * tiles_a + tiles_b))
                 + tuple(range(tiles_a, tiles_a + tiles_b)))

    xf = x.reshape(batch * seq, d_model)
    for l in range(depth):
        lam_init = 0.8 - 0.6 * math.exp(-0.3 * l)
        lam_params = jnp.stack([diff_lam_q1[l], diff_lam_k1[l], diff_lam_q2[l], diff_lam_k2[l]]).astype(F32)
        proj = _rms_proj(xf, g_attn[l][None, :], w_in, l, colscale, src_tiles)
        ma = _dilated_attention(proj, bias_a, batch=batch, seq=seq)
        mb = _diff_attention(proj, bias_b, lam_params, diff_subln_g[l][:, None],
                             batch=batch, seq=seq, tq=TQ_DIFF, lam_init=lam_init)
        mc = _stick_attention(proj, batch=batch, seq=seq, tq=TQ_STICK, heads_per_step=STICK_HEADS_PER_STEP)
        xf = _out_proj(xf, ma, mb, mc, w_out, l)
        xf = _mlp(xf, g_mlp[l][None, :], w_mlp_in, w_mlp_out, l, g_final[None, :], final_norm=(l == depth - 1))
    return xf.reshape(batch, seq, d_model)
```

```python
import functools
import math

import jax
import jax.numpy as jnp
import numpy as np
from jax import lax
from jax.experimental import pallas as pl
from jax.experimental.pallas import tpu as pltpu

HEAD_DIM = 128
N_HEADS_A = 6
N_HEADS_B = 4
N_HEADS_C = 6
DILATED_PATTERNS = ((128, 1), (512, 4), (2048, 16))
BLK = 128
NUM_BUCKETS = 32
MAX_DISTANCE = 2048
EPS = 1e-6
NEG_INF = -1e30

F32 = jnp.float32
BF16 = jnp.bfloat16

V7X_VMEM_BYTES = 64 * 1024 * 1024
BF16_SUBLANE_TILE = 16
F32_SIGN_BIT = 0x80000000

ONES_ROWS = BF16_SUBLANE_TILE

_VMEM_LIMIT = V7X_VMEM_BYTES * 7 // 8
PROJ_TM = 1024
PROJ_WEIGHT_SLOTS = 3
OUT_PROJ_TM = 512
MLP_TM, MLP_TF = 1024, 512
TQ_DIFF = 256
TQ_STICK = 256
STICK_HEADS_PER_STEP = 6
DILATED_LAG = 3
MERGE_ROWS = 256

_QA, _KA, _VA = 0, N_HEADS_A, 2 * N_HEADS_A
_QC = 3 * N_HEADS_A
_KC, _VC = _QC + N_HEADS_C, _QC + 2 * N_HEADS_C
_QB = _QC + 3 * N_HEADS_C
_KB, _VB = _QB + N_HEADS_B, _QB + 2 * N_HEADS_B


def _cparams(sem):
    return pltpu.CompilerParams(dimension_semantics=sem, vmem_limit_bytes=_VMEM_LIMIT)


def _dot_nt(a, b):
    return lax.dot_general(a, b, (((1,), (1,)), ((), ())), preferred_element_type=F32)


def _dot(a, b):
    return jnp.dot(a, b, preferred_element_type=F32)


def _staggered(n_chains, stages, states=None):
    states = [dict() for _ in range(n_chains)] if states is None else states
    for t in range(n_chains + len(stages) - 1):
        for k, stage in enumerate(stages):
            c = t - k
            if stage is not None and 0 <= c < n_chains:
                stage(c, states[c])
    return states


def _t5_bucket_np(dist):
    n = np.maximum(dist, 0)
    max_exact = NUM_BUCKETS // 2
    nf = np.maximum(n, max_exact).astype(np.float32)
    large = max_exact + (np.log(nf / np.float32(max_exact)) / np.float32(math.log(MAX_DISTANCE / max_exact))
                         * np.float32(NUM_BUCKETS - max_exact)).astype(np.int32)
    large = np.minimum(large, NUM_BUCKETS - 1)
    return np.where(n < max_exact, n, large).astype(np.int32)


def _bias_kernel(tab_ref, bkt_ref, o_ref, *, n_heads, tile_buckets):
    for t, present in enumerate(tile_buckets):
        b = bkt_ref[t]
        for h in range(n_heads):
            acc = jnp.full(b.shape, tab_ref[present[0], h], F32)
            for k in present[1:]:
                acc = jnp.where(b == k, tab_ref[k, h], acc)
            o_ref[h, t] = acc


def _build_bias(table, buckets):
    n_heads = table.shape[1]
    n, r, c = buckets.shape
    tile_buckets = tuple(tuple(int(k) for k in np.unique(buckets[t])) for t in range(n))
    return pl.pallas_call(
        functools.partial(_bias_kernel, n_heads=n_heads, tile_buckets=tile_buckets),
        in_specs=[pl.BlockSpec(memory_space=pltpu.SMEM),
                  pl.BlockSpec(memory_space=pltpu.VMEM)],
        out_specs=pl.BlockSpec(memory_space=pltpu.VMEM),
        out_shape=jax.ShapeDtypeStruct((n_heads, n, r, c), F32),
        compiler_params=pltpu.CompilerParams(vmem_limit_bytes=_VMEM_LIMIT),
        name="bias_table",
    )(table, jnp.asarray(buckets))


def _rms_proj_kernel(x_ref, g_ref, w_hbm, cs_ref, o_ref, xn_ref, w_ring, sems, *, layer, src_tiles):
    nj, depth = len(src_tiles), w_ring.shape[0]
    tn = w_ring.shape[2]
    step = pl.program_id(0) * nj + pl.program_id(1)
    n_steps = pl.num_programs(0) * nj

    def tile_copy(s):
        j = s % nj
        src = src_tiles[-1]
        for k in reversed(range(nj - 1)):
            src = jnp.where(j == k, src_tiles[k], src)
        slot = s % depth
        cols = pl.ds(pl.multiple_of(src * tn, tn), tn)
        return pltpu.make_async_copy(w_hbm.at[layer, :, cols], w_ring.at[slot], sems.at[slot])

    @pl.when(step == 0)
    def _():
        for s in range(depth - 1):
            tile_copy(s).start()

    @pl.when(step + depth - 1 < n_steps)
    def _():
        tile_copy(step + depth - 1).start()

    tile_copy(step).wait()

    def project(xn):
        acc = _dot(xn, w_ring[step % depth].astype(BF16))
        o_ref[...] = (acc * cs_ref[...]).astype(o_ref.dtype)

    @pl.when(pl.program_id(1) == 0)
    def _():
        x = x_ref[...]
        ms = jnp.mean(x * x, axis=-1, keepdims=True)
        xn = (x * lax.rsqrt(ms + EPS) * g_ref[...]).astype(BF16)
        xn_ref[...] = xn
        project(xn)

    @pl.when(pl.program_id(1) > 0)
    def _():
        project(xn_ref[...])


def _rms_proj(x, g, w, layer, colscale, src_tiles, *, tm=PROJ_TM):
    t, d = x.shape
    n = w.shape[2]
    tn = n // len(src_tiles)
    return pl.pallas_call(
        functools.partial(_rms_proj_kernel, layer=layer, src_tiles=src_tiles),
        grid=(t // tm, n // tn),
        in_specs=[pl.BlockSpec((tm, d), lambda i, j: (i, 0)),
                  pl.BlockSpec((1, d), lambda i, j: (0, 0)),
                  pl.BlockSpec(memory_space=pl.ANY),
                  pl.BlockSpec((1, tn), lambda i, j: (0, j))],
        out_specs=pl.BlockSpec((tm, tn), lambda i, j: (i, j)),
        out_shape=jax.ShapeDtypeStruct((t, n), BF16),
        scratch_shapes=[pltpu.VMEM((tm, d), BF16),
                        pltpu.VMEM((PROJ_WEIGHT_SLOTS, d, tn), w.dtype),
                        pltpu.SemaphoreType.DMA((PROJ_WEIGHT_SLOTS,))],
        compiler_params=_cparams(("arbitrary", "arbitrary")),
        name="rms_proj",
    )(x, g, w, colscale)


def _dilated_kernel(q_ref, k_ref, v_ref, bias_ref, o_ref, qf, kf, vf, qr, kr, vr, acc_s, m_s, l_s, *, seq, lag):
    qf[...] = q_ref[...].astype(F32)
    kf[...] = k_ref[...].astype(F32)
    vf[...] = v_ref[...].astype(F32)

    residue_major = {id(qf): qr, id(kf): kr, id(vf): vr}
    for _, dil in DILATED_PATTERNS:
        if dil % 8 == 0:
            assert seq // dil == BLK
            for src, dst in ((qf, qr), (kf, kr), (vf, vr)):
                for m in range(BLK):
                    dst[pl.ds(m, dil, stride=BLK + 1), :] = src[pl.ds(dil * m, dil), :]

    row = lax.broadcasted_iota(jnp.int32, (BLK, 2 * BLK), 0)
    col = lax.broadcasted_iota(jnp.int32, (BLK, 2 * BLK), 1)
    band_mask = (col >= row) & (col <= row + BLK)
    row0 = lax.broadcasted_iota(jnp.int32, (BLK, BLK), 0)
    col0 = lax.broadcasted_iota(jnp.int32, (BLK, BLK), 1)
    diag_mask = col0 <= row0

    def rows_of(start, size, dil):
        return pl.ds(start, size, stride=dil) if dil > 1 else pl.ds(start, size)

    def ld(ref, start, size, dil):
        if dil % 8 == 0:
            return residue_major[id(ref)][pl.ds(start * (BLK + 1), size), :].astype(BF16)
        return ref[rows_of(start, size, dil), :].astype(BF16)

    chains = []
    for p_idx, (window, dil) in enumerate(DILATED_PATTERNS):
        assert window // dil == BLK
        for r in range(dil):
            for n in range(seq // dil // BLK):
                chains.append((p_idx, dil, r + dil * BLK * n, None if n == 0 else r + dil * BLK * (n - 1)))

    def scores(c, st):
        p_idx, dil, q0, k0 = chains[c]
        qb = ld(qf, q0, BLK, dil)
        if k0 is None:
            s = _dot_nt(qb, ld(kf, q0, BLK, dil)) + bias_ref[0, p_idx, :, BLK:]
            st["s"] = jnp.where(diag_mask, s, NEG_INF)
        else:
            s = _dot_nt(qb, ld(kf, k0, 2 * BLK, dil)) + bias_ref[0, p_idx]
            st["s"] = jnp.where(band_mask, s, NEG_INF)

    def softmax(c, st):
        s = st.pop("s")
        st["m"] = jnp.max(s, axis=-1, keepdims=True)
        p = jnp.exp(s - st["m"])
        st["l"] = jnp.sum(p, axis=-1, keepdims=True)
        st["p"] = p.astype(BF16)

    def values(c, st):
        p_idx, dil, q0, k0 = chains[c]
        vb = ld(vf, q0, BLK, dil) if k0 is None else ld(vf, k0, 2 * BLK, dil)
        rows = pl.ds(q0 * (BLK + 1), BLK) if dil % 8 == 0 else rows_of(q0, BLK, dil)
        acc_s[p_idx, rows, :] = _dot(st.pop("p"), vb)
        m_s[p_idx, rows, :] = jnp.broadcast_to(st.pop("m"), (BLK, HEAD_DIM))
        l_s[p_idx, rows, :] = jnp.broadcast_to(st.pop("l"), (BLK, HEAD_DIM))

    _staggered(len(chains), [scores] + [None] * (lag - 1) + [softmax] + [None] * (lag - 1) + [values])

    chunk = MERGE_ROWS

    def merge(ci, carry):
        rows = pl.ds(pl.multiple_of(ci * chunk, chunk), chunk)
        def rows_in(ref, p_idx):
            dil = DILATED_PATTERNS[p_idx][1]
            if dil % 8:
                return ref[p_idx, rows, :]
            return jnp.concatenate([ref[p_idx, pl.ds(ci * (chunk // dil) + m, dil, stride=BLK + 1), :]
                                    for m in range(chunk // dil)], axis=0)

        m0, m1, m2 = (rows_in(m_s, p) for p in range(3))
        mm = jnp.maximum(jnp.maximum(m0, m1), m2)
        w0, w1, w2 = jnp.exp(m0 - mm), jnp.exp(m1 - mm), jnp.exp(m2 - mm)
        num = w0 * rows_in(acc_s, 0) + w1 * rows_in(acc_s, 1) + w2 * rows_in(acc_s, 2)
        den = w0 * rows_in(l_s, 0) + w1 * rows_in(l_s, 1) + w2 * rows_in(l_s, 2)
        o_ref[rows, :] = (num / den).astype(o_ref.dtype)
        return carry

    lax.fori_loop(0, seq // chunk, merge, 0)


def _dilated_attention(proj, bias_a, *, batch, seq):
    t = proj.shape[0]
    n_pat = len(DILATED_PATTERNS)
    blk = lambda off: pl.BlockSpec((seq, HEAD_DIM), lambda b, h: (b, off + h))
    return pl.pallas_call(
        functools.partial(_dilated_kernel, seq=seq, lag=DILATED_LAG),
        grid=(batch, N_HEADS_A),
        in_specs=[blk(_QA), blk(_KA), blk(_VA),
                  pl.BlockSpec((1, n_pat, BLK, 2 * BLK), lambda b, h: (h, 0, 0, 0))],
        out_specs=pl.BlockSpec((seq, HEAD_DIM), lambda b, h: (b, h)),
        out_shape=jax.ShapeDtypeStruct((t, N_HEADS_A * HEAD_DIM), BF16),
        scratch_shapes=[pltpu.VMEM((seq, HEAD_DIM), F32)] * 3
                       + [pltpu.VMEM((seq + BLK, HEAD_DIM), F32)] * 3
                       + [pltpu.VMEM((n_pat, seq + BLK, HEAD_DIM), F32)] * 3,
        compiler_params=_cparams(("parallel", "parallel")),
        name="dilated_attn",
    )(proj, proj, proj, bias_a)


def _diff_kernel(q_ref, k_ref, v_ref, bias_ref, lam_ref, g_ref, o_ref, vt_ref, m_ref, acc_ref, *, tq, n_heads, seq,
                 lam_init):
    i = pl.program_id(1)
    half = HEAD_DIM // 2
    heads = [slice(g * HEAD_DIM, (g + 1) * HEAD_DIM) for g in range(n_heads)]

    @pl.when(i == 0)
    def _():
        r = lax.broadcasted_iota(jnp.int32, (HEAD_DIM, HEAD_DIM), 0)
        c = lax.broadcasted_iota(jnp.int32, (HEAD_DIM, HEAD_DIM), 1)
        eye = (r == c).astype(BF16)

        def transpose_block(jb, carry):
            rows = pl.ds(pl.multiple_of(jb * tq, tq), tq)
            for g, hs in enumerate(heads):
                vt_ref[g, jb, :HEAD_DIM] = _dot_nt(eye, v_ref[rows, hs]).astype(BF16)
                vt_ref[g, jb, HEAD_DIM:] = jnp.ones((ONES_ROWS, tq), BF16)
            return carry

        lax.fori_loop(0, seq // tq, transpose_block, 0)

    lane = lax.broadcasted_iota(jnp.int32, (tq, HEAD_DIM), 1)
    qqs = []
    for hs in heads:
        q = q_ref[:, hs]
        zero = jnp.zeros_like(q)
        qqs.append(jnp.concatenate([jnp.where(lane < half, q, zero), jnp.where(lane >= half, q, zero)], axis=0))
    key = lax.broadcasted_iota(jnp.int32, (tq, 2 * tq), 0)
    qry = lax.broadcasted_iota(jnp.int32, (tq, 2 * tq), 1)
    causal = key <= jnp.where(qry >= tq, qry - tq, qry)

    def step(js, diagonal_last=False):
        n_chains = len(js) * n_heads

        def scores(c, st):
            j, g = js[c // n_heads], c % n_heads
            rows = pl.ds(pl.multiple_of(j * tq, tq), tq)
            bias = bias_ref[g, i - j]
            s = _dot_nt(k_ref[rows, heads[g]], qqs[g]) + jnp.concatenate([bias, bias], axis=1)
            masked = diagonal_last and c // n_heads == len(js) - 1
            st["s"] = jnp.where(causal, s, NEG_INF) if masked else s

        def softmax(c, st):
            g = c % n_heads
            m = m_ref[g]
            s = st.pop("s")
            m_new = jnp.maximum(m, jnp.max(s, axis=0, keepdims=True))
            m_ref[g] = m_new
            st["alpha"] = jnp.exp(m - m_new)
            st["p"] = jnp.exp(s - m_new).astype(BF16)

        def values(c, st):
            g = c % n_heads
            vt = vt_ref[g, js[c // n_heads]]
            acc_ref[g] = st.pop("alpha") * acc_ref[g] + _dot(vt, st.pop("p"))

        _staggered(n_chains, [scores, softmax, values])

    m_ref[...] = jnp.full(m_ref.shape, NEG_INF, F32)
    acc_ref[...] = jnp.zeros(acc_ref.shape, F32)

    def triple(t, carry):
        step([3 * t, 3 * t + 1, 3 * t + 2])
        return carry

    lax.fori_loop(0, i // 3, triple, 0)

    @pl.when(i % 3 == 1)
    def _():
        step([i - 1])

    @pl.when(i % 3 == 2)
    def _():
        step([i - 2, i - 1])

    step([i], diagonal_last=True)

    lp = lam_ref[...]
    lam = (jnp.exp(jnp.sum(lp[0:1] * lp[1:2], axis=-1, keepdims=True))
           - jnp.exp(jnp.sum(lp[2:3] * lp[3:4], axis=-1, keepdims=True)) + lam_init)
    for g, hs in enumerate(heads):
        acc = acc_ref[g]
        o = acc[:HEAD_DIM] / acc[HEAD_DIM:HEAD_DIM + 1]
        out = o[:, :tq] - lam * o[:, tq:]
        ms = jnp.mean(out * out, axis=0, keepdims=True)
        y = out * lax.rsqrt(ms + EPS) * g_ref[...] * (1.0 - lam_init)
        o_ref[:, hs] = y.T.astype(o_ref.dtype)


def _diff_attention(proj, bias_b, lam_params, g, *, batch, seq, tq, lam_init):
    t = proj.shape[0]
    nq = seq // tq
    nh = N_HEADS_B
    width = nh * HEAD_DIM
    assert _QB % nh == 0 and _KB % nh == 0 and _VB % nh == 0
    kv = lambda off: pl.BlockSpec((seq, width), lambda b, i: (b, off // nh))
    return pl.pallas_call(
        functools.partial(_diff_kernel, tq=tq, n_heads=nh, seq=seq, lam_init=lam_init),
        grid=(batch, nq),
        in_specs=[pl.BlockSpec((tq, width), lambda b, i: (b * nq + i, _QB // nh)),
                  kv(_KB), kv(_VB),
                  pl.BlockSpec((nh, nq, tq, tq), lambda b, i: (0, 0, 0, 0)),
                  pl.BlockSpec((4, HEAD_DIM // 2), lambda b, i: (0, 0)),
                  pl.BlockSpec((HEAD_DIM, 1), lambda b, i: (0, 0))],
        out_specs=pl.BlockSpec((tq, width), lambda b, i: (b * nq + i, 0)),
        out_shape=jax.ShapeDtypeStruct((t, width), BF16),
        scratch_shapes=[pltpu.VMEM((nh, nq, HEAD_DIM + ONES_ROWS, tq), BF16),
                        pltpu.VMEM((nh, 1, 2 * tq), F32),
                        pltpu.VMEM((nh, HEAD_DIM + ONES_ROWS, 2 * tq), F32)],
        compiler_params=_cparams(("parallel", "arbitrary")),
        name="diff_attn",
    )(proj, proj, proj, bias_b, lam_params, g)


def _stick_kernel(q_ref, k_ref, v_ref, o_ref, c_ref, acc_ref, *, tq, n_heads):
    i = pl.program_id(2)
    row = lax.broadcasted_iota(jnp.int32, (tq, tq), 0)
    col = lax.broadcasted_iota(jnp.int32, (tq, tq), 1)
    strict = col < row
    suffix = (row >= col).astype(BF16)
    suffix2 = jnp.concatenate([suffix, suffix], axis=0)
    heads = [slice(g * HEAD_DIM, (g + 1) * HEAD_DIM) for g in range(n_heads)]
    qs = [q_ref[:, hs] for hs in heads]

    def blocks(js, diagonal_first=False):
        n_chains = len(js) * n_heads
        rows_of = [pl.ds(pl.multiple_of(j * tq, tq), tq) for j in js]

        def masked(c):
            return diagonal_first and c < n_heads

        def scores(c, st):
            st["z"] = _dot_nt(qs[c % n_heads], k_ref[rows_of[c // n_heads], heads[c % n_heads]])

        def log_break(c, st):
            z = st.pop("z")
            neg_abs = lax.bitcast_convert_type(
                lax.bitcast_convert_type(z, jnp.uint32) | jnp.uint32(F32_SIGN_BIT), F32)
            w = jnp.maximum(z, 0.0) + jnp.log(1.0 + jnp.exp(neg_abs))
            if masked(c):
                w = jnp.where(strict, w, 0.0)
            hi = w.astype(BF16)
            st["hilo"] = jnp.concatenate([hi, (w - hi.astype(F32)).astype(BF16)], axis=1)
            st["zc"] = z - c_ref[c % n_heads]

        def suffix_sums(c, st):
            st["incl"] = _dot(st.pop("hilo"), suffix2)
            c_ref[c % n_heads] += st["incl"][:, 0:1]

        def weights(c, st):
            a = jnp.exp(st.pop("zc") - st.pop("incl"))
            if masked(c):
                a = jnp.where(strict, a, 0.0)
            st["a"] = a.astype(BF16)

        def values(c, st):
            acc_ref[c % n_heads] += _dot(st.pop("a"), v_ref[rows_of[c // n_heads], heads[c % n_heads]])

        _staggered(n_chains, [scores, log_break, suffix_sums, weights, values])

    c_ref[...] = jnp.zeros(c_ref.shape, F32)
    acc_ref[...] = jnp.zeros(acc_ref.shape, F32)

    @pl.when(i == 0)
    def _():
        blocks([i], diagonal_first=True)

    @pl.when(i == 1)
    def _():
        blocks([i, i - 1], diagonal_first=True)

    @pl.when(i > 1)
    def _():
        blocks([i, i - 1, i - 2], diagonal_first=True)

        def triple(t, carry):
            blocks([i - 3 - 3 * t, i - 4 - 3 * t, i - 5 - 3 * t])
            return carry

        lax.fori_loop(0, (i - 2) // 3, triple, 0)

        @pl.when((i - 2) % 3 == 1)
        def _():
            blocks([0])

        @pl.when((i - 2) % 3 == 2)
        def _():
            blocks([1, 0])

    for g, hs in enumerate(heads):
        o_ref[:, hs] = acc_ref[g].astype(o_ref.dtype)


def _stick_attention(proj, *, batch, seq, tq, heads_per_step):
    t = proj.shape[0]
    nq = seq // tq
    g = heads_per_step
    assert g >= 2
    width = g * HEAD_DIM
    kv = lambda off: pl.BlockSpec((seq, width), lambda b, h, i: (b, off // g + h))
    return pl.pallas_call(
        functools.partial(_stick_kernel, tq=tq, n_heads=g),
        grid=(batch, N_HEADS_C // g, nq),
        in_specs=[pl.BlockSpec((tq, width), lambda b, h, i: (b * nq + i, _QC // g + h)),
                  kv(_KC), kv(_VC)],
        out_specs=pl.BlockSpec((tq, width), lambda b, h, i: (b * nq + i, h)),
        out_shape=jax.ShapeDtypeStruct((t, N_HEADS_C * HEAD_DIM), BF16),
        scratch_shapes=[pltpu.VMEM((g, tq, 1), F32), pltpu.VMEM((g, tq, HEAD_DIM), F32)],
        compiler_params=_cparams(("parallel", "parallel", "arbitrary")),
        name="stick_attn",
    )(proj, proj, proj)


def _out_proj_kernel(x_ref, a_ref, b_ref, c_ref, w_ref, o_ref):
    ka, kb = a_ref.shape[1], b_ref.shape[1]
    acc = (_dot(a_ref[...], w_ref[:ka].astype(BF16)) + _dot(b_ref[...], w_ref[ka:ka + kb].astype(BF16))
           + _dot(c_ref[...], w_ref[ka + kb:].astype(BF16)))
    o_ref[...] = x_ref[...] + acc


def _out_proj(x, ma, mb, mc, w, layer, *, tm=OUT_PROJ_TM):
    t, d = x.shape
    act = lambda k: pl.BlockSpec((tm, k), lambda i: (i, 0))
    return pl.pallas_call(
        _out_proj_kernel,
        grid=(t // tm,),
        in_specs=[act(d), act(ma.shape[1]), act(mb.shape[1]), act(mc.shape[1]),
                  pl.BlockSpec((None,) + w.shape[1:], lambda i: (layer, 0, 0))],
        out_specs=act(d),
        out_shape=jax.ShapeDtypeStruct((t, d), F32),
        compiler_params=_cparams(("parallel",)),
        name="out_proj",
    )(x, ma, mb, mc, w)


def _mlp_kernel(x_ref, g_ref, w1_ref, w2_ref, gf_ref, o_ref, xn_ref, *, final_norm):
    f = pl.program_id(1)

    def hidden_tile_update(xn):
        h = jnp.maximum(_dot(xn, w1_ref[...].astype(BF16)), 0.0)
        return _dot((h * h).astype(BF16), w2_ref[...].astype(BF16))

    @pl.when(f == 0)
    def _():
        x = x_ref[...]
        ms = jnp.mean(x * x, axis=-1, keepdims=True)
        xn = (x * lax.rsqrt(ms + EPS) * g_ref[...]).astype(BF16)
        xn_ref[...] = xn
        o_ref[...] = x + hidden_tile_update(xn)

    @pl.when(f > 0)
    def _():
        o_ref[...] += hidden_tile_update(xn_ref[...])

    if final_norm:
        @pl.when(f == pl.num_programs(1) - 1)
        def _():
            y = o_ref[...]
            ms = jnp.mean(y * y, axis=-1, keepdims=True)
            o_ref[...] = y * lax.rsqrt(ms + EPS) * gf_ref[...]


def _mlp(x, g, w1, w2, layer, g_final, *, final_norm, tm=MLP_TM, tf=MLP_TF):
    t, d = x.shape
    dff = w1.shape[2]
    return pl.pallas_call(
        functools.partial(_mlp_kernel, final_norm=final_norm),
        grid=(t // tm, dff // tf),
        in_specs=[pl.BlockSpec((tm, d), lambda i, f: (i, 0)),
                  pl.BlockSpec((1, d), lambda i, f: (0, 0)),
                  pl.BlockSpec((None, d, tf), lambda i, f: (layer, 0, f)),
                  pl.BlockSpec((None, tf, d), lambda i, f: (layer, f, 0)),
                  pl.BlockSpec((1, d), lambda i, f: (0, 0))],
        out_specs=pl.BlockSpec((tm, d), lambda i, f: (i, 0)),
        out_shape=jax.ShapeDtypeStruct((t, d), F32),
        scratch_shapes=[pltpu.VMEM((tm, d), BF16)],
        compiler_params=_cparams(("parallel", "arbitrary")),
        name="mlp",
    )(x, g, w1, w2, g_final)


def kernel(x, w_in, w_out, g_attn, g_mlp, w_mlp_in, w_mlp_out, rel_bias_table,
           diff_lam_q1, diff_lam_k1, diff_lam_q2, diff_lam_k2, diff_subln_g, g_final):
    batch, seq, d_model = x.shape
    depth = w_in.shape[0]

    i = np.arange(BLK)[:, None]
    c = np.arange(2 * BLK)[None, :]
    steps = i + BLK - c
    buckets_a = np.stack([_t5_bucket_np(steps * dil) for _, dil in DILATED_PATTERNS])
    dist = (np.arange(seq // TQ_DIFF)[:, None, None] * TQ_DIFF + np.arange(TQ_DIFF)[None, None, :]
            - np.arange(TQ_DIFF)[None, :, None])
    buckets_b = _t5_bucket_np(dist)
    bias_a = _build_bias(rel_bias_table[:, :N_HEADS_A], buckets_a)
    bias_b = _build_bias(rel_bias_table[:, N_HEADS_A:], buckets_b)

    colscale = np.ones((1, w_in.shape[2]), np.float32)
    colscale[0, _QA * HEAD_DIM:_KA * HEAD_DIM] = 1.0 / math.sqrt(HEAD_DIM)
    colscale[0, _QB * HEAD_DIM:_KB * HEAD_DIM] = 1.0 / math.sqrt(HEAD_DIM // 2)
    colscale[0, _QC * HEAD_DIM:_KC * HEAD_DIM] = 1.0 / math.sqrt(HEAD_DIM)
    colscale = jnp.asarray(colscale)

    tile = N_HEADS_A * HEAD_DIM
    assert N_HEADS_C == N_HEADS_A and (3 * N_HEADS_B * HEAD_DIM) % tile == 0
    tiles_a, tiles_b = 3, 3 * N_HEADS_B * HEAD_DIM // tile
    src_tiles = (tuple(range(tiles_a)) + tuple(range(tiles_a + tiles_b, 2 * tiles_a + tiles_b))
                 + tuple(range(tiles_a, tiles_a + tiles_b)))

    xf = x.reshape(batch * seq, d_model)
    for l in range(depth):
        lam_init = 0.8 - 0.6 * math.exp(-0.3 * l)
        lam_params = jnp.stack([diff_lam_q1[l], diff_lam_k1[l], diff_lam_q2[l], diff_lam_k2[l]]).astype(F32)
        proj = _rms_proj(xf, g_attn[l][None, :], w_in, l, colscale, src_tiles)
        ma = _dilated_attention(proj, bias_a, batch=batch, seq=seq)
        mb = _diff_attention(proj, bias_b, lam_params, diff_subln_g[l][:, None],
                             batch=batch, seq=seq, tq=TQ_DIFF, lam_init=lam_init)
        mc = _stick_attention(proj, batch=batch, seq=seq, tq=TQ_STICK, heads_per_step=STICK_HEADS_PER_STEP)
        xf = _out_proj(xf, ma, mb, mc, w_out, l)
        xf = _mlp(xf, g_mlp[l][None, :], w_mlp_in, w_mlp_out, l, g_final[None, :], final_norm=(l == depth - 1))
    return xf.reshape(batch, seq, d_model)
```
